```python
import jax
import jax.numpy as jnp
from jax import lax
import numpy as np

D_MODEL = 1024
BATCH = 16
SEQ = 2048
DEPTH = 4

GRID_W = 64
CTX_LEN = 256
HEAD_DIM = 64
NA_HEADS = 6
NA_WIN_ROWS = 8
NA_WIN_COLS = 16
GQA_HEADS = 6
GQA_KV_HEADS = 2
Q_BLOCK = 128
ROPE_THETA = 10000.0
HG_HEADS = 4
HG_CHUNK = 64
LB_FLOOR = 1e-30
WA = NA_HEADS * HEAD_DIM
WB = GQA_HEADS * HEAD_DIM
WKV = GQA_KV_HEADS * HEAD_DIM
WC = HG_HEADS * HEAD_DIM
MIX_WIDTH = WA + WB + WC
COL_LAYOUT = (('a_k', WA), ('a_v', WA), ('b_k', WKV), ('b_v', WKV), ('c_f', WC), ('c_b', WC), ('c_i', WC),
              ('a_q', WA), ('b_q', WB), ('c_q', WC), ('c_g', WC))
KEY_SIDE_COLS = 2 * WA + 2 * WKV + 3 * WC
IN_COLS = KEY_SIDE_COLS + WA + WB + 2 * WC
D_FF = 2816
CONV_W = 3
N_MOD = 6
DEEPNORM_ALPHA = (2.0 * DEPTH) ** 0.25
DEEPNORM_BETA = (8.0 * DEPTH) ** -0.25
LN_EPS = 1e-6
RMS_EPS = 1e-6

kernel_name = 'hybrid_natten_gqa_hgrn2_dit_block'


def layer_norm(x, g=None, b=None):
    xf = x.astype(jnp.float32)
    mu = jnp.mean(xf, -1, keepdims=True)
    var = jnp.mean(jnp.square(xf - mu), -1, keepdims=True)
    y = (xf - mu) * lax.rsqrt(var + LN_EPS)
    if g is not None:
        y = y * g.astype(jnp.float32) + b.astype(jnp.float32)
    return y.astype(x.dtype)


def rms_norm(x, g):
    xf = x.astype(jnp.float32)
    y = xf * lax.rsqrt(jnp.mean(jnp.square(xf), -1, keepdims=True) + RMS_EPS)
    return (y * g.astype(jnp.float32)).astype(x.dtype)


def modulate(x, shift, scale):
    return layer_norm(x) * (1 + scale) + shift


def split_heads(t):
    return t.reshape(t.shape[0], t.shape[1], -1, HEAD_DIM)


def split_projection(y):
    parts, off = {}, 0
    for name, width in COL_LAYOUT:
        if off + width > y.shape[-1]:
            break
        parts[name] = split_heads(y[..., off:off + width])
        off += width
    return parts


def axial_rope_tables(n):
    t = jnp.arange(n)
    pos = jnp.stack([t // GRID_W, t % GRID_W]).astype(jnp.float32)
    n_freq = HEAD_DIM // 4
    inv_freq = ROPE_THETA ** (-jnp.arange(n_freq, dtype=jnp.float32) / n_freq)
    ang = pos[:, :, None] * inv_freq
    return jnp.cos(ang), jnp.sin(ang)


def apply_axial_rope(x, cos, sin):
    B, T, H, d = x.shape
    xs = x.reshape(B, T, H, 2, 2, d // 4)
    cos = jnp.swapaxes(cos, 0, 1)[:, None].astype(x.dtype)
    sin = jnp.swapaxes(sin, 0, 1)[:, None].astype(x.dtype)
    x1, x2 = xs[..., 0, :], xs[..., 1, :]
    out = jnp.stack([x1 * cos - x2 * sin, x2 * cos + x1 * sin], axis=-2)
    return out.reshape(B, T, H, d)


def block_attention(q, k, v):
    B, T, Hq, d = q.shape
    Hkv = k.shape[2]
    G = Hq // Hkv
    scale = d ** -0.5
    qb = q.reshape(B, T // Q_BLOCK, Q_BLOCK, Hkv, G, d).transpose(1, 0, 2, 3, 4, 5)

    def one_block(q_blk):
        s = jnp.einsum('bqkgd,bskd->bkgqs', q_blk, k).astype(jnp.float32) * scale
        p = jax.nn.softmax(s, axis=-1).astype(v.dtype)
        return jnp.einsum('bkgqs,bskd->bqkgd', p, v)

    o = lax.map(one_block, qb)
    return o.transpose(1, 0, 2, 3, 4, 5).reshape(B, T, Hq * d)


def neighbourhood_attention(q, k, v, k_ctx, v_ctx, rpb):
    B, N, H, d = q.shape
    rows = N // GRID_W
    kr = min(NA_WIN_ROWS, rows)
    kc = NA_WIN_COLS
    n_loc = kr * kc
    scale = d ** -0.5
    cols = np.arange(GRID_W)
    col_start = np.clip(cols - kc // 2, 0, GRID_W - kc)
    col_idx = (col_start[:, None] + np.arange(kc)[None, :]).astype(np.int32)
    dc_idx = (col_idx - cols[:, None] + (NA_WIN_COLS - 1)).astype(np.int32)
    kg = k.reshape(B, rows, GRID_W, H, d)
    vg = v.reshape(B, rows, GRID_W, H, d)
    qg = q.reshape(B, rows, GRID_W, H, d).transpose(1, 0, 2, 3, 4)

    def one_row(args):
        r, q_row = args
        r_start = jnp.clip(r - kr // 2, 0, rows - kr)
        k_nb = lax.dynamic_slice_in_dim(kg, r_start, kr, axis=1)[:, :, col_idx]
        v_nb = lax.dynamic_slice_in_dim(vg, r_start, kr, axis=1)[:, :, col_idx]
        dr_idx = r_start + jnp.arange(kr) - r + (NA_WIN_ROWS - 1)
        bias = rpb[:, dr_idx[:, None, None], dc_idx[None]]
        s_loc = (jnp.einsum('bwhd,brwjhd->bhwrj', q_row, k_nb).astype(jnp.float32) * scale
                 + bias.transpose(0, 2, 1, 3).astype(jnp.float32))
        s_ctx = jnp.einsum('bwhd,blhd->bhwl', q_row, k_ctx).astype(jnp.float32) * scale
        s = jnp.concatenate([s_loc.reshape(B, H, GRID_W, n_loc), s_ctx], axis=-1)
        p = jax.nn.softmax(s, axis=-1).astype(v.dtype)
        p_loc = p[..., :n_loc].reshape(B, H, GRID_W, kr, kc)
        return (jnp.einsum('bhwrj,brwjhd->bwhd', p_loc, v_nb)
                + jnp.einsum('bhwl,blhd->bwhd', p[..., n_loc:], v_ctx))

    o = lax.map(one_row, (jnp.arange(rows), qg))
    return o.transpose(1, 0, 2, 3, 4).reshape(B, N, H * d)


def forget_log(z, lb):
    lb = lb.astype(jnp.float32)
    return jnp.logaddexp(jnp.log(jnp.maximum(lb, LB_FLOOR)),
                         jnp.log1p(-lb) + jax.nn.log_sigmoid(z.astype(jnp.float32)))


def gla_chunkwise(q, k, v, logf, s0):
    B, T, H, dk = q.shape
    C = HG_CHUNK
    n = T // C

    def to_chunks(a):
        return a.reshape(B, n, C, H, a.shape[-1]).transpose(1, 0, 3, 2, 4)

    causal = jnp.tril(jnp.ones((C, C), bool))[:, :, None]

    def step(S, inp):
        qc, kc, vc, gc = inp
        b = jnp.cumsum(gc, axis=-2)
        diff = b[..., :, None, :] - b[..., None, :, :]
        decay = jnp.where(causal, jnp.exp(jnp.where(causal, diff, 0.0)), 0.0)
        att = jnp.einsum('bhtk,bhsk,bhtsk->bhts', qc, kc, decay)
        o = jnp.einsum('bhts,bhsv->bhtv', att, vc) + jnp.einsum('bhtk,bhkv->bhtv', qc * jnp.exp(b), S)
        b_last = b[..., -1:, :]
        S = S * jnp.exp(b_last[..., 0, :])[..., None] + jnp.einsum('bhsk,bhsv->bhkv', kc * jnp.exp(b_last - b), vc)
        return S, o

    S, o = lax.scan(step, s0, (to_chunks(q), to_chunks(k), to_chunks(v), to_chunks(logf)))
    o = o.transpose(1, 0, 3, 2, 4).reshape(B, T, H, v.shape[-1])
    return o.astype(v.dtype), S


def hgrn2_scan(q, v, logf, s0, reverse):
    k = -jnp.expm1(logf)
    if reverse:
        q, k, v, logf = (jnp.flip(a, axis=1) for a in (q, k, v, logf))
    o, s = gla_chunkwise(q, k, v, logf, s0)
    if reverse:
        o = jnp.flip(o, axis=1)
    return o, s


def hgrn2_final_state(v, logf, reverse):
    k = -jnp.expm1(logf)
    if reverse:
        k, v, logf = (jnp.flip(a, axis=1) for a in (k, v, logf))
    suffix = lax.cumsum(logf, axis=1, reverse=True) - logf
    return jnp.einsum('bthk,bthv->bhkv', k * jnp.exp(suffix), v)


def hgrn2_readout(o, g, gn_g):
    y = rms_norm(o, gn_g) * jax.nn.silu(g)
    return y.reshape(y.shape[0], y.shape[1], WC)


def token_mixer(h, hc, w_in, w_out, rpb, q_g, k_g, lb, gn_g, rope_cos, rope_sin, with_ctx_out):
    lat = split_projection(h @ w_in)
    ctx = split_projection(hc @ (w_in if with_ctx_out else w_in[:, :KEY_SIDE_COLS]))
    o_a = neighbourhood_attention(lat['a_q'], lat['a_k'], lat['a_v'], ctx['a_k'], ctx['a_v'], rpb)
    kc_b = rms_norm(ctx['b_k'], k_g)
    q_b = apply_axial_rope(rms_norm(lat['b_q'], q_g), rope_cos, rope_sin)
    k_b = jnp.concatenate([kc_b, apply_axial_rope(rms_norm(lat['b_k'], k_g), rope_cos, rope_sin)], axis=1)
    v_b = jnp.concatenate([ctx['b_v'], lat['b_v']], axis=1)
    o_b = block_attention(q_b, k_b, v_b)
    lb_f = lb[0].reshape(HG_HEADS, HEAD_DIM)
    lb_b = lb[1].reshape(HG_HEADS, HEAD_DIM)
    q_scale = HEAD_DIM ** -0.5
    logf_cf = forget_log(ctx['c_f'], lb_f)
    logf_cb = forget_log(ctx['c_b'], lb_b)
    if with_ctx_out:
        zero = jnp.zeros((hc.shape[0], HG_HEADS, HEAD_DIM, HEAD_DIM), jnp.float32)
        qc_c = ctx['c_q'] * q_scale
        oc_f, s_f = hgrn2_scan(qc_c, ctx['c_i'], logf_cf, zero, False)
        oc_bw, s_b = hgrn2_scan(qc_c, ctx['c_i'], logf_cb, zero, True)
    else:
        s_f = hgrn2_final_state(ctx['c_i'], logf_cf, False)
        s_b = hgrn2_final_state(ctx['c_i'], logf_cb, True)
    q_c = lat['c_q'] * q_scale
    o_f, _ = hgrn2_scan(q_c, lat['c_i'], forget_log(lat['c_f'], lb_f), s_f, False)
    o_bw, _ = hgrn2_scan(q_c, lat['c_i'], forget_log(lat['c_b'], lb_b), s_b, True)
    o_c = hgrn2_readout(o_f + o_bw, lat['c_g'], gn_g)
    y = jnp.concatenate([o_a, o_b, o_c], axis=-1) @ w_out
    if not with_ctx_out:
        return y, None
    oc_a = block_attention(ctx['a_q'], ctx['a_k'], ctx['a_v'])
    oc_b = block_attention(rms_norm(ctx['b_q'], q_g), kc_b, ctx['b_v'])
    oc_c = hgrn2_readout(oc_f + oc_bw, ctx['c_g'], gn_g)
    yc = jnp.concatenate([oc_a, oc_b, oc_c], axis=-1) @ w_out
    return y, yc


def conv_ffn(h, w_up, conv_w, conv_b, w_down):
    T = h.shape[1]
    u = h @ w_up
    up = jnp.pad(u, ((0, 0), (CONV_W // 2, CONV_W // 2), (0, 0)))
    u = sum(up[:, j:j + T] * conv_w[j] for j in range(CONV_W)) + conv_b
    gate, val = jnp.split(u, 2, axis=-1)
    return (jax.nn.silu(gate) * val) @ w_down


def setup_inputs(seed: int = 0) -> dict:
    key = jax.random.key(seed)
    ks = jax.random.split(key, 21)

    def nrm(k, shape, scale):
        return jax.random.normal(k, shape, jnp.float32) * scale

    D = D_MODEL
    return {
        'x': nrm(ks[0], (BATCH, SEQ, D), 1.0),
        'c': nrm(ks[1], (BATCH, D), 1.0),
        'ctx': nrm(ks[2], (BATCH, CTX_LEN, D), 1.0),
        'c_ctx': nrm(ks[3], (D,), 1.0),
        'w_ada': nrm(ks[4], (DEPTH, D, N_MOD * D), 0.5 * D ** -0.5),
        'b_ada': nrm(ks[5], (DEPTH, N_MOD * D), 0.02),
        'w_in': nrm(ks[6], (DEPTH, D, IN_COLS), D ** -0.5),
        'w_out': nrm(ks[7], (DEPTH, MIX_WIDTH, D), DEEPNORM_BETA * MIX_WIDTH ** -0.5),
        'na_rpb': nrm(ks[8], (DEPTH, NA_HEADS, 2 * NA_WIN_ROWS - 1, 2 * NA_WIN_COLS - 1), 0.1),
        'q_norm': 1.0 + nrm(ks[9], (DEPTH, HEAD_DIM), 0.02),
        'k_norm': 1.0 + nrm(ks[10], (DEPTH, HEAD_DIM), 0.02),
        'hg_lower_bound': 1.0 + nrm(ks[11], (DEPTH, 2, WC), 0.1),
        'hg_norm': 1.0 + nrm(ks[12], (DEPTH, HEAD_DIM), 0.02),
        'ln1_g': 1.0 + nrm(ks[13], (DEPTH, D), 0.02),
        'ln1_b': nrm(ks[14], (DEPTH, D), 0.02),
        'w_up': nrm(ks[15], (DEPTH, D, 2 * D_FF), D ** -0.5),
        'conv_w': nrm(ks[16], (DEPTH, CONV_W, 2 * D_FF), CONV_W ** -0.5),
        'conv_b': nrm(ks[17], (DEPTH, 2 * D_FF), 0.02),
        'w_down': nrm(ks[18], (DEPTH, D_FF, D), DEEPNORM_BETA * D_FF ** -0.5),
        'ln2_g': 1.0 + nrm(ks[19], (DEPTH, D), 0.02),
        'ln2_b': nrm(ks[20], (DEPTH, D), 0.02),
    }


def reference(x, c, ctx, c_ctx, w_ada, b_ada, w_in, w_out, na_rpb, q_norm, k_norm, hg_lower_bound, hg_norm,
              ln1_g, ln1_b, w_up, conv_w, conv_b, w_down, ln2_g, ln2_b):
    s_c = jax.nn.silu(c)
    s_cc = jax.nn.silu(c_ctx)
    lb = jax.nn.softmax(hg_lower_bound.astype(jnp.float32), axis=0)
    lb = jnp.cumsum(lb, axis=0) - lb[0]
    rope_cos, rope_sin = axial_rope_tables(x.shape[1])
    xc = ctx
    for l in range(DEPTH):
        last = l == DEPTH - 1
        mod = jnp.split((s_c @ w_ada[l] + b_ada[l])[:, None, :], N_MOD, axis=-1)
        n_ctx_mod = 2 if last else N_MOD
        mod_c = jnp.split(s_cc @ w_ada[l, :, :n_ctx_mod * D_MODEL] + b_ada[l, :n_ctx_mod * D_MODEL],
                          n_ctx_mod, axis=-1)
        h = modulate(x, mod[0], mod[1])
        hc = modulate(xc, mod_c[0], mod_c[1])
        y, yc = token_mixer(h, hc, w_in[l], w_out[l], na_rpb[l], q_norm[l], k_norm[l], lb[l], hg_norm[l],
                            rope_cos, rope_sin, not last)
        x = layer_norm(DEEPNORM_ALPHA * x + mod[2] * y, ln1_g[l], ln1_b[l])
        f = conv_ffn(modulate(x, mod[3], mod[4]), w_up[l], conv_w[l], conv_b[l], w_down[l])
        x = layer_norm(DEEPNORM_ALPHA * x + mod[5] * f, ln2_g[l], ln2_b[l])
        if not last:
            xc = layer_norm(DEEPNORM_ALPHA * xc + mod_c[2] * yc, ln1_g[l], ln1_b[l])
            fc = conv_ffn(modulate(xc, mod_c[3], mod_c[4]), w_up[l], conv_w[l], conv_b[l], w_down[l])
            xc = layer_norm(DEEPNORM_ALPHA * xc + mod_c[5] * fc, ln2_g[l], ln2_b[l])
    return x
```

```python
import functools

import numpy as np
import jax
import jax.numpy as jnp
from jax import lax
from jax.experimental import pallas as pl
from jax.experimental.pallas import tpu as pltpu

D = 1024
DEPTH = 4
GRID_W = 64
CTX = 256
SEQ = 2048
ROWS = SEQ // GRID_W
T_ALL = CTX + SEQ
HD = 64
NA_HEADS = 6
NA_WIN_ROWS = 8
NA_WIN_COLS = 16
GQA_HEADS = 6
GQA_KV = 2
HG_HEADS = 4
HG_CHUNK = 64
HG_SUB = 16
ROPE_THETA = 10000.0
LB_FLOOR = 1e-30
WA = NA_HEADS * HD
WB = GQA_HEADS * HD
WKV = GQA_KV * HD
WC = HG_HEADS * HD
IN_COLS = 3072
D_FF = 2816
N_MOD = 6
ALPHA = (2.0 * DEPTH) ** 0.25
LN_EPS = 1e-6
RMS_EPS = 1e-6
ATT_SCALE = HD ** -0.5
NEG = -1e30

LANES = 128
TM = 256
N_TILES = T_ALL // TM
NA_RB = 4
NA_WR = NA_WIN_ROWS + NA_RB - 1
NA_NLOC = NA_WR * GRID_W
FF_CHUNK = 256
HALO = 8
VMEM_LIMIT = 56 * 1024 * 1024

YA_COLS = 3 * WA
YB_COLS = WB + 2 * WKV
YC_COLS = 5 * WC
GQA_ORDER = (0, 3, 1, 4, 2, 5)

HI = lax.Precision.HIGHEST
F32 = jnp.float32
BF16 = jnp.bfloat16


def _params(*sem):
    return pltpu.CompilerParams(dimension_semantics=sem, vmem_limit_bytes=VMEM_LIMIT)


def _dot_nt(a, b):
    return lax.dot_general(a, b, (((1,), (1,)), ((), ())), preferred_element_type=F32)


def _dot_tn(a, b):
    return lax.dot_general(a, b, (((0,), (0,)), ((), ())), preferred_element_type=F32)


def _ln(x):
    mu = jnp.mean(x, axis=-1, keepdims=True)
    xc = x - mu
    var = jnp.mean(xc * xc, axis=-1, keepdims=True)
    return xc * lax.rsqrt(var + LN_EPS)


def _silu(x):
    return x * jax.nn.sigmoid(x)


def _lo_mask(shape):
    return lax.broadcasted_iota(jnp.int32, shape, len(shape) - 1) % LANES < HD


def _head_rms(x, gain):
    lo = _lo_mask(x.shape)
    x2 = x * x
    s_lo = jnp.sum(jnp.where(lo, x2, 0.0), axis=-1, keepdims=True)
    s_hi = jnp.sum(jnp.where(lo, 0.0, x2), axis=-1, keepdims=True)
    ms = jnp.where(lo, s_lo, s_hi) * (1.0 / HD)
    return x * lax.rsqrt(ms + RMS_EPS) * gain


def _mod_kernel(s_ref, w_ref, b_ref, o_ref):
    s = _silu(s_ref[...])
    o_ref[0] = jnp.dot(s, w_ref[0], precision=HI, preferred_element_type=F32) + b_ref[0]


def _modulation(s_in, w_ada, b_ada):
    r = s_in.shape[0]
    cb = 1536
    return pl.pallas_call(
        _mod_kernel,
        grid=(DEPTH, N_MOD * D // cb),
        in_specs=[pl.BlockSpec((r, D), lambda l, j: (0, 0)),
                  pl.BlockSpec((1, D, cb), lambda l, j: (l, 0, j)),
                  pl.BlockSpec((1, 1, cb), lambda l, j: (l, 0, j))],
        out_specs=pl.BlockSpec((1, r, cb), lambda l, j: (l, 0, j)),
        out_shape=jax.ShapeDtypeStruct((DEPTH, r, N_MOD * D), F32),
        compiler_params=_params("arbitrary", "arbitrary"),
        name="adaln_mod",
    )(s_in, w_ada, b_ada.reshape(DEPTH, 1, N_MOD * D))


def _lb_kernel(p_ref, o_ref):
    p = p_ref[...]
    e = jnp.exp(p - jnp.max(p, axis=0, keepdims=True))
    sm = e / jnp.sum(e, axis=0, keepdims=True)
    acc = sm[0:1]
    rows = [acc - sm[0:1]]
    for l in range(1, DEPTH):
        acc = acc + sm[l:l + 1]
        rows.append(acc - sm[0:1])
    o_ref[...] = jnp.concatenate(rows, axis=0)


def _lower_bounds(hg_lower_bound):
    p = hg_lower_bound.reshape(DEPTH, 2 * WC).astype(F32)
    return pl.pallas_call(
        _lb_kernel,
        out_shape=jax.ShapeDtypeStruct((DEPTH, 2 * WC), F32),
        name="hgrn_lower_bounds",
    )(p)


def _inproj_kernel(x_ref, mod_ref, w_ref, ya_ref, yb_ref, yc_ref):
    y = _ln(x_ref[0])
    h = (y * (1.0 + mod_ref[0, 0, 1:2, :]) + mod_ref[0, 0, 0:1, :]).astype(BF16)
    out = jnp.dot(h, w_ref[...], preferred_element_type=F32)
    ya_ref[0] = out[:, :YA_COLS].astype(BF16)
    yb_ref[0] = out[:, YA_COLS:YA_COLS + YB_COLS].astype(BF16)
    yc_ref[0] = out[:, YA_COLS + YB_COLS:]


def _in_projection(xs, modsel, w_in_p):
    b = xs.shape[0]
    row = lambda i, t: (i, t, 0)
    return pl.pallas_call(
        _inproj_kernel,
        grid=(b, N_TILES),
        in_specs=[pl.BlockSpec((1, TM, D), row),
                  pl.BlockSpec((1, 1, 8, D), lambda i, t: (i, jnp.minimum(t, 1), 0, 0)),
                  pl.BlockSpec((D, IN_COLS), lambda i, t: (0, 0))],
        out_specs=[pl.BlockSpec((1, TM, YA_COLS), row),
                   pl.BlockSpec((1, TM, YB_COLS), row),
                   pl.BlockSpec((1, TM, YC_COLS), row)],
        out_shape=[jax.ShapeDtypeStruct((b, T_ALL, YA_COLS), BF16),
                   jax.ShapeDtypeStruct((b, T_ALL, YB_COLS), BF16),
                   jax.ShapeDtypeStruct((b, T_ALL, YC_COLS), F32)],
        compiler_params=_params("parallel", "arbitrary"),
        name="in_projection",
    )(xs, modsel, w_in_p)


def _na_tables():
    def table(rb):
        r0 = NA_RB * rb
        ws = int(np.clip(r0 - NA_WIN_ROWS // 2, 0, ROWS - NA_WR))
        q = np.arange(NA_RB * GRID_W)
        qr, qc = r0 + q // GRID_W, q % GRID_W
        k = np.arange(NA_NLOC)
        kr, kc = ws + k // GRID_W, k % GRID_W
        rs = np.clip(qr - NA_WIN_ROWS // 2, 0, ROWS - NA_WIN_ROWS)
        cs = np.clip(qc - NA_WIN_COLS // 2, 0, GRID_W - NA_WIN_COLS)
        valid = ((kr[None] >= rs[:, None]) & (kr[None] < rs[:, None] + NA_WIN_ROWS)
                 & (kc[None] >= cs[:, None]) & (kc[None] < cs[:, None] + NA_WIN_COLS))
        dr = kr[None] - qr[:, None] + NA_WIN_ROWS - 1
        dc = kc[None] - qc[:, None] + NA_WIN_COLS - 1
        return np.where(valid, dr, 0), np.where(valid, dc, 0), valid

    n_rb = ROWS // NA_RB
    first, mid, last = table(0), table(1), table(n_rb - 1)
    for rb in range(2, n_rb - 1):
        assert all(np.array_equal(a, b_) for a, b_ in zip(table(rb), mid))
    dr, dc, valid = (np.stack(z) for z in zip(first, mid, last))
    return dr.astype(np.int32), dc.astype(np.int32), valid


def _na_bias(rpb):
    dr, dc, valid = _na_tables()
    bias = jnp.where(valid[None], rpb.astype(F32)[:, dr, dc], NEG)
    return bias.reshape(NA_HEADS // 2, 2, 3, NA_RB * GRID_W, NA_NLOC)


def _na_kernel(q_ref, k_ref, v_ref, bias_ref, o_ref):
    t = pl.program_id(2)
    q = q_ref[0]
    lo = _lo_mask((1, LANES))
    kc = k_ref[0, 0:CTX, :]
    vc = v_ref[0, 0:CTX, :]
    zero = jnp.zeros_like(q)

    def heads(fn):
        o0 = fn(jnp.where(lo, q, zero), 0)
        o1 = fn(jnp.where(lo, zero, q), 1)
        o_ref[0] = jnp.where(lo, o0, o1).astype(o_ref.dtype)

    @pl.when(t == 0)
    def _():
        def ctx_only(qm, h):
            s = _dot_nt(qm, kc) * ATT_SCALE
            p = jnp.exp(s - jnp.max(s, axis=-1, keepdims=True))
            l = jnp.sum(p, axis=-1, keepdims=True)
            return jnp.dot(p.astype(BF16), vc, preferred_element_type=F32) / l
        heads(ctx_only)

    @pl.when(t > 0)
    def _():
        rb = t - 1
        ws = jnp.clip(NA_RB * rb - NA_WIN_ROWS // 2, 0, ROWS - NA_WR)
        start = pl.multiple_of(CTX + ws * GRID_W, GRID_W)
        kw = k_ref[0, pl.ds(start, NA_NLOC), :]
        vw = v_ref[0, pl.ds(start, NA_NLOC), :]
        typ = jnp.where(rb == 0, 0, jnp.where(rb == ROWS // NA_RB - 1, 2, 1))

        def local(qm, h):
            s_loc = _dot_nt(qm, kw) * ATT_SCALE + bias_ref[0, h, typ]
            s_ctx = _dot_nt(qm, kc) * ATT_SCALE
            m = jnp.maximum(jnp.max(s_loc, axis=-1, keepdims=True), jnp.max(s_ctx, axis=-1, keepdims=True))
            p_loc = jnp.exp(s_loc - m)
            p_ctx = jnp.exp(s_ctx - m)
            l = jnp.sum(p_loc, axis=-1, keepdims=True) + jnp.sum(p_ctx, axis=-1, keepdims=True)
            o = (jnp.dot(p_loc.astype(BF16), vw, preferred_element_type=F32)
                 + jnp.dot(p_ctx.astype(BF16), vc, preferred_element_type=F32))
            return o / l
        heads(local)


def _mixer_a(ya, bias):
    b = ya.shape[0]
    nhp = NA_HEADS // 2
    return pl.pallas_call(
        _na_kernel,
        grid=(nhp, b, N_TILES),
        in_specs=[pl.BlockSpec((1, TM, LANES), lambda p, i, t: (i, t, p)),
                  pl.BlockSpec((1, T_ALL, LANES), lambda p, i, t: (i, 0, nhp + p)),
                  pl.BlockSpec((1, T_ALL, LANES), lambda p, i, t: (i, 0, 2 * nhp + p)),
                  pl.BlockSpec((1, 2, 3, NA_RB * GRID_W, NA_NLOC), lambda p, i, t: (p, 0, 0, 0, 0))],
        out_specs=pl.BlockSpec((1, TM, LANES), lambda p, i, t: (i, t, p)),
        out_shape=jax.ShapeDtypeStruct((b, T_ALL, WA), BF16),
        compiler_params=_params("arbitrary", "arbitrary", "arbitrary"),
        name="mixer_a_neighbourhood",
    )(ya, ya, ya, bias)


def _rope_tables():
    t = np.arange(SEQ)
    pos = np.stack([t // GRID_W, t % GRID_W]).astype(np.float32)
    n_freq = HD // 4
    inv_freq = (ROPE_THETA ** (-np.arange(n_freq, dtype=np.float32) / n_freq)).astype(np.float32)
    ang = pos[:, :, None] * inv_freq
    cos, sin = np.cos(ang), np.sin(ang)
    cos_h = np.concatenate([cos[0], cos[0], cos[1], cos[1]], axis=-1)
    sin_h = np.concatenate([-sin[0], sin[0], -sin[1], sin[1]], axis=-1)
    cos_all = np.concatenate([np.ones((CTX, HD), np.float32), cos_h], axis=0)
    sin_all = np.concatenate([np.zeros((CTX, HD), np.float32), sin_h], axis=0)
    return (np.tile(cos_all, (1, 2)).astype(np.float32), np.tile(sin_all, (1, 2)).astype(np.float32))


def _rope(x, cos, sin):
    quarter = HD // 4
    first = lax.broadcasted_iota(jnp.int32, x.shape, 1) % (2 * quarter) < quarter
    partner = jnp.where(first, pltpu.roll(x, LANES - quarter, axis=1), pltpu.roll(x, quarter, axis=1))
    return x * cos + partner * sin


def _gqa_kernel(q_ref, k_ref, v_ref, cos_ref, sin_ref, qg_ref, kg_ref, o_ref, kn_ref):
    t = pl.program_id(1)
    lo = _lo_mask((1, LANES))

    @pl.when(t == 0)
    def _():
        def prep(i, carry):
            rows = pl.ds(pl.multiple_of(i * TM, TM), TM)
            kx = _head_rms(k_ref[0, rows, :].astype(F32), kg_ref[...])
            kn_ref[rows, :] = _rope(kx, cos_ref[rows, :], sin_ref[rows, :]).astype(BF16)
            return carry
        lax.fori_loop(0, N_TILES, prep, 0)

    rows = pl.ds(pl.multiple_of(t * TM, TM), TM)
    cos = cos_ref[rows, :]
    sin = sin_ref[rows, :]

    def attend(n_keys):
        kn = kn_ref[0:n_keys, :]
        vv = v_ref[0, 0:n_keys, :]
        for j in range(GQA_HEADS // 2):
            qx = _head_rms(q_ref[0, :, j * LANES:(j + 1) * LANES].astype(F32), qg_ref[...])
            qn = _rope(qx, cos, sin).astype(BF16)
            zero = jnp.zeros_like(qn)
            outs = []
            for h in range(2):
                qm = jnp.where(lo, qn, zero) if h == 0 else jnp.where(lo, zero, qn)
                s = _dot_nt(qm, kn) * ATT_SCALE
                p = jnp.exp(s - jnp.max(s, axis=-1, keepdims=True))
                l = jnp.sum(p, axis=-1, keepdims=True)
                outs.append(jnp.dot(p.astype(BF16), vv, preferred_element_type=F32) / l)
            o_ref[0, :, j * LANES:(j + 1) * LANES] = jnp.where(lo, outs[0], outs[1]).astype(o_ref.dtype)

    @pl.when(t == 0)
    def _():
        attend(CTX)

    @pl.when(t > 0)
    def _():
        attend(T_ALL)


def _mixer_b(yb, cos, sin, qg, kg):
    b = yb.shape[0]
    kcol = WB // LANES
    full = lambda i, t: (0, 0)
    return pl.pallas_call(
        _gqa_kernel,
        grid=(b, N_TILES),
        in_specs=[pl.BlockSpec((1, TM, WB), lambda i, t: (i, t, 0)),
                  pl.BlockSpec((1, T_ALL, LANES), lambda i, t: (i, 0, kcol)),
                  pl.BlockSpec((1, T_ALL, LANES), lambda i, t: (i, 0, kcol + 1)),
                  pl.BlockSpec((T_ALL, LANES), full),
                  pl.BlockSpec((T_ALL, LANES), full),
                  pl.BlockSpec((1, LANES), full),
                  pl.BlockSpec((1, LANES), full)],
        out_specs=pl.BlockSpec((1, TM, WB), lambda i, t: (i, t, 0)),
        out_shape=jax.ShapeDtypeStruct((b, T_ALL, WB), BF16),
        scratch_shapes=[pltpu.VMEM((T_ALL, LANES), BF16)],
        compiler_params=_params("parallel", "arbitrary"),
        name="mixer_b_gqa",
    )(yb, yb, yb, cos, sin, qg, kg)


def _hgrn_consts():
    c, sc = HG_CHUNK, HG_SUB
    t = np.arange(c)
    out = []
    for reverse in (False, True):
        if not reverse:
            cum = t[None, :] <= t[:, None]
            through = (t[None, :] // sc) <= (t[:, None] // sc)
        else:
            cum = t[None, :] >= t[:, None]
            through = (t[None, :] // sc) >= (t[:, None] // sc)
        out.append(np.concatenate([cum, through], axis=0).astype(np.float32))
    return np.stack(out)


def _forget_log(z, lb):
    log_sig = -(jnp.maximum(-z, 0.0) + jnp.log1p(jnp.exp(-jnp.abs(z))))
    a = jnp.log(jnp.maximum(lb, LB_FLOOR))
    c = jnp.log1p(-lb) + log_sig
    return jnp.maximum(a, c) + jnp.log1p(jnp.exp(-jnp.abs(a - c)))


def _hgrn_kernel(zf_ref, zb_ref, v_ref, q_ref, g_ref, lb_ref, gn_ref, tri_ref, ones_ref,
                 o_ref, acc_ref, st_ref, d_ref, r_ref, b_ref):
    c, sc = HG_CHUNK, HG_SUB
    n_sub = c // sc
    n_pair = WC // LANES
    acc_ref[...] = jnp.zeros_like(acc_ref)
    st_ref[...] = jnp.zeros_like(st_ref)
    row = lax.broadcasted_iota(jnp.int32, (c, 1), 0)
    col = lax.broadcasted_iota(jnp.int32, (1, c), 1)
    sub_row = lax.broadcasted_iota(jnp.int32, (sc, 1), 0)
    lo = _lo_mask((1, LANES))
    blk = (lax.broadcasted_iota(jnp.int32, (LANES, LANES), 0) // HD
           == lax.broadcasted_iota(jnp.int32, (LANES, LANES), 1) // HD)

    def chunk_step(d, z_ref, ci):
        reverse = d == 1
        rows = pl.ds(pl.multiple_of(ci * c, c), c)
        z = z_ref[0, rows, :]
        lb = lb_ref[d:d + 1, :]
        g = _forget_log(z, lb)
        kk = (1.0 - lb) * jax.nn.sigmoid(-z) - (jnp.maximum(lb, LB_FLOOR) - lb)
        v = v_ref[0, rows, :]
        q = q_ref[0, rows, :] * ATT_SCALE
        cums = jnp.dot(tri_ref[d], g, precision=HI, preferred_element_type=F32)
        b, e = cums[0:c], cums[c:2 * c]
        b_ref[d, 0] = b
        b_ref[d, 1] = kk
        b_ref[d, 2] = v
        last = c - 1 if not reverse else 0
        b_last = b_ref[d, 0, last:last + 1, :]
        vb = v.astype(BF16)

        qe = (q * jnp.exp(b)).astype(BF16)
        kdec = (kk * jnp.exp(b_last - b)).astype(BF16)
        o_parts = []
        for p in range(n_pair):
            ln = slice(p * LANES, (p + 1) * LANES)
            st = st_ref[d, p]
            o_parts.append(_dot_nt(qe[:, ln], st.astype(BF16)))
            upd = _dot_tn(vb[:, ln], kdec[:, ln])
            st_ref[d, p] = jnp.where(blk, st * jnp.exp(b_last[:, ln]) + upd, 0.0)
        o = jnp.concatenate(o_parts, axis=1)

        kx = (kk * jnp.exp(e - b)).astype(BF16)
        q_stack = []
        for j in range(n_sub - 1):
            jj = j if not reverse else n_sub - 1 - j
            edge = (jj + 1) * sc - 1 if not reverse else jj * sc
            e_j = b_ref[d, 0, edge:edge + 1, :]
            q_stack.append((q * jnp.exp(jnp.minimum(b - e_j, 0.0))).astype(BF16))
        q_stack = jnp.concatenate(q_stack, axis=0)
        zero = jnp.zeros_like(q_stack)
        att = []
        for h in range(HG_HEADS):
            p, half = divmod(h, 2)
            ln = slice(p * LANES, (p + 1) * LANES)
            qs = q_stack[:, ln]
            qm = jnp.where(lo, qs, zero[:, ln]) if half == 0 else jnp.where(lo, zero[:, ln], qs)
            res = _dot_nt(qm, kx[:, ln])
            a = jnp.zeros((c, c), F32)
            for j in range(n_sub - 1):
                jj = j if not reverse else n_sub - 1 - j
                in_j = (col // sc) == jj
                later = (row // sc) > jj if not reverse else (row // sc) < jj
                a = a + jnp.where(in_j & later, res[j * c:(j + 1) * c, :], 0.0)
            att.append(a.astype(BF16))
        o_off = []
        for p in range(n_pair):
            ln = slice(p * LANES, (p + 1) * LANES)
            o0 = jnp.dot(att[2 * p], vb[:, ln], preferred_element_type=F32)
            o1 = jnp.dot(att[2 * p + 1], vb[:, ln], preferred_element_type=F32)
            o_off.append(jnp.where(lo, o0, o1))
        o = o + jnp.concatenate(o_off, axis=1)

        for i in range(n_sub):
            bi = b[i * sc:(i + 1) * sc, :]
            qi = q[i * sc:(i + 1) * sc, :]
            for s in range(sc):
                r = i * sc + s
                w = jnp.exp(jnp.minimum(bi - b_ref[d, 0, r:r + 1, :], 0.0)) * (qi * b_ref[d, 1, r:r + 1, :])
                keep = sub_row >= s if not reverse else sub_row <= s
                d_ref[d, r * sc:(r + 1) * sc, :] = jnp.where(keep, w, 0.0).astype(BF16)
        r_ref[d] = jnp.dot(d_ref[d], ones_ref[...], preferred_element_type=F32)
        o_diag = []
        for i in range(n_sub):
            a = jnp.zeros((sc, WC), F32)
            for s in range(sc):
                r = i * sc + s
                a = a + r_ref[d, r * sc:(r + 1) * sc, :] * b_ref[d, 2, r:r + 1, :]
            o_diag.append(a)
        o = o + jnp.concatenate(o_diag, axis=0)
        acc_ref[rows, :] += o

    n_ctx = CTX // c
    n_all = T_ALL // c

    def body(n, carry):
        chunk_step(0, zf_ref, n)
        chunk_step(1, zb_ref, jnp.where(n < n_ctx, n_ctx - 1 - n, n_all + n_ctx - 1 - n))
        return carry
    lax.fori_loop(0, n_all, body, 0)

    def readout(i, carry):
        rows = pl.ds(pl.multiple_of(i * TM, TM), TM)
        gate = _silu(g_ref[0, rows, :])
        for p in range(n_pair):
            ln = slice(p * LANES, (p + 1) * LANES)
            y = _head_rms(acc_ref[rows, ln], gn_ref[...]) * gate[:, ln]
            o_ref[0, rows, ln] = y.astype(o_ref.dtype)
        return carry
    lax.fori_loop(0, N_TILES, readout, 0)


def _mixer_c(yc, lb, gn, tri, ones):
    b = yc.shape[0]
    c = HG_CHUNK
    col = lambda j: (lambda i: (i, 0, j))
    return pl.pallas_call(
        _hgrn_kernel,
        grid=(b,),
        in_specs=[pl.BlockSpec((1, T_ALL, WC), col(0)),
                  pl.BlockSpec((1, T_ALL, WC), col(1)),
                  pl.BlockSpec((1, T_ALL, WC), col(2)),
                  pl.BlockSpec((1, T_ALL, WC), col(3)),
                  pl.BlockSpec((1, T_ALL, WC), col(4)),
                  pl.BlockSpec((2, WC), lambda i: (0, 0)),
                  pl.BlockSpec((1, LANES), lambda i: (0, 0)),
                  pl.BlockSpec((2, 2 * c, c), lambda i: (0, 0, 0)),
                  pl.BlockSpec((WC, WC), lambda i: (0, 0))],
        out_specs=pl.BlockSpec((1, T_ALL, WC), lambda i: (i, 0, 0)),
        out_shape=jax.ShapeDtypeStruct((b, T_ALL, WC), BF16),
        scratch_shapes=[pltpu.VMEM((T_ALL, WC), F32),
                        pltpu.VMEM((2, WC // LANES, LANES, LANES), F32),
                        pltpu.VMEM((2, c * HG_SUB, WC), BF16),
                        pltpu.VMEM((2, c * HG_SUB, WC), F32),
                        pltpu.VMEM((2, 3, c, WC), F32)],
        compiler_params=_params("parallel"),
        name="mixer_c_hgrn2",
    )(yc, yc, yc, yc, yc, lb, gn, tri, ones)


def _outproj_kernel(oa_ref, ob_ref, oc_ref, x_ref, mod_ref, w_ref, g_ref, b_ref, o_ref):
    mix = jnp.concatenate([oa_ref[0], ob_ref[0], oc_ref[0]], axis=1)
    y = jnp.dot(mix, w_ref[...], preferred_element_type=F32)
    z = _ln(ALPHA * x_ref[0] + mod_ref[0, 0, 2:3, :] * y)
    o_ref[0] = z * g_ref[...] + b_ref[...]


def _out_projection(oa, ob, oc, xs, modsel, w_out_p, ln_g, ln_b, t0):
    b = xs.shape[0]
    row = lambda i, t: (i, t + t0, 0)
    full = lambda i, t: (0, 0)
    return pl.pallas_call(
        _outproj_kernel,
        grid=(b, N_TILES - t0),
        in_specs=[pl.BlockSpec((1, TM, WA), row),
                  pl.BlockSpec((1, TM, WB), row),
                  pl.BlockSpec((1, TM, WC), row),
                  pl.BlockSpec((1, TM, D), row),
                  pl.BlockSpec((1, 1, 8, D), lambda i, t: (i, jnp.minimum(t + t0, 1), 0, 0)),
                  pl.BlockSpec((D, D), full),
                  pl.BlockSpec((1, D), full),
                  pl.BlockSpec((1, D), full)],
        out_specs=pl.BlockSpec((1, TM, D), row),
        out_shape=jax.ShapeDtypeStruct((b, T_ALL, D), F32),
        compiler_params=_params("parallel", "arbitrary"),
        name="out_projection_ln1",
    )(oa, ob, oc, xs, modsel, w_out_p, ln_g, ln_b)


def _ffn_kernel(t0, xp_ref, x_ref, xn_ref, mod_ref, wup_ref, cw_ref, cb_ref, wdn_ref, g_ref, b_ref, o_ref):
    t = pl.program_id(1) + t0
    x = x_ref[0]
    xe = jnp.concatenate([xp_ref[0], x, xn_ref[0]], axis=0)
    h = _ln(xe) * (1.0 + mod_ref[0, 0, 4:5, :]) + mod_ref[0, 0, 3:4, :]
    r = lax.broadcasted_iota(jnp.int32, (HALO + TM + HALO, 1), 0)
    keep = ((r >= HALO) | (t >= 2)) & ((r < HALO + TM) | ((t >= 1) & (t < N_TILES - 1)))
    h = jnp.where(keep, h, 0.0).astype(BF16)
    n_e = HALO + TM + HALO
    acc = jnp.zeros((TM, D), F32)
    for c in range(D_FF // FF_CHUNK):
        parts = []
        for base in (0, D_FF):
            cols = slice(base + c * FF_CHUNK, base + (c + 1) * FF_CHUNK)
            u = jnp.dot(h, wup_ref[:, cols], preferred_element_type=F32)
            u = (pltpu.roll(u, 1, axis=0) * cw_ref[0:1, cols] + u * cw_ref[1:2, cols]
                 + pltpu.roll(u, n_e - 1, axis=0) * cw_ref[2:3, cols])
            parts.append(u[HALO:HALO + TM, :] + cb_ref[:, cols])
        a = (_silu(parts[0]) * parts[1]).astype(BF16)
        acc = acc + jnp.dot(a, wdn_ref[c * FF_CHUNK:(c + 1) * FF_CHUNK, :], preferred_element_type=F32)
    z = _ln(ALPHA * x + mod_ref[0, 0, 5:6, :] * acc)
    o_ref[0] = z * g_ref[...] + b_ref[...]


def _conv_ffn(x1, modsel, w_up, conv_w, conv_b, w_down, ln_g, ln_b, t0):
    b = x1.shape[0]
    per = TM // HALO
    n_halo = T_ALL // HALO
    row = lambda i, t: (i, t + t0, 0)
    full = lambda i, t: (0, 0)
    return pl.pallas_call(
        functools.partial(_ffn_kernel, t0),
        grid=(b, N_TILES - t0),
        in_specs=[pl.BlockSpec((1, HALO, D), lambda i, t: (i, jnp.maximum((t + t0) * per - 1, t0 * per), 0)),
                  pl.BlockSpec((1, TM, D), row),
                  pl.BlockSpec((1, HALO, D), lambda i, t: (i, jnp.minimum((t + t0 + 1) * per, n_halo - 1), 0)),
                  pl.BlockSpec((1, 1, 8, D), lambda i, t: (i, jnp.minimum(t + t0, 1), 0, 0)),
                  pl.BlockSpec((D, 2 * D_FF), full),
                  pl.BlockSpec((3, 2 * D_FF), full),
                  pl.BlockSpec((1, 2 * D_FF), full),
                  pl.BlockSpec((D_FF, D), full),
                  pl.BlockSpec((1, D), full),
                  pl.BlockSpec((1, D), full)],
        out_specs=pl.BlockSpec((1, TM, D), row),
        out_shape=jax.ShapeDtypeStruct((b, T_ALL, D), F32),
        compiler_params=_params("parallel", "arbitrary"),
        name="conv_ffn_ln2",
    )(x1, x1, x1, modsel, w_up, conv_w, conv_b, w_down, ln_g, ln_b)


def _in_col_perm():
    off, o = {}, 0
    for name, width in (('a_k', WA), ('a_v', WA), ('b_k', WKV), ('b_v', WKV), ('c_f', WC), ('c_b', WC),
                        ('c_i', WC), ('a_q', WA), ('b_q', WB), ('c_q', WC), ('c_g', WC)):
        off[name] = np.arange(o, o + width)
        o += width
    bq = off['b_q'].reshape(GQA_HEADS, HD)[list(GQA_ORDER)].reshape(-1)
    return np.concatenate([off['a_q'], off['a_k'], off['a_v'], bq, off['b_k'], off['b_v'],
                           off['c_f'], off['c_b'], off['c_i'], off['c_q'], off['c_g']])


def _out_row_perm():
    ob = (WA + np.arange(WB)).reshape(GQA_HEADS, HD)[list(GQA_ORDER)].reshape(-1)
    return np.concatenate([np.arange(WA), ob, WA + WB + np.arange(WC)])


def kernel(x, c, ctx, c_ctx, w_ada, b_ada, w_in, w_out, na_rpb, q_norm, k_norm, hg_lower_bound, hg_norm,
           ln1_g, ln1_b, w_up, conv_w, conv_b, w_down, ln2_g, ln2_b):
    b = x.shape[0]
    assert x.shape == (b, SEQ, D) and ctx.shape == (b, CTX, D)
    n_rows = -(-(b + 1) // 8) * 8
    s_in = jnp.concatenate([c, c_ctx[None, :], jnp.zeros((n_rows - b - 1, D), F32)], axis=0)
    mods = _modulation(s_in, w_ada, b_ada).reshape(DEPTH, n_rows, N_MOD, D)
    lbs = _lower_bounds(hg_lower_bound).reshape(DEPTH, 2, WC)

    cos, sin = (jnp.asarray(a) for a in _rope_tables())
    tri = jnp.asarray(_hgrn_consts())
    ones = jnp.asarray(np.kron(np.eye(HG_HEADS), np.ones((HD, HD))), BF16)
    in_perm = _in_col_perm()
    out_perm = _out_row_perm()
    tile2 = lambda g: jnp.tile(g.astype(F32), 2).reshape(1, LANES)

    xs = jnp.concatenate([ctx, x], axis=1)
    for l in range(DEPTH):
        last = l == DEPTH - 1
        t0 = 1 if last else 0
        lat = mods[l, :b]
        cx = jnp.broadcast_to(mods[l, b][None], lat.shape)
        modsel = jnp.pad(jnp.stack([cx, lat], axis=1), ((0, 0), (0, 0), (0, 8 - N_MOD), (0, 0)))
        ya, yb, yc = _in_projection(xs, modsel, w_in[l][:, in_perm].astype(BF16))
        oa = _mixer_a(ya, _na_bias(na_rpb[l]))
        ob = _mixer_b(yb, cos, sin, tile2(q_norm[l]), tile2(k_norm[l]))
        oc = _mixer_c(yc, lbs[l], tile2(hg_norm[l]), tri, ones)
        x1 = _out_projection(oa, ob, oc, xs, modsel, w_out[l][out_perm].astype(BF16),
                             ln1_g[l].reshape(1, D), ln1_b[l].reshape(1, D), t0)
        xs = _conv_ffn(x1, modsel, w_up[l].astype(BF16), conv_w[l], conv_b[l].reshape(1, 2 * D_FF),
                       w_down[l].astype(BF16), ln2_g[l].reshape(1, D), ln2_b[l].reshape(1, D), t0)
    return xs[:, CTX:, :]
```

```python
import functools

import numpy as np
import jax
import jax.numpy as jnp
from jax import lax
from jax.experimental import pallas as pl
from jax.experimental.pallas import tpu as pltpu

D = 1024
DEPTH = 4
GRID_W = 64
CTX = 256
SEQ = 2048
ROWS = SEQ // GRID_W
T_ALL = CTX + SEQ
HD = 64
NA_HEADS = 6
NA_WIN_ROWS = 8
NA_WIN_COLS = 16
GQA_HEADS = 6
GQA_KV = 2
HG_HEADS = 4
HG_CHUNK = 64
HG_SUB = 16
ROPE_THETA = 10000.0
LB_FLOOR = 1e-30
WA = NA_HEADS * HD
WB = GQA_HEADS * HD
WKV = GQA_KV * HD
WC = HG_HEADS * HD
IN_COLS = 3072
D_FF = 2816
N_MOD = 6
ALPHA = (2.0 * DEPTH) ** 0.25
LN_EPS = 1e-6
RMS_EPS = 1e-6
ATT_SCALE = HD ** -0.5
NEG = -1e30

LANES = 128
TM = 256
N_TILES = T_ALL // TM
NA_RB = 4
NA_WR = NA_WIN_ROWS + NA_RB - 1
NA_NLOC = NA_WR * GRID_W
FF_CHUNK = 256
HALO = 8
VMEM_LIMIT = 56 * 1024 * 1024

YA_COLS = 3 * WA
YB_COLS = WB + 2 * WKV
YC_COLS = 5 * WC
GQA_ORDER = (0, 3, 1, 4, 2, 5)

HI = lax.Precision.HIGHEST
F32 = jnp.float32
BF16 = jnp.bfloat16


def _params(*sem):
    return pltpu.CompilerParams(dimension_semantics=sem, vmem_limit_bytes=VMEM_LIMIT)


def _dot_nt(a, b):
    return lax.dot_general(a, b, (((1,), (1,)), ((), ())), preferred_element_type=F32)


def _dot_tn(a, b):
    return lax.dot_general(a, b, (((0,), (0,)), ((), ())), preferred_element_type=F32)


def _ln(x):
    mu = jnp.mean(x, axis=-1, keepdims=True)
    xc = x - mu
    var = jnp.mean(xc * xc, axis=-1, keepdims=True)
    return xc * lax.rsqrt(var + LN_EPS)


def _silu(x):
    return x * jax.nn.sigmoid(x)


def _lo_mask(shape):
    return lax.broadcasted_iota(jnp.int32, shape, len(shape) - 1) % LANES < HD


def _head_rms(x, gain):
    lo = _lo_mask(x.shape)
    x2 = x * x
    s_lo = jnp.sum(jnp.where(lo, x2, 0.0), axis=-1, keepdims=True)
    s_hi = jnp.sum(jnp.where(lo, 0.0, x2), axis=-1, keepdims=True)
    ms = jnp.where(lo, s_lo, s_hi) * (1.0 / HD)
    return x * lax.rsqrt(ms + RMS_EPS) * gain


def _mod_kernel(s_ref, w_ref, b_ref, o_ref):
    s = _silu(s_ref[...])
    o_ref[0] = jnp.dot(s, w_ref[0], precision=HI, preferred_element_type=F32) + b_ref[0]


def _modulation(s_in, w_ada, b_ada):
    r = s_in.shape[0]
    cb = 1536
    return pl.pallas_call(
        _mod_kernel,
        grid=(DEPTH, N_MOD * D // cb),
        in_specs=[pl.BlockSpec((r, D), lambda l, j: (0, 0)),
                  pl.BlockSpec((1, D, cb), lambda l, j: (l, 0, j)),
                  pl.BlockSpec((1, 1, cb), lambda l, j: (l, 0, j))],
        out_specs=pl.BlockSpec((1, r, cb), lambda l, j: (l, 0, j)),
        out_shape=jax.ShapeDtypeStruct((DEPTH, r, N_MOD * D), F32),
        compiler_params=_params("arbitrary", "arbitrary"),
        name="adaln_mod",
    )(s_in, w_ada, b_ada.reshape(DEPTH, 1, N_MOD * D))


def _lb_kernel(p_ref, o_ref):
    p = p_ref[...]
    e = jnp.exp(p - jnp.max(p, axis=0, keepdims=True))
    sm = e / jnp.sum(e, axis=0, keepdims=True)
    acc = sm[0:1]
    rows = [acc - sm[0:1]]
    for l in range(1, DEPTH):
        acc = acc + sm[l:l + 1]
        rows.append(acc - sm[0:1])
    o_ref[...] = jnp.concatenate(rows, axis=0)


def _lower_bounds(hg_lower_bound):
    p = hg_lower_bound.reshape(DEPTH, 2 * WC).astype(F32)
    return pl.pallas_call(
        _lb_kernel,
        out_shape=jax.ShapeDtypeStruct((DEPTH, 2 * WC), F32),
        name="hgrn_lower_bounds",
    )(p)


def _inproj_kernel(x_ref, mod_ref, w_ref, ya_ref, yb_ref, yc_ref):
    y = _ln(x_ref[0])
    h = (y * (1.0 + mod_ref[0, 0, 1:2, :]) + mod_ref[0, 0, 0:1, :]).astype(BF16)
    out = jnp.dot(h, w_ref[...], preferred_element_type=F32)
    ya_ref[0] = out[:, :YA_COLS].astype(BF16)
    yb_ref[0] = out[:, YA_COLS:YA_COLS + YB_COLS].astype(BF16)
    yc_ref[0] = out[:, YA_COLS + YB_COLS:]


def _in_projection(xs, modsel, w_in_p):
    b = xs.shape[0]
    row = lambda i, t: (i, t, 0)
    return pl.pallas_call(
        _inproj_kernel,
        grid=(b, N_TILES),
        in_specs=[pl.BlockSpec((1, TM, D), row),
                  pl.BlockSpec((1, 1, 8, D), lambda i, t: (i, jnp.minimum(t, 1), 0, 0)),
                  pl.BlockSpec((D, IN_COLS), lambda i, t: (0, 0))],
        out_specs=[pl.BlockSpec((1, TM, YA_COLS), row),
                   pl.BlockSpec((1, TM, YB_COLS), row),
                   pl.BlockSpec((1, TM, YC_COLS), row)],
        out_shape=[jax.ShapeDtypeStruct((b, T_ALL, YA_COLS), BF16),
                   jax.ShapeDtypeStruct((b, T_ALL, YB_COLS), BF16),
                   jax.ShapeDtypeStruct((b, T_ALL, YC_COLS), F32)],
        compiler_params=_params("parallel", "arbitrary"),
        name="in_projection",
    )(xs, modsel, w_in_p)


def _na_slab_index():
    def table(rb):
        r0 = NA_RB * rb
        ws = int(np.clip(r0 - NA_WIN_ROWS // 2, 0, ROWS - NA_WR))
        qr = r0 + np.arange(NA_RB)[:, None]
        kr = ws + np.arange(NA_WR)[None, :]
        rs = np.clip(qr - NA_WIN_ROWS // 2, 0, ROWS - NA_WIN_ROWS)
        valid = (kr >= rs) & (kr < rs + NA_WIN_ROWS)
        return np.where(valid, kr - qr + NA_WIN_ROWS - 1, 2 * NA_WIN_ROWS - 1)

    n_rb = ROWS // NA_RB
    for rb in range(2, n_rb - 1):
        assert np.array_equal(table(rb), table(1))
    return np.stack([table(0), table(1), table(n_rb - 1)])


def _na_slabs(rpb):
    qc = np.arange(GRID_W)[:, None]
    kc = np.arange(GRID_W)[None, :]
    cs = np.clip(qc - NA_WIN_COLS // 2, 0, GRID_W - NA_WIN_COLS)
    valid = (kc >= cs) & (kc < cs + NA_WIN_COLS)
    onehot = (valid[None] & ((kc - qc + NA_WIN_COLS - 1)[None] == np.arange(2 * NA_WIN_COLS - 1)[:, None, None]))
    onehot = jnp.asarray(onehot.reshape(2 * NA_WIN_COLS - 1, GRID_W * GRID_W), F32)
    slabs = jnp.einsum('hrd,dx->hrx', rpb.astype(F32), onehot, precision=HI)
    slabs = jnp.where(jnp.asarray(valid.reshape(-1)), slabs, NEG)
    slabs = jnp.concatenate([slabs, jnp.full((NA_HEADS, 1, GRID_W * GRID_W), NEG, F32)], axis=1)
    slabs = slabs.reshape(NA_HEADS // 2, 2, 2 * NA_WIN_ROWS, GRID_W, GRID_W)
    return jnp.concatenate([slabs, slabs], axis=-1)


def _na_kernel(idx, q_ref, k_ref, v_ref, slab_ref, o_ref, bias_ref):
    t = pl.program_id(2)
    q = q_ref[0]
    lo = _lo_mask((1, LANES))

    @pl.when((pl.program_id(1) == 0) & (t == 0))
    def _():
        for h in range(2):
            for typ in range(3):
                for rl in range(NA_RB):
                    rows = slice(rl * GRID_W, (rl + 1) * GRID_W)
                    for m in range(NA_WR // 2):
                        even = slab_ref[0, h, int(idx[typ, rl, 2 * m])]
                        odd = slab_ref[0, h, int(idx[typ, rl, 2 * m + 1])]
                        bias_ref[h, typ, rows, m * LANES:(m + 1) * LANES] = jnp.where(lo, even, odd)
                    if NA_WR % 2:
                        tail = slab_ref[0, h, int(idx[typ, rl, NA_WR - 1])]
                        bias_ref[h, typ, rows, (NA_WR - 1) * GRID_W:NA_NLOC] = tail[:, 0:GRID_W]

    kc = k_ref[0, 0:CTX, :]
    vc = v_ref[0, 0:CTX, :]
    zero = jnp.zeros_like(q)
    qs = q * ATT_SCALE
    qm = [jnp.where(lo, qs, zero), jnp.where(lo, zero, qs)]

    @pl.when(t == 0)
    def _():
        outs = []
        for s in [_dot_nt(qm[h], kc) for h in range(2)]:
            p = jnp.exp(s - jnp.max(s, axis=-1, keepdims=True))
            l = jnp.sum(p, axis=-1, keepdims=True)
            outs.append(jnp.dot(p.astype(BF16), vc, preferred_element_type=F32) / l)
        o_ref[0] = jnp.where(lo, outs[0], outs[1]).astype(o_ref.dtype)

    @pl.when(t > 0)
    def _():
        rb = t - 1
        ws = jnp.clip(NA_RB * rb - NA_WIN_ROWS // 2, 0, ROWS - NA_WR)
        start = pl.multiple_of(CTX + ws * GRID_W, GRID_W)
        kw = k_ref[0, pl.ds(start, NA_NLOC), :]
        vw = v_ref[0, pl.ds(start, NA_NLOC), :]
        typ = jnp.where(rb == 0, 0, jnp.where(rb == ROWS // NA_RB - 1, 2, 1))

        scores = [(_dot_nt(qm[h], kw) + bias_ref[h, typ], _dot_nt(qm[h], kc)) for h in range(2)]
        outs = []
        for s_loc, s_ctx in scores:
            m = jnp.maximum(jnp.max(s_loc, axis=-1, keepdims=True), jnp.max(s_ctx, axis=-1, keepdims=True))
            p_loc = jnp.exp(s_loc - m)
            p_ctx = jnp.exp(s_ctx - m)
            l = jnp.sum(p_loc, axis=-1, keepdims=True) + jnp.sum(p_ctx, axis=-1, keepdims=True)
            o = (jnp.dot(p_loc.astype(BF16), vw, preferred_element_type=F32)
                 + jnp.dot(p_ctx.astype(BF16), vc, preferred_element_type=F32))
            outs.append(o / l)
        o_ref[0] = jnp.where(lo, outs[0], outs[1]).astype(o_ref.dtype)


def _mixer_a(ya, slabs):
    b = ya.shape[0]
    nhp = NA_HEADS // 2
    return pl.pallas_call(
        functools.partial(_na_kernel, _na_slab_index()),
        grid=(nhp, b, N_TILES),
        in_specs=[pl.BlockSpec((1, TM, LANES), lambda p, i, t: (i, t, p)),
                  pl.BlockSpec((1, T_ALL, LANES), lambda p, i, t: (i, 0, nhp + p)),
                  pl.BlockSpec((1, T_ALL, LANES), lambda p, i, t: (i, 0, 2 * nhp + p)),
                  pl.BlockSpec((1, 2, 2 * NA_WIN_ROWS, GRID_W, LANES), lambda p, i, t: (p, 0, 0, 0, 0))],
        out_specs=pl.BlockSpec((1, TM, LANES), lambda p, i, t: (i, t, p)),
        out_shape=jax.ShapeDtypeStruct((b, T_ALL, WA), BF16),
        scratch_shapes=[pltpu.VMEM((2, 3, NA_RB * GRID_W, NA_NLOC), F32)],
        compiler_params=_params("arbitrary", "arbitrary", "arbitrary"),
        name="mixer_a_neighbourhood",
    )(ya, ya, ya, slabs)


def _rope_tables():
    t = np.arange(SEQ)
    pos = np.stack([t // GRID_W, t % GRID_W]).astype(np.float32)
    n_freq = HD // 4
    inv_freq = (ROPE_THETA ** (-np.arange(n_freq, dtype=np.float32) / n_freq)).astype(np.float32)
    ang = pos[:, :, None] * inv_freq
    cos, sin = np.cos(ang), np.sin(ang)
    cos_h = np.concatenate([cos[0], cos[0], cos[1], cos[1]], axis=-1)
    sin_h = np.concatenate([-sin[0], sin[0], -sin[1], sin[1]], axis=-1)
    cos_all = np.concatenate([np.ones((CTX, HD), np.float32), cos_h], axis=0)
    sin_all = np.concatenate([np.zeros((CTX, HD), np.float32), sin_h], axis=0)
    return (np.tile(cos_all, (1, 2)).astype(np.float32), np.tile(sin_all, (1, 2)).astype(np.float32))


def _rope(x, cos, sin):
    quarter = HD // 4
    first = lax.broadcasted_iota(jnp.int32, x.shape, 1) % (2 * quarter) < quarter
    partner = jnp.where(first, pltpu.roll(x, LANES - quarter, axis=1), pltpu.roll(x, quarter, axis=1))
    return x * cos + partner * sin


def _gqa_kernel(q_ref, k_ref, v_ref, cos_ref, sin_ref, qg_ref, kg_ref, o_ref, kn_ref):
    t = pl.program_id(1)
    lo = _lo_mask((1, LANES))

    @pl.when(t == 0)
    def _():
        def prep(i, carry):
            rows = pl.ds(pl.multiple_of(i * TM, TM), TM)
            kx = _head_rms(k_ref[0, rows, :].astype(F32), kg_ref[...])
            kn_ref[rows, :] = _rope(kx, cos_ref[rows, :], sin_ref[rows, :]).astype(BF16)
            return carry
        lax.fori_loop(0, N_TILES, prep, 0)

    rows = pl.ds(pl.multiple_of(t * TM, TM), TM)
    cos = cos_ref[rows, :]
    sin = sin_ref[rows, :]

    def attend(n_keys):
        kn = kn_ref[0:n_keys, :]
        vv = v_ref[0, 0:n_keys, :]

        qn = []
        for j in range(GQA_HEADS // 2):
            qx = _head_rms(q_ref[0, :, j * LANES:(j + 1) * LANES].astype(F32), qg_ref[...])
            qn.append((_rope(qx, cos, sin) * ATT_SCALE).astype(BF16))
        zero = jnp.zeros_like(qn[0])

        def scores(i):
            j, h = divmod(i, 2)
            return _dot_nt(jnp.where(lo, qn[j], zero) if h == 0 else jnp.where(lo, zero, qn[j]), kn)

        s_next = scores(0)
        outs = []
        for i in range(GQA_HEADS):
            s = s_next
            if i + 1 < GQA_HEADS:
                s_next = scores(i + 1)
            p = jnp.exp(s - jnp.max(s, axis=-1, keepdims=True))
            l = jnp.sum(p, axis=-1, keepdims=True)
            outs.append(jnp.dot(p.astype(BF16), vv, preferred_element_type=F32) / l)
            if i % 2:
                j = i // 2
                o_ref[0, :, j * LANES:(j + 1) * LANES] = jnp.where(lo, outs[i - 1], outs[i]).astype(o_ref.dtype)

    @pl.when(t == 0)
    def _():
        attend(CTX)

    @pl.when(t > 0)
    def _():
        attend(T_ALL)


def _mixer_b(yb, cos, sin, qg, kg):
    b = yb.shape[0]
    kcol = WB // LANES
    full = lambda i, t: (0, 0)
    return pl.pallas_call(
        _gqa_kernel,
        grid=(b, N_TILES),
        in_specs=[pl.BlockSpec((1, TM, WB), lambda i, t: (i, t, 0)),
                  pl.BlockSpec((1, T_ALL, LANES), lambda i, t: (i, 0, kcol)),
                  pl.BlockSpec((1, T_ALL, LANES), lambda i, t: (i, 0, kcol + 1)),
                  pl.BlockSpec((T_ALL, LANES), full),
                  pl.BlockSpec((T_ALL, LANES), full),
                  pl.BlockSpec((1, LANES), full),
                  pl.BlockSpec((1, LANES), full)],
        out_specs=pl.BlockSpec((1, TM, WB), lambda i, t: (i, t, 0)),
        out_shape=jax.ShapeDtypeStruct((b, T_ALL, WB), BF16),
        scratch_shapes=[pltpu.VMEM((T_ALL, LANES), BF16)],
        compiler_params=_params("parallel", "arbitrary"),
        name="mixer_b_gqa",
    )(yb, yb, yb, cos, sin, qg, kg)


def _hgrn_consts():
    c, sc = HG_CHUNK, HG_SUB
    t = np.arange(c)
    out = []
    for reverse in (False, True):
        if not reverse:
            cum = t[None, :] <= t[:, None]
            through = (t[None, :] // sc) <= (t[:, None] // sc)
        else:
            cum = t[None, :] >= t[:, None]
            through = (t[None, :] // sc) >= (t[:, None] // sc)
        out.append(np.concatenate([cum, through], axis=0).astype(np.float32))
    return np.stack(out)


def _forget_gate(z, lb):
    ez = jnp.exp(-jnp.abs(z))
    big = 1.0 / (1.0 + ez)
    small = ez * big
    pos = z >= 0.0
    lb_floor = jnp.maximum(lb, LB_FLOOR)
    f = lb_floor + (1.0 - lb) * jnp.where(pos, big, small)
    one_minus_f = (1.0 - lb) * jnp.where(pos, small, big) - (lb_floor - lb)
    return jnp.log(f), one_minus_f


def _hgrn_kernel(zf_ref, zb_ref, v_ref, q_ref, g_ref, lb_ref, gn_ref, tri_ref, ones_ref,
                 o_ref, acc_ref, st_ref, d_ref, r_ref, b_ref):
    c, sc = HG_CHUNK, HG_SUB
    n_sub = c // sc
    n_pair = WC // LANES
    acc_ref[...] = jnp.zeros_like(acc_ref)
    st_ref[...] = jnp.zeros_like(st_ref)
    row = lax.broadcasted_iota(jnp.int32, (c, 1), 0)
    col = lax.broadcasted_iota(jnp.int32, (1, c), 1)
    sub_row = lax.broadcasted_iota(jnp.int32, (sc, 1), 0)
    lo = _lo_mask((1, LANES))
    blk = (lax.broadcasted_iota(jnp.int32, (LANES, LANES), 0) // HD
           == lax.broadcasted_iota(jnp.int32, (LANES, LANES), 1) // HD)

    def chunk_step(d, z_ref, ci):
        reverse = d == 1
        rows = pl.ds(pl.multiple_of(ci * c, c), c)
        g, kk = _forget_gate(z_ref[0, rows, :], lb_ref[d:d + 1, :])
        v = v_ref[0, rows, :]
        q = q_ref[0, rows, :] * ATT_SCALE
        cums = jnp.dot(tri_ref[d], g, precision=HI, preferred_element_type=F32)
        b, e = cums[0:c], cums[c:2 * c]
        b_ref[d, 0] = b
        b_ref[d, 1] = kk
        b_ref[d, 2] = v
        last = c - 1 if not reverse else 0
        b_last = b_ref[d, 0, last:last + 1, :]
        vb = v.astype(BF16)

        qe = (q * jnp.exp(b)).astype(BF16)
        kdec = (kk * jnp.exp(b_last - b)).astype(BF16)
        o_parts = []
        for p in range(n_pair):
            ln = slice(p * LANES, (p + 1) * LANES)
            st = st_ref[d, p]
            o_parts.append(_dot_nt(qe[:, ln], st.astype(BF16)))
            upd = _dot_tn(vb[:, ln], kdec[:, ln])
            st_ref[d, p] = jnp.where(blk, st * jnp.exp(b_last[:, ln]) + upd, 0.0)
        o = jnp.concatenate(o_parts, axis=1)

        kx = (kk * jnp.exp(e - b)).astype(BF16)
        q_stack = []
        for j in range(n_sub - 1):
            jj = j if not reverse else n_sub - 1 - j
            edge = (jj + 1) * sc - 1 if not reverse else jj * sc
            e_j = b_ref[d, 0, edge:edge + 1, :]
            q_stack.append((q * jnp.exp(jnp.minimum(b - e_j, 0.0))).astype(BF16))
        q_stack = jnp.concatenate(q_stack, axis=0)
        zero = jnp.zeros_like(q_stack)
        att = []
        for h in range(HG_HEADS):
            p, half = divmod(h, 2)
            ln = slice(p * LANES, (p + 1) * LANES)
            qs = q_stack[:, ln]
            qm = jnp.where(lo, qs, zero[:, ln]) if half == 0 else jnp.where(lo, zero[:, ln], qs)
            res = _dot_nt(qm, kx[:, ln])
            a = jnp.zeros((c, c), F32)
            for j in range(n_sub - 1):
                jj = j if not reverse else n_sub - 1 - j
                in_j = (col // sc) == jj
                later = (row // sc) > jj if not reverse else (row // sc) < jj
                a = a + jnp.where(in_j & later, res[j * c:(j + 1) * c, :], 0.0)
            att.append(a.astype(BF16))
        o_off = []
        for p in range(n_pair):
            ln = slice(p * LANES, (p + 1) * LANES)
            o0 = jnp.dot(att[2 * p], vb[:, ln], preferred_element_type=F32)
            o1 = jnp.dot(att[2 * p + 1], vb[:, ln], preferred_element_type=F32)
            o_off.append(jnp.where(lo, o0, o1))
        o = o + jnp.concatenate(o_off, axis=1)

        for i in range(n_sub):
            bi = b[i * sc:(i + 1) * sc, :]
            qi = q[i * sc:(i + 1) * sc, :]
            for s in range(sc):
                r = i * sc + s
                w = jnp.exp(bi - b_ref[d, 0, r:r + 1, :]) * (qi * b_ref[d, 1, r:r + 1, :])
                keep = sub_row >= s if not reverse else sub_row <= s
                d_ref[d, r * sc:(r + 1) * sc, :] = jnp.where(keep, w, 0.0).astype(BF16)
        r_ref[d] = jnp.dot(d_ref[d], ones_ref[...], preferred_element_type=F32)
        o_diag = []
        for i in range(n_sub):
            a = jnp.zeros((sc, WC), F32)
            for s in range(sc):
                r = i * sc + s
                a = a + r_ref[d, r * sc:(r + 1) * sc, :] * b_ref[d, 2, r:r + 1, :]
            o_diag.append(a)
        o = o + jnp.concatenate(o_diag, axis=0)
        acc_ref[rows, :] += o

    n_ctx = CTX // c
    n_all = T_ALL // c

    def body(n, carry):
        chunk_step(0, zf_ref, n)
        chunk_step(1, zb_ref, jnp.where(n < n_ctx, n_ctx - 1 - n, n_all + n_ctx - 1 - n))
        return carry
    lax.fori_loop(0, n_all, body, 0)

    def readout(i, carry):
        rows = pl.ds(pl.multiple_of(i * TM, TM), TM)
        gate = _silu(g_ref[0, rows, :])
        for p in range(n_pair):
            ln = slice(p * LANES, (p + 1) * LANES)
            y = _head_rms(acc_ref[rows, ln], gn_ref[...]) * gate[:, ln]
            o_ref[0, rows, ln] = y.astype(o_ref.dtype)
        return carry
    lax.fori_loop(0, N_TILES, readout, 0)


def _mixer_c(yc, lb, gn, tri, ones):
    b = yc.shape[0]
    c = HG_CHUNK
    col = lambda j: (lambda i: (i, 0, j))
    return pl.pallas_call(
        _hgrn_kernel,
        grid=(b,),
        in_specs=[pl.BlockSpec((1, T_ALL, WC), col(0)),
                  pl.BlockSpec((1, T_ALL, WC), col(1)),
                  pl.BlockSpec((1, T_ALL, WC), col(2)),
                  pl.BlockSpec((1, T_ALL, WC), col(3)),
                  pl.BlockSpec((1, T_ALL, WC), col(4)),
                  pl.BlockSpec((2, WC), lambda i: (0, 0)),
                  pl.BlockSpec((1, LANES), lambda i: (0, 0)),
                  pl.BlockSpec((2, 2 * c, c), lambda i: (0, 0, 0)),
                  pl.BlockSpec((WC, WC), lambda i: (0, 0))],
        out_specs=pl.BlockSpec((1, T_ALL, WC), lambda i: (i, 0, 0)),
        out_shape=jax.ShapeDtypeStruct((b, T_ALL, WC), BF16),
        scratch_shapes=[pltpu.VMEM((T_ALL, WC), F32),
                        pltpu.VMEM((2, WC // LANES, LANES, LANES), F32),
                        pltpu.VMEM((2, c * HG_SUB, WC), BF16),
                        pltpu.VMEM((2, c * HG_SUB, WC), F32),
                        pltpu.VMEM((2, 3, c, WC), F32)],
        compiler_params=_params("parallel"),
        name="mixer_c_hgrn2",
    )(yc, yc, yc, yc, yc, lb, gn, tri, ones)


def _outproj_kernel(oa_ref, ob_ref, oc_ref, x_ref, mod_ref, w_ref, g_ref, b_ref, o_ref):
    mix = jnp.concatenate([oa_ref[0], ob_ref[0], oc_ref[0]], axis=1)
    y = jnp.dot(mix, w_ref[...], preferred_element_type=F32)
    z = _ln(ALPHA * x_ref[0] + mod_ref[0, 0, 2:3, :] * y)
    o_ref[0] = z * g_ref[...] + b_ref[...]


def _out_projection(oa, ob, oc, xs, modsel, w_out_p, ln_g, ln_b, t0):
    b = xs.shape[0]
    row = lambda i, t: (i, t + t0, 0)
    full = lambda i, t: (0, 0)
    return pl.pallas_call(
        _outproj_kernel,
        grid=(b, N_TILES - t0),
        in_specs=[pl.BlockSpec((1, TM, WA), row),
                  pl.BlockSpec((1, TM, WB), row),
                  pl.BlockSpec((1, TM, WC), row),
                  pl.BlockSpec((1, TM, D), row),
                  pl.BlockSpec((1, 1, 8, D), lambda i, t: (i, jnp.minimum(t + t0, 1), 0, 0)),
                  pl.BlockSpec((D, D), full),
                  pl.BlockSpec((1, D), full),
                  pl.BlockSpec((1, D), full)],
        out_specs=pl.BlockSpec((1, TM, D), row),
        out_shape=jax.ShapeDtypeStruct((b, T_ALL, D), F32),
        compiler_params=_params("parallel", "arbitrary"),
        name="out_projection_ln1",
    )(oa, ob, oc, xs, modsel, w_out_p, ln_g, ln_b)


def _ffn_kernel(t0, xp_ref, x_ref, xn_ref, mod_ref, wup_ref, cw_ref, cb_ref, wdn_ref, g_ref, b_ref, o_ref):
    t = pl.program_id(1) + t0
    x = x_ref[0]
    xe = jnp.concatenate([xp_ref[0], x, xn_ref[0]], axis=0)
    h = _ln(xe) * (1.0 + mod_ref[0, 0, 4:5, :]) + mod_ref[0, 0, 3:4, :]
    r = lax.broadcasted_iota(jnp.int32, (HALO + TM + HALO, 1), 0)
    keep = ((r >= HALO) | (t >= 2)) & ((r < HALO + TM) | ((t >= 1) & (t < N_TILES - 1)))
    h = jnp.where(keep, h, 0.0).astype(BF16)
    n_e = HALO + TM + HALO
    n_chunk = D_FF // FF_CHUNK

    def up(c, base):
        return jnp.dot(h, wup_ref[:, base + c * FF_CHUNK:base + (c + 1) * FF_CHUNK], preferred_element_type=F32)

    def conv(u, c, base):
        cols = slice(base + c * FF_CHUNK, base + (c + 1) * FF_CHUNK)
        u = (pltpu.roll(u, 1, axis=0) * cw_ref[0:1, cols] + u * cw_ref[1:2, cols]
             + pltpu.roll(u, n_e - 1, axis=0) * cw_ref[2:3, cols])
        return u[HALO:HALO + TM, :] + cb_ref[:, cols]

    acc = jnp.zeros((TM, D), F32)
    ahead = 2
    pending = [(up(c, 0), up(c, D_FF)) for c in range(ahead)]
    for c in range(n_chunk):
        u_gate, u_val = pending.pop(0)
        if c + ahead < n_chunk:
            pending.append((up(c + ahead, 0), up(c + ahead, D_FF)))
        a = (_silu(conv(u_gate, c, 0)) * conv(u_val, c, D_FF)).astype(BF16)
        acc = acc + jnp.dot(a, wdn_ref[c * FF_CHUNK:(c + 1) * FF_CHUNK, :], preferred_element_type=F32)
    z = _ln(ALPHA * x + mod_ref[0, 0, 5:6, :] * acc)
    o_ref[0] = z * g_ref[...] + b_ref[...]


def _conv_ffn(x1, modsel, w_up, conv_w, conv_b, w_down, ln_g, ln_b, t0):
    b = x1.shape[0]
    per = TM // HALO
    n_halo = T_ALL // HALO
    row = lambda i, t: (i, t + t0, 0)
    full = lambda i, t: (0, 0)
    return pl.pallas_call(
        functools.partial(_ffn_kernel, t0),
        grid=(b, N_TILES - t0),
        in_specs=[pl.BlockSpec((1, HALO, D), lambda i, t: (i, jnp.maximum((t + t0) * per - 1, t0 * per), 0)),
                  pl.BlockSpec((1, TM, D), row),
                  pl.BlockSpec((1, HALO, D), lambda i, t: (i, jnp.minimum((t + t0 + 1) * per, n_halo - 1), 0)),
                  pl.BlockSpec((1, 1, 8, D), lambda i, t: (i, jnp.minimum(t + t0, 1), 0, 0)),
                  pl.BlockSpec((D, 2 * D_FF), full),
                  pl.BlockSpec((3, 2 * D_FF), full),
                  pl.BlockSpec((1, 2 * D_FF), full),
                  pl.BlockSpec((D_FF, D), full),
                  pl.BlockSpec((1, D), full),
                  pl.BlockSpec((1, D), full)],
        out_specs=pl.BlockSpec((1, TM, D), row),
        out_shape=jax.ShapeDtypeStruct((b, T_ALL, D), F32),
        compiler_params=_params("parallel", "arbitrary"),
        name="conv_ffn_ln2",
    )(x1, x1, x1, modsel, w_up, conv_w, conv_b, w_down, ln_g, ln_b)


def _in_col_perm():
    off, o = {}, 0
    for name, width in (('a_k', WA), ('a_v', WA), ('b_k', WKV), ('b_v', WKV), ('c_f', WC), ('c_b', WC),
                        ('c_i', WC), ('a_q', WA), ('b_q', WB), ('c_q', WC), ('c_g', WC)):
        off[name] = np.arange(o, o + width)
        o += width
    bq = off['b_q'].reshape(GQA_HEADS, HD)[list(GQA_ORDER)].reshape(-1)
    return np.concatenate([off['a_q'], off['a_k'], off['a_v'], bq, off['b_k'], off['b_v'],
                           off['c_f'], off['c_b'], off['c_i'], off['c_q'], off['c_g']])


def _out_row_perm():
    ob = (WA + np.arange(WB)).reshape(GQA_HEADS, HD)[list(GQA_ORDER)].reshape(-1)
    return np.concatenate([np.arange(WA), ob, WA + WB + np.arange(WC)])


def _take_runs(w, perm, axis):
    cuts = [0] + [i for i in range(1, len(perm)) if perm[i] != perm[i - 1] + 1] + [len(perm)]
    parts = [lax.slice_in_dim(w, int(perm[a]), int(perm[z - 1]) + 1, axis=axis) for a, z in zip(cuts[:-1], cuts[1:])]
    return jnp.concatenate(parts, axis=axis)


def kernel(x, c, ctx, c_ctx, w_ada, b_ada, w_in, w_out, na_rpb, q_norm, k_norm, hg_lower_bound, hg_norm,
           ln1_g, ln1_b, w_up, conv_w, conv_b, w_down, ln2_g, ln2_b):
    b = x.shape[0]
    assert x.shape == (b, SEQ, D) and ctx.shape == (b, CTX, D)
    n_rows = -(-(b + 1) // 8) * 8
    s_in = jnp.concatenate([c, c_ctx[None, :], jnp.zeros((n_rows - b - 1, D), F32)], axis=0)
    mods = _modulation(s_in, w_ada, b_ada).reshape(DEPTH, n_rows, N_MOD, D)
    lbs = _lower_bounds(hg_lower_bound).reshape(DEPTH, 2, WC)

    cos, sin = (jnp.asarray(a) for a in _rope_tables())
    tri = jnp.asarray(_hgrn_consts())
    ones = jnp.asarray(np.kron(np.eye(HG_HEADS), np.ones((HD, HD))), BF16)
    in_perm = _in_col_perm()
    out_perm = _out_row_perm()
    tile2 = lambda g: jnp.tile(g.astype(F32), 2).reshape(1, LANES)

    xs = jnp.concatenate([ctx, x], axis=1)
    for l in range(DEPTH):
        last = l == DEPTH - 1
        t0 = 1 if last else 0
        lat = mods[l, :b]
        cx = jnp.broadcast_to(mods[l, b][None], lat.shape)
        modsel = jnp.pad(jnp.stack([cx, lat], axis=1), ((0, 0), (0, 0), (0, 8 - N_MOD), (0, 0)))
        ya, yb, yc = _in_projection(xs, modsel, _take_runs(w_in[l], in_perm, 1).astype(BF16))
        oa = _mixer_a(ya, _na_slabs(na_rpb[l]))
        ob = _mixer_b(yb, cos, sin, tile2(q_norm[l]), tile2(k_norm[l]))
        oc = _mixer_c(yc, lbs[l], tile2(hg_norm[l]), tri, ones)
        x1 = _out_projection(oa, ob, oc, xs, modsel, _take_runs(w_out[l], out_perm, 0).astype(BF16),
                             ln1_g[l].reshape(1, D), ln1_b[l].reshape(1, D), t0)
        xs = _conv_ffn(x1, modsel, w_up[l].astype(BF16), conv_w[l], conv_b[l].reshape(1, 2 * D_FF),
                       w_down[l].astype(BF16), ln2_g[l].reshape(1, D), ln2_b[l].reshape(1, D), t0)
    return xs[:, CTX:, :]
```

```python
import functools

import numpy as np
import jax
import jax.numpy as jnp
from jax import lax
from jax.experimental import pallas as pl
from jax.experimental.pallas import tpu as pltpu

D = 1024
DEPTH = 4
GRID_W = 64
CTX = 256
SEQ = 2048
ROWS = SEQ // GRID_W
T_ALL = CTX + SEQ
HD = 64
NA_HEADS = 6
NA_WIN_ROWS = 8
NA_WIN_COLS = 16
GQA_HEADS = 6
GQA_KV = 2
HG_HEADS = 4
HG_CHUNK = 64
HG_SUB = 16
ROPE_THETA = 10000.0
LB_FLOOR = 1e-30
WA = NA_HEADS * HD
WB = GQA_HEADS * HD
WKV = GQA_KV * HD
WC = HG_HEADS * HD
IN_COLS = 3072
D_FF = 2816
N_MOD = 6
ALPHA = (2.0 * DEPTH) ** 0.25
LN_EPS = 1e-6
RMS_EPS = 1e-6
ATT_SCALE = HD ** -0.5
LOG2E = 1.4426950408889634
NEG = -1e30

LANES = 128
TM = 256
N_TILES = T_ALL // TM
NA_RB = 4
NA_WR = NA_WIN_ROWS + NA_RB - 1
NA_NLOC = NA_WR * GRID_W
FF_CHUNK = 256
HALO = 8
VMEM_LIMIT = 56 * 1024 * 1024

YA_COLS = 3 * WA
YB_COLS = WB + 2 * WKV
YC_COLS = 5 * WC
GQA_ORDER = (0, 3, 1, 4, 2, 5)

HI = lax.Precision.HIGHEST
F32 = jnp.float32
BF16 = jnp.bfloat16


def _params(*sem):
    return pltpu.CompilerParams(dimension_semantics=sem, vmem_limit_bytes=VMEM_LIMIT)


def _dot_nt(a, b):
    return lax.dot_general(a, b, (((1,), (1,)), ((), ())), preferred_element_type=F32)


def _dot_tn(a, b):
    return lax.dot_general(a, b, (((0,), (0,)), ((), ())), preferred_element_type=F32)


def _ln(x):
    mu = jnp.mean(x, axis=-1, keepdims=True)
    xc = x - mu
    var = jnp.mean(xc * xc, axis=-1, keepdims=True)
    return xc * lax.rsqrt(var + LN_EPS)


def _dot_01(m, x):
    hi = x.astype(BF16)
    rest = x - hi.astype(F32)
    mid = rest.astype(BF16)
    lo = (rest - mid.astype(F32)).astype(BF16)
    return (jnp.dot(m, hi, preferred_element_type=F32) + jnp.dot(m, mid, preferred_element_type=F32)
            + jnp.dot(m, lo, preferred_element_type=F32))


def _silu(x):
    return x * jax.nn.sigmoid(x)


def _lo_mask(shape):
    return lax.broadcasted_iota(jnp.int32, shape, len(shape) - 1) % LANES < HD


def _head_rms(x, gain):
    lo = _lo_mask(x.shape)
    x2 = x * x
    s_lo = jnp.sum(jnp.where(lo, x2, 0.0), axis=-1, keepdims=True)
    s_hi = jnp.sum(jnp.where(lo, 0.0, x2), axis=-1, keepdims=True)
    ms = jnp.where(lo, s_lo, s_hi) * (1.0 / HD)
    return x * lax.rsqrt(ms + RMS_EPS) * gain


def _mod_kernel(s_ref, w_ref, b_ref, o_ref):
    s = _silu(s_ref[...])
    o_ref[0] = jnp.dot(s, w_ref[0], precision=HI, preferred_element_type=F32) + b_ref[0]


def _modulation(s_in, w_ada, b_ada):
    r = s_in.shape[0]
    cb = 1536
    return pl.pallas_call(
        _mod_kernel,
        grid=(DEPTH, N_MOD * D // cb),
        in_specs=[pl.BlockSpec((r, D), lambda l, j: (0, 0)),
                  pl.BlockSpec((1, D, cb), lambda l, j: (l, 0, j)),
                  pl.BlockSpec((1, 1, cb), lambda l, j: (l, 0, j))],
        out_specs=pl.BlockSpec((1, r, cb), lambda l, j: (l, 0, j)),
        out_shape=jax.ShapeDtypeStruct((DEPTH, r, N_MOD * D), F32),
        compiler_params=_params("arbitrary", "arbitrary"),
        name="adaln_mod",
    )(s_in, w_ada, b_ada.reshape(DEPTH, 1, N_MOD * D))


def _lb_kernel(p_ref, o_ref):
    p = p_ref[...]
    e = jnp.exp(p - jnp.max(p, axis=0, keepdims=True))
    sm = e / jnp.sum(e, axis=0, keepdims=True)
    acc = sm[0:1]
    rows = [acc - sm[0:1]]
    for l in range(1, DEPTH):
        acc = acc + sm[l:l + 1]
        rows.append(acc - sm[0:1])
    o_ref[...] = jnp.concatenate(rows, axis=0)


def _lower_bounds(hg_lower_bound):
    p = hg_lower_bound.reshape(DEPTH, 2 * WC).astype(F32)
    return pl.pallas_call(
        _lb_kernel,
        out_shape=jax.ShapeDtypeStruct((DEPTH, 2 * WC), F32),
        name="hgrn_lower_bounds",
    )(p)


def _inproj_kernel(x_ref, mod_ref, w_ref, ya_ref, yb_ref, yc_ref):
    y = _ln(x_ref[0])
    h = (y * (1.0 + mod_ref[0, 0, 1:2, :]) + mod_ref[0, 0, 0:1, :]).astype(BF16)
    out = jnp.dot(h, w_ref[...], preferred_element_type=F32)
    ya_ref[0] = out[:, :YA_COLS].astype(BF16)
    yb_ref[0] = out[:, YA_COLS:YA_COLS + YB_COLS].astype(BF16)
    yc_ref[0] = out[:, YA_COLS + YB_COLS:]


def _in_projection(xs, modsel, w_in_p):
    b = xs.shape[0]
    row = lambda i, t: (i, t, 0)
    return pl.pallas_call(
        _inproj_kernel,
        grid=(b, N_TILES),
        in_specs=[pl.BlockSpec((1, TM, D), row),
                  pl.BlockSpec((1, 1, 8, D), lambda i, t: (i, jnp.minimum(t, 1), 0, 0)),
                  pl.BlockSpec((D, IN_COLS), lambda i, t: (0, 0))],
        out_specs=[pl.BlockSpec((1, TM, YA_COLS), row),
                   pl.BlockSpec((1, TM, YB_COLS), row),
                   pl.BlockSpec((1, TM, YC_COLS), row)],
        out_shape=[jax.ShapeDtypeStruct((b, T_ALL, YA_COLS), BF16),
                   jax.ShapeDtypeStruct((b, T_ALL, YB_COLS), BF16),
                   jax.ShapeDtypeStruct((b, T_ALL, YC_COLS), F32)],
        compiler_params=_params("parallel", "arbitrary"),
        name="in_projection",
    )(xs, modsel, w_in_p)


def _na_slab_index():
    def table(rb):
        r0 = NA_RB * rb
        ws = int(np.clip(r0 - NA_WIN_ROWS // 2, 0, ROWS - NA_WR))
        qr = r0 + np.arange(NA_RB)[:, None]
        kr = ws + np.arange(NA_WR)[None, :]
        rs = np.clip(qr - NA_WIN_ROWS // 2, 0, ROWS - NA_WIN_ROWS)
        valid = (kr >= rs) & (kr < rs + NA_WIN_ROWS)
        return np.where(valid, kr - qr + NA_WIN_ROWS - 1, 2 * NA_WIN_ROWS - 1)

    n_rb = ROWS // NA_RB
    for rb in range(2, n_rb - 1):
        assert np.array_equal(table(rb), table(1))
    return np.stack([table(0), table(1), table(n_rb - 1)])


def _na_slabs(rpb):
    qc = np.arange(GRID_W)[:, None]
    kc = np.arange(GRID_W)[None, :]
    cs = np.clip(qc - NA_WIN_COLS // 2, 0, GRID_W - NA_WIN_COLS)
    valid = (kc >= cs) & (kc < cs + NA_WIN_COLS)
    onehot = (valid[None] & ((kc - qc + NA_WIN_COLS - 1)[None] == np.arange(2 * NA_WIN_COLS - 1)[:, None, None]))
    onehot = jnp.asarray(onehot.reshape(2 * NA_WIN_COLS - 1, GRID_W * GRID_W), F32)
    slabs = jnp.einsum('hrd,dx->hrx', rpb.astype(F32), onehot, precision=HI)
    slabs = jnp.where(jnp.asarray(valid.reshape(-1)), slabs, NEG)
    slabs = jnp.concatenate([slabs, jnp.full((NA_HEADS, 1, GRID_W * GRID_W), NEG, F32)], axis=1)
    slabs = slabs.reshape(NA_HEADS // 2, 2, 2 * NA_WIN_ROWS, GRID_W, GRID_W)
    return jnp.concatenate([slabs, slabs], axis=-1)


def _na_kernel(idx, q_ref, k_ref, v_ref, slab_ref, o_ref, bias_ref):
    t = pl.program_id(2)
    q = q_ref[0]
    lo = _lo_mask((1, LANES))

    @pl.when((pl.program_id(1) == 0) & (t == 0))
    def _():
        for h in range(2):
            for typ in range(3):
                for rl in range(NA_RB):
                    rows = slice(rl * GRID_W, (rl + 1) * GRID_W)
                    for m in range(NA_WR // 2):
                        even = slab_ref[0, h, int(idx[typ, rl, 2 * m])]
                        odd = slab_ref[0, h, int(idx[typ, rl, 2 * m + 1])]
                        bias_ref[h, typ, rows, m * LANES:(m + 1) * LANES] = jnp.where(lo, even, odd)
                    if NA_WR % 2:
                        tail = slab_ref[0, h, int(idx[typ, rl, NA_WR - 1])]
                        bias_ref[h, typ, rows, (NA_WR - 1) * GRID_W:NA_NLOC] = tail[:, 0:GRID_W]

    kc = k_ref[0, 0:CTX, :]
    vc = v_ref[0, 0:CTX, :]
    zero = jnp.zeros_like(q)
    qs = q * ATT_SCALE
    qm = [jnp.where(lo, qs, zero), jnp.where(lo, zero, qs)]

    @pl.when(t == 0)
    def _():
        outs = []
        for s in [_dot_nt(qm[h], kc) for h in range(2)]:
            p = jnp.exp(s - jnp.max(s, axis=-1, keepdims=True))
            l = jnp.sum(p, axis=-1, keepdims=True)
            outs.append(jnp.dot(p.astype(BF16), vc, preferred_element_type=F32) / l)
        o_ref[0] = jnp.where(lo, outs[0], outs[1]).astype(o_ref.dtype)

    @pl.when(t > 0)
    def _():
        rb = t - 1
        ws = jnp.clip(NA_RB * rb - NA_WIN_ROWS // 2, 0, ROWS - NA_WR)
        start = pl.multiple_of(CTX + ws * GRID_W, GRID_W)
        kw = k_ref[0, pl.ds(start, NA_NLOC), :]
        vw = v_ref[0, pl.ds(start, NA_NLOC), :]
        typ = jnp.where(rb == 0, 0, jnp.where(rb == ROWS // NA_RB - 1, 2, 1))

        scores = [(_dot_nt(qm[h], kw) + bias_ref[h, typ], _dot_nt(qm[h], kc)) for h in range(2)]
        outs = []
        for s_loc, s_ctx in scores:
            m = jnp.maximum(jnp.max(s_loc, axis=-1, keepdims=True), jnp.max(s_ctx, axis=-1, keepdims=True))
            p_loc = jnp.exp(s_loc - m)
            p_ctx = jnp.exp(s_ctx - m)
            l = jnp.sum(p_loc, axis=-1, keepdims=True) + jnp.sum(p_ctx, axis=-1, keepdims=True)
            o = (jnp.dot(p_loc.astype(BF16), vw, preferred_element_type=F32)
                 + jnp.dot(p_ctx.astype(BF16), vc, preferred_element_type=F32))
            outs.append(o / l)
        o_ref[0] = jnp.where(lo, outs[0], outs[1]).astype(o_ref.dtype)


def _mixer_a(ya, slabs):
    b = ya.shape[0]
    nhp = NA_HEADS // 2
    return pl.pallas_call(
        functools.partial(_na_kernel, _na_slab_index()),
        grid=(nhp, b, N_TILES),
        in_specs=[pl.BlockSpec((1, TM, LANES), lambda p, i, t: (i, t, p)),
                  pl.BlockSpec((1, T_ALL, LANES), lambda p, i, t: (i, 0, nhp + p)),
                  pl.BlockSpec((1, T_ALL, LANES), lambda p, i, t: (i, 0, 2 * nhp + p)),
                  pl.BlockSpec((1, 2, 2 * NA_WIN_ROWS, GRID_W, LANES), lambda p, i, t: (p, 0, 0, 0, 0))],
        out_specs=pl.BlockSpec((1, TM, LANES), lambda p, i, t: (i, t, p)),
        out_shape=jax.ShapeDtypeStruct((b, T_ALL, WA), BF16),
        scratch_shapes=[pltpu.VMEM((2, 3, NA_RB * GRID_W, NA_NLOC), F32)],
        compiler_params=_params("arbitrary", "arbitrary", "arbitrary"),
        name="mixer_a_neighbourhood",
    )(ya, ya, ya, slabs)


def _rope_tables():
    t = np.arange(SEQ)
    pos = np.stack([t // GRID_W, t % GRID_W]).astype(np.float32)
    n_freq = HD // 4
    inv_freq = (ROPE_THETA ** (-np.arange(n_freq, dtype=np.float32) / n_freq)).astype(np.float32)
    ang = pos[:, :, None] * inv_freq
    cos, sin = np.cos(ang), np.sin(ang)
    cos_h = np.concatenate([cos[0], cos[0], cos[1], cos[1]], axis=-1)
    sin_h = np.concatenate([-sin[0], sin[0], -sin[1], sin[1]], axis=-1)
    cos_all = np.concatenate([np.ones((CTX, HD), np.float32), cos_h], axis=0)
    sin_all = np.concatenate([np.zeros((CTX, HD), np.float32), sin_h], axis=0)
    return (np.tile(cos_all, (1, 2)).astype(np.float32), np.tile(sin_all, (1, 2)).astype(np.float32))


def _rope(x, cos, sin):
    quarter = HD // 4
    first = lax.broadcasted_iota(jnp.int32, x.shape, 1) % (2 * quarter) < quarter
    partner = jnp.where(first, pltpu.roll(x, LANES - quarter, axis=1), pltpu.roll(x, quarter, axis=1))
    return x * cos + partner * sin


def _gqa_kernel(q_ref, k_ref, v_ref, cos_ref, sin_ref, qg_ref, kg_ref, o_ref, kn_ref):
    t = pl.program_id(1)
    lo = _lo_mask((1, LANES))

    @pl.when(t == 0)
    def _():
        def prep(i, carry):
            rows = pl.ds(pl.multiple_of(i * TM, TM), TM)
            kx = _head_rms(k_ref[0, rows, :].astype(F32), kg_ref[...])
            kn_ref[rows, :] = _rope(kx, cos_ref[rows, :], sin_ref[rows, :]).astype(BF16)
            return carry
        lax.fori_loop(0, N_TILES, prep, 0)

    rows = pl.ds(pl.multiple_of(t * TM, TM), TM)
    cos = cos_ref[rows, :]
    sin = sin_ref[rows, :]

    def attend(n_keys):
        kn = kn_ref[0:n_keys, :]
        vv = v_ref[0, 0:n_keys, :]

        qn = []
        for j in range(GQA_HEADS // 2):
            qx = _head_rms(q_ref[0, :, j * LANES:(j + 1) * LANES].astype(F32), qg_ref[...])
            qn.append((_rope(qx, cos, sin) * ATT_SCALE).astype(BF16))
        zero = jnp.zeros_like(qn[0])

        def scores(i):
            j, h = divmod(i, 2)
            return _dot_nt(jnp.where(lo, qn[j], zero) if h == 0 else jnp.where(lo, zero, qn[j]), kn)

        s_next = scores(0)
        outs = []
        for i in range(GQA_HEADS):
            s = s_next
            if i + 1 < GQA_HEADS:
                s_next = scores(i + 1)
            p = jnp.exp(s - jnp.max(s, axis=-1, keepdims=True))
            l = jnp.sum(p, axis=-1, keepdims=True)
            outs.append(jnp.dot(p.astype(BF16), vv, preferred_element_type=F32) / l)
            if i % 2:
                j = i // 2
                o_ref[0, :, j * LANES:(j + 1) * LANES] = jnp.where(lo, outs[i - 1], outs[i]).astype(o_ref.dtype)

    @pl.when(t == 0)
    def _():
        attend(CTX)

    @pl.when(t > 0)
    def _():
        attend(T_ALL)


def _mixer_b(yb, cos, sin, qg, kg):
    b = yb.shape[0]
    kcol = WB // LANES
    full = lambda i, t: (0, 0)
    return pl.pallas_call(
        _gqa_kernel,
        grid=(b, N_TILES),
        in_specs=[pl.BlockSpec((1, TM, WB), lambda i, t: (i, t, 0)),
                  pl.BlockSpec((1, T_ALL, LANES), lambda i, t: (i, 0, kcol)),
                  pl.BlockSpec((1, T_ALL, LANES), lambda i, t: (i, 0, kcol + 1)),
                  pl.BlockSpec((T_ALL, LANES), full),
                  pl.BlockSpec((T_ALL, LANES), full),
                  pl.BlockSpec((1, LANES), full),
                  pl.BlockSpec((1, LANES), full)],
        out_specs=pl.BlockSpec((1, TM, WB), lambda i, t: (i, t, 0)),
        out_shape=jax.ShapeDtypeStruct((b, T_ALL, WB), BF16),
        scratch_shapes=[pltpu.VMEM((T_ALL, LANES), BF16)],
        compiler_params=_params("parallel", "arbitrary"),
        name="mixer_b_gqa",
    )(yb, yb, yb, cos, sin, qg, kg)


def _hgrn_consts():
    c, sc = HG_CHUNK, HG_SUB
    t = np.arange(c)
    out = []
    for reverse in (False, True):
        if not reverse:
            cum = t[None, :] <= t[:, None]
            through = (t[None, :] // sc) <= (t[:, None] // sc)
        else:
            cum = t[None, :] >= t[:, None]
            through = (t[None, :] // sc) >= (t[:, None] // sc)
        out.append(np.concatenate([cum, through], axis=0).astype(np.float32))
    return np.stack(out)


def _forget_gate(z, lb):
    ez = jnp.exp(-jnp.abs(z))
    big = 1.0 / (1.0 + ez)
    small = ez * big
    pos = z >= 0.0
    lb_floor = jnp.maximum(lb, LB_FLOOR)
    f = lb_floor + (1.0 - lb) * jnp.where(pos, big, small)
    one_minus_f = (1.0 - lb) * jnp.where(pos, small, big) - (lb_floor - lb)
    return jnp.log(f), one_minus_f


def _hgrn_kernel(zf_ref, zb_ref, v_ref, q_ref, g_ref, lb_ref, gn_ref, tri_ref, ones_ref,
                 o_ref, acc_ref, st_ref, d_ref, b_ref):
    c, sc = HG_CHUNK, HG_SUB
    n_sub = c // sc
    n_pair = WC // LANES
    acc_ref[...] = jnp.zeros_like(acc_ref)
    st_ref[...] = jnp.zeros_like(st_ref)
    sub_row = lax.broadcasted_iota(jnp.int32, (sc, 1), 0)
    blk = (lax.broadcasted_iota(jnp.int32, (LANES, LANES), 0) // HD
           == lax.broadcasted_iota(jnp.int32, (LANES, LANES), 1) // HD)
    own_head = (lax.broadcasted_iota(jnp.int32, (HG_HEADS * c, WC), 0) // c
                == lax.broadcasted_iota(jnp.int32, (HG_HEADS * c, WC), 1) // HD)
    q_sub = lax.broadcasted_iota(jnp.int32, (c, WC), 0) // sc
    k_sub = (lax.broadcasted_iota(jnp.int32, (c, WC), 1) % c) // sc
    k_lane = lax.broadcasted_iota(jnp.int32, (1, WC), 1) % c

    z_refs = (zf_ref, zb_ref)
    n_ctx = CTX // c
    n_all = T_ALL // c

    def chunk_of(d, n):
        return n if d == 0 else jnp.where(n < n_ctx, n_ctx - 1 - n, n_all + n_ctx - 1 - n)

    def rows_of(d, n):
        return pl.ds(pl.multiple_of(chunk_of(d, n) * c, c), c)

    def gates(d, n, slot):
        g, kk = _forget_gate(z_refs[d][0, rows_of(d, n), :], lb_ref[d:d + 1, :])
        cums = _dot_01(tri_ref[d], g * LOG2E)
        b_ref[d, slot, 0] = cums[0:c]
        b_ref[d, slot, 1] = kk
        b_ref[d, slot, 2] = cums[c:2 * c]

    def state_and_far(d, n, slot):
        reverse = d == 1
        rows = rows_of(d, n)
        b = b_ref[d, slot, 0]
        kk = b_ref[d, slot, 1]
        e = b_ref[d, slot, 2]
        q = q_ref[0, rows, :] * ATT_SCALE
        vb = v_ref[0, rows, :].astype(BF16)
        last = c - 1 if not reverse else 0
        b_last = b_ref[d, slot, 0, last:last + 1, :]

        qe = (q * jnp.exp2(b)).astype(BF16)
        kdec = (kk * jnp.exp2(b_last - b)).astype(BF16)
        o_parts = []
        for p in range(n_pair):
            ln = slice(p * LANES, (p + 1) * LANES)
            st = st_ref[d, p]
            o_parts.append(_dot_nt(qe[:, ln], st.astype(BF16)))
            upd = _dot_tn(vb[:, ln], kdec[:, ln])
            st_ref[d, p] = jnp.where(blk, st * jnp.exp2(b_last[:, ln]) + upd, 0.0)
        o = jnp.concatenate(o_parts, axis=1)

        kx = (kk * jnp.exp2(e - b)).astype(BF16)
        keys = jnp.where(own_head, jnp.concatenate([kx] * HG_HEADS, axis=0), jnp.zeros((), BF16))
        vals = jnp.where(own_head, jnp.concatenate([vb] * HG_HEADS, axis=0), jnp.zeros((), BF16))
        q_stack = []
        for j in range(n_sub - 1):
            jj = j if not reverse else n_sub - 1 - j
            edge = (jj + 1) * sc - 1 if not reverse else jj * sc
            e_j = b_ref[d, slot, 0, edge:edge + 1, :]
            q_stack.append((q * jnp.exp2(jnp.minimum(b - e_j, 0.0))).astype(BF16))
        res = _dot_nt(jnp.concatenate(q_stack, axis=0), keys)
        att = jnp.zeros((c, WC), F32)
        for j in range(n_sub - 1):
            jj = j if not reverse else n_sub - 1 - j
            later = q_sub > jj if not reverse else q_sub < jj
            att = att + jnp.where((k_sub == jj) & later, res[j * c:(j + 1) * c, :], 0.0)
        return dict(rows=rows, b=b, q=q, o=o, att=att, vals=vals)

    def near_products(d, slot, w):
        reverse = d == 1
        for i in range(n_sub):
            bi = w['b'][i * sc:(i + 1) * sc, :]
            qi = w['q'][i * sc:(i + 1) * sc, :]
            for s in range(sc):
                r = i * sc + s
                p = jnp.exp2(bi - b_ref[d, slot, 0, r:r + 1, :]) * (qi * b_ref[d, slot, 1, r:r + 1, :])
                keep = sub_row >= s if not reverse else sub_row <= s
                d_ref[d, r * sc:(r + 1) * sc, :] = jnp.where(keep, p, 0.0).astype(BF16)
        w['sums'] = jnp.dot(d_ref[d], ones_ref[...], preferred_element_type=F32)

    def finish(d, w):
        near = []
        for i in range(n_sub):
            a = jnp.zeros((sc, WC), F32)
            for s in range(sc):
                r = i * sc + s
                a = jnp.where(k_lane == r, w['sums'][r * sc:(r + 1) * sc, :], a)
            near.append(a)
        att = w['att'] + jnp.concatenate(near, axis=0)
        acc_ref[w['rows'], :] += w['o'] + jnp.dot(att.astype(BF16), w['vals'], preferred_element_type=F32)

    for d in range(2):
        gates(d, 0, 0)

    def body(m, carry):
        for slot in range(2):
            n = 2 * m + slot
            for d in range(2):
                gates(d, jnp.minimum(n + 1, n_all - 1), 1 - slot)
            work = [state_and_far(d, n, slot) for d in range(2)]
            for d in range(2):
                near_products(d, slot, work[d])
            for d in range(2):
                finish(d, work[d])
        return carry
    lax.fori_loop(0, n_all // 2, body, 0)

    def readout(i, carry):
        rows = pl.ds(pl.multiple_of(i * TM, TM), TM)
        gate = _silu(g_ref[0, rows, :])
        for p in range(n_pair):
            ln = slice(p * LANES, (p + 1) * LANES)
            y = _head_rms(acc_ref[rows, ln], gn_ref[...]) * gate[:, ln]
            o_ref[0, rows, ln] = y.astype(o_ref.dtype)
        return carry
    lax.fori_loop(0, N_TILES, readout, 0)


def _mixer_c(yc, lb, gn, tri, ones):
    b = yc.shape[0]
    c = HG_CHUNK
    col = lambda j: (lambda i: (i, 0, j))
    return pl.pallas_call(
        _hgrn_kernel,
        grid=(b,),
        in_specs=[pl.BlockSpec((1, T_ALL, WC), col(0)),
                  pl.BlockSpec((1, T_ALL, WC), col(1)),
                  pl.BlockSpec((1, T_ALL, WC), col(2)),
                  pl.BlockSpec((1, T_ALL, WC), col(3)),
                  pl.BlockSpec((1, T_ALL, WC), col(4)),
                  pl.BlockSpec((2, WC), lambda i: (0, 0)),
                  pl.BlockSpec((1, LANES), lambda i: (0, 0)),
                  pl.BlockSpec((2, 2 * c, c), lambda i: (0, 0, 0)),
                  pl.BlockSpec((WC, WC), lambda i: (0, 0))],
        out_specs=pl.BlockSpec((1, T_ALL, WC), lambda i: (i, 0, 0)),
        out_shape=jax.ShapeDtypeStruct((b, T_ALL, WC), BF16),
        scratch_shapes=[pltpu.VMEM((T_ALL, WC), F32),
                        pltpu.VMEM((2, WC // LANES, LANES, LANES), F32),
                        pltpu.VMEM((2, c * HG_SUB, WC), BF16),
                        pltpu.VMEM((2, 2, 3, c, WC), F32)],
        compiler_params=_params("parallel"),
        name="mixer_c_hgrn2",
    )(yc, yc, yc, yc, yc, lb, gn, tri, ones)


def _outproj_kernel(oa_ref, ob_ref, oc_ref, x_ref, mod_ref, w_ref, g_ref, b_ref, o_ref):
    mix = jnp.concatenate([oa_ref[0], ob_ref[0], oc_ref[0]], axis=1)
    y = jnp.dot(mix, w_ref[...], preferred_element_type=F32)
    z = _ln(ALPHA * x_ref[0] + mod_ref[0, 0, 2:3, :] * y)
    o_ref[0] = z * g_ref[...] + b_ref[...]


def _out_projection(oa, ob, oc, xs, modsel, w_out_p, ln_g, ln_b, t0):
    b = xs.shape[0]
    row = lambda i, t: (i, t + t0, 0)
    full = lambda i, t: (0, 0)
    return pl.pallas_call(
        _outproj_kernel,
        grid=(b, N_TILES - t0),
        in_specs=[pl.BlockSpec((1, TM, WA), row),
                  pl.BlockSpec((1, TM, WB), row),
                  pl.BlockSpec((1, TM, WC), row),
                  pl.BlockSpec((1, TM, D), row),
                  pl.BlockSpec((1, 1, 8, D), lambda i, t: (i, jnp.minimum(t + t0, 1), 0, 0)),
                  pl.BlockSpec((D, D), full),
                  pl.BlockSpec((1, D), full),
                  pl.BlockSpec((1, D), full)],
        out_specs=pl.BlockSpec((1, TM, D), row),
        out_shape=jax.ShapeDtypeStruct((b, T_ALL, D), F32),
        compiler_params=_params("parallel", "arbitrary"),
        name="out_projection_ln1",
    )(oa, ob, oc, xs, modsel, w_out_p, ln_g, ln_b)


def _ffn_kernel(t0, xp_ref, x_ref, xn_ref, mod_ref, wup_ref, cw_ref, cb_ref, wdn_ref, g_ref, b_ref, o_ref):
    t = pl.program_id(1) + t0
    x = x_ref[0]
    xe = jnp.concatenate([xp_ref[0], x, xn_ref[0]], axis=0)
    h = _ln(xe) * (1.0 + mod_ref[0, 0, 4:5, :]) + mod_ref[0, 0, 3:4, :]
    r = lax.broadcasted_iota(jnp.int32, (HALO + TM + HALO, 1), 0)
    keep = ((r >= HALO) | (t >= 2)) & ((r < HALO + TM) | ((t >= 1) & (t < N_TILES - 1)))
    h = jnp.where(keep, h, 0.0).astype(BF16)
    n_e = HALO + TM + HALO
    n_chunk = D_FF // FF_CHUNK

    def up(c, base):
        return jnp.dot(h, wup_ref[:, base + c * FF_CHUNK:base + (c + 1) * FF_CHUNK], preferred_element_type=F32)

    def conv(u, c, base):
        cols = slice(base + c * FF_CHUNK, base + (c + 1) * FF_CHUNK)
        u = (pltpu.roll(u, 1, axis=0) * cw_ref[0:1, cols] + u * cw_ref[1:2, cols]
             + pltpu.roll(u, n_e - 1, axis=0) * cw_ref[2:3, cols])
        return u[HALO:HALO + TM, :] + cb_ref[:, cols]

    acc = jnp.zeros((TM, D), F32)
    ahead = 2
    pending = [(up(c, 0), up(c, D_FF)) for c in range(ahead)]
    for c in range(n_chunk):
        u_gate, u_val = pending.pop(0)
        if c + ahead < n_chunk:
            pending.append((up(c + ahead, 0), up(c + ahead, D_FF)))
        a = (_silu(conv(u_gate, c, 0)) * conv(u_val, c, D_FF)).astype(BF16)
        acc = acc + jnp.dot(a, wdn_ref[c * FF_CHUNK:(c + 1) * FF_CHUNK, :], preferred_element_type=F32)
    z = _ln(ALPHA * x + mod_ref[0, 0, 5:6, :] * acc)
    o_ref[0] = z * g_ref[...] + b_ref[...]


def _conv_ffn(x1, modsel, w_up, conv_w, conv_b, w_down, ln_g, ln_b, t0):
    b = x1.shape[0]
    per = TM // HALO
    n_halo = T_ALL // HALO
    row = lambda i, t: (i, t + t0, 0)
    full = lambda i, t: (0, 0)
    return pl.pallas_call(
        functools.partial(_ffn_kernel, t0),
        grid=(b, N_TILES - t0),
        in_specs=[pl.BlockSpec((1, HALO, D), lambda i, t: (i, jnp.maximum((t + t0) * per - 1, t0 * per), 0)),
                  pl.BlockSpec((1, TM, D), row),
                  pl.BlockSpec((1, HALO, D), lambda i, t: (i, jnp.minimum((t + t0 + 1) * per, n_halo - 1), 0)),
                  pl.BlockSpec((1, 1, 8, D), lambda i, t: (i, jnp.minimum(t + t0, 1), 0, 0)),
                  pl.BlockSpec((D, 2 * D_FF), full),
                  pl.BlockSpec((3, 2 * D_FF), full),
                  pl.BlockSpec((1, 2 * D_FF), full),
                  pl.BlockSpec((D_FF, D), full),
                  pl.BlockSpec((1, D), full),
                  pl.BlockSpec((1, D), full)],
        out_specs=pl.BlockSpec((1, TM, D), row),
        out_shape=jax.ShapeDtypeStruct((b, T_ALL, D), F32),
        compiler_params=_params("parallel", "arbitrary"),
        name="conv_ffn_ln2",
    )(x1, x1, x1, modsel, w_up, conv_w, conv_b, w_down, ln_g, ln_b)


def _in_col_perm():
    off, o = {}, 0
    for name, width in (('a_k', WA), ('a_v', WA), ('b_k', WKV), ('b_v', WKV), ('c_f', WC), ('c_b', WC),
                        ('c_i', WC), ('a_q', WA), ('b_q', WB), ('c_q', WC), ('c_g', WC)):
        off[name] = np.arange(o, o + width)
        o += width
    bq = off['b_q'].reshape(GQA_HEADS, HD)[list(GQA_ORDER)].reshape(-1)
    return np.concatenate([off['a_q'], off['a_k'], off['a_v'], bq, off['b_k'], off['b_v'],
                           off['c_f'], off['c_b'], off['c_i'], off['c_q'], off['c_g']])


def _out_row_perm():
    ob = (WA + np.arange(WB)).reshape(GQA_HEADS, HD)[list(GQA_ORDER)].reshape(-1)
    return np.concatenate([np.arange(WA), ob, WA + WB + np.arange(WC)])


def _take_runs(w, perm, axis):
    cuts = [0] + [i for i in range(1, len(perm)) if perm[i] != perm[i - 1] + 1] + [len(perm)]
    parts = [lax.slice_in_dim(w, int(perm[a]), int(perm[z - 1]) + 1, axis=axis) for a, z in zip(cuts[:-1], cuts[1:])]
    return jnp.concatenate(parts, axis=axis)


def kernel(x, c, ctx, c_ctx, w_ada, b_ada, w_in, w_out, na_rpb, q_norm, k_norm, hg_lower_bound, hg_norm,
           ln1_g, ln1_b, w_up, conv_w, conv_b, w_down, ln2_g, ln2_b):
    b = x.shape[0]
    assert x.shape == (b, SEQ, D) and ctx.shape == (b, CTX, D)
    n_rows = -(-(b + 1) // 8) * 8
    s_in = jnp.concatenate([c, c_ctx[None, :], jnp.zeros((n_rows - b - 1, D), F32)], axis=0)
    mods = _modulation(s_in, w_ada, b_ada).reshape(DEPTH, n_rows, N_MOD, D)
    lbs = _lower_bounds(hg_lower_bound).reshape(DEPTH, 2, WC)

    cos, sin = (jnp.asarray(a) for a in _rope_tables())
    tri = jnp.asarray(_hgrn_consts(), BF16)
    ones = jnp.asarray(np.kron(np.eye(HG_HEADS), np.ones((HD, HD))), BF16)
    in_perm = _in_col_perm()
    out_perm = _out_row_perm()
    tile2 = lambda g: jnp.tile(g.astype(F32), 2).reshape(1, LANES)

    xs = jnp.concatenate([ctx, x], axis=1)
    for l in range(DEPTH):
        last = l == DEPTH - 1
        t0 = 1 if last else 0
        lat = mods[l, :b]
        cx = jnp.broadcast_to(mods[l, b][None], lat.shape)
        modsel = jnp.pad(jnp.stack([cx, lat], axis=1), ((0, 0), (0, 0), (0, 8 - N_MOD), (0, 0)))
        ya, yb, yc = _in_projection(xs, modsel, _take_runs(w_in[l], in_perm, 1).astype(BF16))
        oa = _mixer_a(ya, _na_slabs(na_rpb[l]))
        ob = _mixer_b(yb, cos, sin, tile2(q_norm[l]), tile2(k_norm[l]))
        oc = _mixer_c(yc, lbs[l], tile2(hg_norm[l]), tri, ones)
        x1 = _out_projection(oa, ob, oc, xs, modsel, _take_runs(w_out[l], out_perm, 0).astype(BF16),
                             ln1_g[l].reshape(1, D), ln1_b[l].reshape(1, D), t0)
        xs = _conv_ffn(x1, modsel, w_up[l].astype(BF16), conv_w[l], conv_b[l].reshape(1, 2 * D_FF),
                       w_down[l].astype(BF16), ln2_g[l].reshape(1, D), ln2_b[l].reshape(1, D), t0)
    return xs[:, CTX:, :]
```

```python
import functools

import numpy as np
import jax
import jax.numpy as jnp
from jax import lax
from jax.experimental import pallas as pl
from jax.experimental.pallas import tpu as pltpu

D = 1024
DEPTH = 4
GRID_W = 64
CTX = 256
SEQ = 2048
ROWS = SEQ // GRID_W
T_ALL = CTX + SEQ
HD = 64
NA_HEADS = 6
NA_WIN_ROWS = 8
NA_WIN_COLS = 16
GQA_HEADS = 6
GQA_KV = 2
HG_HEADS = 4
HG_CHUNK = 64
HG_SUB = 16
ROPE_THETA = 10000.0
LB_FLOOR = 1e-30
WA = NA_HEADS * HD
WB = GQA_HEADS * HD
WKV = GQA_KV * HD
WC = HG_HEADS * HD
IN_COLS = 3072
D_FF = 2816
N_MOD = 6
ALPHA = (2.0 * DEPTH) ** 0.25
LN_EPS = 1e-6
RMS_EPS = 1e-6
ATT_SCALE = HD ** -0.5
LOG2E = 1.4426950408889634
NEG = -1e30

LANES = 128
TM = 256
N_TILES = T_ALL // TM
NA_RB = 4
NA_WR = NA_WIN_ROWS + NA_RB - 1
NA_NLOC = NA_WR * GRID_W
FF_CHUNK = 256
HALO = 8
MIX_HALO = 16
VMEM_LIMIT = 56 * 1024 * 1024

YA_COLS = 3 * WA
YB_COLS = WB + 2 * WKV
YC_COLS = 5 * WC
GQA_ORDER = (0, 3, 1, 4, 2, 5)

HI = lax.Precision.HIGHEST
F32 = jnp.float32
BF16 = jnp.bfloat16


def _params(*sem):
    return pltpu.CompilerParams(dimension_semantics=sem, vmem_limit_bytes=VMEM_LIMIT)


def _dot_nt(a, b):
    return lax.dot_general(a, b, (((1,), (1,)), ((), ())), preferred_element_type=F32)


def _dot_tn(a, b):
    return lax.dot_general(a, b, (((0,), (0,)), ((), ())), preferred_element_type=F32)


def _ln(x):
    mu = jnp.mean(x, axis=-1, keepdims=True)
    xc = x - mu
    var = jnp.mean(xc * xc, axis=-1, keepdims=True)
    return xc * lax.rsqrt(var + LN_EPS)


def _dot_01(m, x):
    hi = x.astype(BF16)
    rest = x - hi.astype(F32)
    mid = rest.astype(BF16)
    lo = (rest - mid.astype(F32)).astype(BF16)
    return (jnp.dot(m, hi, preferred_element_type=F32) + jnp.dot(m, mid, preferred_element_type=F32)
            + jnp.dot(m, lo, preferred_element_type=F32))


def _silu(x):
    return x * jax.nn.sigmoid(x)


def _lo_mask(shape):
    return lax.broadcasted_iota(jnp.int32, shape, len(shape) - 1) % LANES < HD


def _head_rms(x, gain):
    lo = _lo_mask(x.shape)
    x2 = x * x
    s_lo = jnp.sum(jnp.where(lo, x2, 0.0), axis=-1, keepdims=True)
    s_hi = jnp.sum(jnp.where(lo, 0.0, x2), axis=-1, keepdims=True)
    ms = jnp.where(lo, s_lo, s_hi) * (1.0 / HD)
    return x * lax.rsqrt(ms + RMS_EPS) * gain


def _mod_kernel(s_ref, w_ref, b_ref, o_ref):
    s = _silu(s_ref[...])
    o_ref[0] = jnp.dot(s, w_ref[0], precision=HI, preferred_element_type=F32) + b_ref[0]


def _modulation(s_in, w_ada, b_ada):
    r = s_in.shape[0]
    cb = 1536
    return pl.pallas_call(
        _mod_kernel,
        grid=(DEPTH, N_MOD * D // cb),
        in_specs=[pl.BlockSpec((r, D), lambda l, j: (0, 0)),
                  pl.BlockSpec((1, D, cb), lambda l, j: (l, 0, j)),
                  pl.BlockSpec((1, 1, cb), lambda l, j: (l, 0, j))],
        out_specs=pl.BlockSpec((1, r, cb), lambda l, j: (l, 0, j)),
        out_shape=jax.ShapeDtypeStruct((DEPTH, r, N_MOD * D), F32),
        compiler_params=_params("arbitrary", "arbitrary"),
        name="adaln_mod",
    )(s_in, w_ada, b_ada.reshape(DEPTH, 1, N_MOD * D))


def _lb_kernel(p_ref, o_ref):
    p = p_ref[...]
    e = jnp.exp(p - jnp.max(p, axis=0, keepdims=True))
    sm = e / jnp.sum(e, axis=0, keepdims=True)
    acc = sm[0:1]
    rows = [acc - sm[0:1]]
    for l in range(1, DEPTH):
        acc = acc + sm[l:l + 1]
        rows.append(acc - sm[0:1])
    o_ref[...] = jnp.concatenate(rows, axis=0)


def _lower_bounds(hg_lower_bound):
    p = hg_lower_bound.reshape(DEPTH, 2 * WC).astype(F32)
    return pl.pallas_call(
        _lb_kernel,
        out_shape=jax.ShapeDtypeStruct((DEPTH, 2 * WC), F32),
        name="hgrn_lower_bounds",
    )(p)


def _inproj_kernel(x_ref, mod_ref, w_ref, ya_ref, yb_ref, yc_ref):
    y = _ln(x_ref[0])
    h = (y * (1.0 + mod_ref[0, 0, 1:2, :]) + mod_ref[0, 0, 0:1, :]).astype(BF16)
    out = jnp.dot(h, w_ref[...], preferred_element_type=F32)
    ya_ref[0] = out[:, :YA_COLS].astype(BF16)
    yb_ref[0] = out[:, YA_COLS:YA_COLS + YB_COLS].astype(BF16)
    yc_ref[0] = out[:, YA_COLS + YB_COLS:]


def _in_projection(xs, modsel, w_in_p):
    b = xs.shape[0]
    row = lambda i, t: (i, t, 0)
    return pl.pallas_call(
        _inproj_kernel,
        grid=(b, N_TILES),
        in_specs=[pl.BlockSpec((1, TM, D), row),
                  pl.BlockSpec((1, 1, 8, D), lambda i, t: (i, jnp.minimum(t, 1), 0, 0)),
                  pl.BlockSpec((D, IN_COLS), lambda i, t: (0, 0))],
        out_specs=[pl.BlockSpec((1, TM, YA_COLS), row),
                   pl.BlockSpec((1, TM, YB_COLS), row),
                   pl.BlockSpec((1, TM, YC_COLS), row)],
        out_shape=[jax.ShapeDtypeStruct((b, T_ALL, YA_COLS), BF16),
                   jax.ShapeDtypeStruct((b, T_ALL, YB_COLS), BF16),
                   jax.ShapeDtypeStruct((b, T_ALL, YC_COLS), F32)],
        compiler_params=_params("parallel", "arbitrary"),
        name="in_projection",
    )(xs, modsel, w_in_p)


def _na_slab_index():
    def table(rb):
        r0 = NA_RB * rb
        ws = int(np.clip(r0 - NA_WIN_ROWS // 2, 0, ROWS - NA_WR))
        qr = r0 + np.arange(NA_RB)[:, None]
        kr = ws + np.arange(NA_WR)[None, :]
        rs = np.clip(qr - NA_WIN_ROWS // 2, 0, ROWS - NA_WIN_ROWS)
        valid = (kr >= rs) & (kr < rs + NA_WIN_ROWS)
        return np.where(valid, kr - qr + NA_WIN_ROWS - 1, 2 * NA_WIN_ROWS - 1)

    n_rb = ROWS // NA_RB
    for rb in range(2, n_rb - 1):
        assert np.array_equal(table(rb), table(1))
    return np.stack([table(0), table(1), table(n_rb - 1)])


def _na_slabs(rpb):
    qc = np.arange(GRID_W)[:, None]
    kc = np.arange(GRID_W)[None, :]
    cs = np.clip(qc - NA_WIN_COLS // 2, 0, GRID_W - NA_WIN_COLS)
    valid = (kc >= cs) & (kc < cs + NA_WIN_COLS)
    onehot = (valid[None] & ((kc - qc + NA_WIN_COLS - 1)[None] == np.arange(2 * NA_WIN_COLS - 1)[:, None, None]))
    onehot = jnp.asarray(onehot.reshape(2 * NA_WIN_COLS - 1, GRID_W * GRID_W), F32)
    slabs = jnp.einsum('hrd,dx->hrx', rpb.astype(F32), onehot, precision=HI)
    slabs = jnp.where(jnp.asarray(valid.reshape(-1)), slabs, NEG)
    slabs = jnp.concatenate([slabs, jnp.full((NA_HEADS, 1, GRID_W * GRID_W), NEG, F32)], axis=1)
    slabs = slabs.reshape(NA_HEADS, 2 * NA_WIN_ROWS, GRID_W, GRID_W)
    return jnp.concatenate([slabs, slabs], axis=-1)


def _na_kernel(idx, q_ref, k_ref, v_ref, slab_ref, o_ref, bias_ref):
    t = pl.program_id(1)
    n_blk = NA_HEADS // 2
    lo = _lo_mask((1, LANES))

    @pl.when((pl.program_id(0) == 0) & (t == 0))
    def _():
        for head in range(NA_HEADS):
            for typ in range(3):
                for rl in range(NA_RB):
                    rows = slice(rl * GRID_W, (rl + 1) * GRID_W)
                    for m in range(NA_WR // 2):
                        even = slab_ref[head, int(idx[typ, rl, 2 * m])]
                        odd = slab_ref[head, int(idx[typ, rl, 2 * m + 1])]
                        bias_ref[head, typ, rows, m * LANES:(m + 1) * LANES] = jnp.where(lo, even, odd)
                    if NA_WR % 2:
                        tail = slab_ref[head, int(idx[typ, rl, NA_WR - 1])]
                        bias_ref[head, typ, rows, (NA_WR - 1) * GRID_W:NA_NLOC] = tail[:, 0:GRID_W]

    def masked_queries(head):
        blk, half = divmod(head, 2)
        qs = q_ref[0, :, blk * LANES:(blk + 1) * LANES] * ATT_SCALE
        zero = jnp.zeros_like(qs)
        return jnp.where(lo, qs, zero) if half == 0 else jnp.where(lo, zero, qs)

    def cols(head):
        blk = head // 2
        return slice(blk * LANES, (blk + 1) * LANES)

    def attend(scores, finish):
        ahead = 2
        pending = [scores(h) for h in range(ahead)]
        outs = []
        for head in range(NA_HEADS):
            s = pending.pop(0)
            if head + ahead < NA_HEADS:
                pending.append(scores(head + ahead))
            outs.append(finish(head, s))
            if head % 2:
                o_ref[0, :, cols(head)] = jnp.where(lo, outs[head - 1], outs[head]).astype(o_ref.dtype)

    @pl.when(t == 0)
    def _():
        def scores(head):
            return _dot_nt(masked_queries(head), k_ref[0, 0:CTX, cols(head)])

        def finish(head, s):
            p = jnp.exp(s - jnp.max(s, axis=-1, keepdims=True))
            l = jnp.sum(p, axis=-1, keepdims=True)
            return jnp.dot(p.astype(BF16), v_ref[0, 0:CTX, cols(head)], preferred_element_type=F32) / l
        attend(scores, finish)

    @pl.when(t > 0)
    def _():
        rb = t - 1
        ws = jnp.clip(NA_RB * rb - NA_WIN_ROWS // 2, 0, ROWS - NA_WR)
        window = pl.ds(pl.multiple_of(CTX + ws * GRID_W, GRID_W), NA_NLOC)
        typ = jnp.where(rb == 0, 0, jnp.where(rb == ROWS // NA_RB - 1, 2, 1))

        def scores(head):
            qm = masked_queries(head)
            return (_dot_nt(qm, k_ref[0, window, cols(head)]) + bias_ref[head, typ],
                    _dot_nt(qm, k_ref[0, 0:CTX, cols(head)]))

        def finish(head, s):
            s_loc, s_ctx = s
            m = jnp.maximum(jnp.max(s_loc, axis=-1, keepdims=True), jnp.max(s_ctx, axis=-1, keepdims=True))
            p_loc = jnp.exp(s_loc - m)
            p_ctx = jnp.exp(s_ctx - m)
            l = jnp.sum(p_loc, axis=-1, keepdims=True) + jnp.sum(p_ctx, axis=-1, keepdims=True)
            o = (jnp.dot(p_loc.astype(BF16), v_ref[0, window, cols(head)], preferred_element_type=F32)
                 + jnp.dot(p_ctx.astype(BF16), v_ref[0, 0:CTX, cols(head)], preferred_element_type=F32))
            return o / l
        attend(scores, finish)


def _mixer_a(ya, slabs):
    b = ya.shape[0]
    return pl.pallas_call(
        functools.partial(_na_kernel, _na_slab_index()),
        grid=(b, N_TILES),
        in_specs=[pl.BlockSpec((1, TM, WA), lambda i, t: (i, t, 0)),
                  pl.BlockSpec((1, T_ALL, WA), lambda i, t: (i, 0, 1)),
                  pl.BlockSpec((1, T_ALL, WA), lambda i, t: (i, 0, 2)),
                  pl.BlockSpec((NA_HEADS, 2 * NA_WIN_ROWS, GRID_W, LANES), lambda i, t: (0, 0, 0, 0))],
        out_specs=pl.BlockSpec((1, TM, WA), lambda i, t: (i, t, 0)),
        out_shape=jax.ShapeDtypeStruct((b, T_ALL, WA), BF16),
        scratch_shapes=[pltpu.VMEM((NA_HEADS, 3, NA_RB * GRID_W, NA_NLOC), F32)],
        compiler_params=_params("arbitrary", "arbitrary"),
        name="mixer_a_neighbourhood",
    )(ya, ya, ya, slabs)


def _rope_tables():
    t = np.arange(SEQ)
    pos = np.stack([t // GRID_W, t % GRID_W]).astype(np.float32)
    n_freq = HD // 4
    inv_freq = (ROPE_THETA ** (-np.arange(n_freq, dtype=np.float32) / n_freq)).astype(np.float32)
    ang = pos[:, :, None] * inv_freq
    cos, sin = np.cos(ang), np.sin(ang)
    cos_h = np.concatenate([cos[0], cos[0], cos[1], cos[1]], axis=-1)
    sin_h = np.concatenate([-sin[0], sin[0], -sin[1], sin[1]], axis=-1)
    cos_all = np.concatenate([np.ones((CTX, HD), np.float32), cos_h], axis=0)
    sin_all = np.concatenate([np.zeros((CTX, HD), np.float32), sin_h], axis=0)
    return (np.tile(cos_all, (1, 2)).astype(np.float32), np.tile(sin_all, (1, 2)).astype(np.float32))


def _rope(x, cos, sin):
    quarter = HD // 4
    first = lax.broadcasted_iota(jnp.int32, x.shape, 1) % (2 * quarter) < quarter
    partner = jnp.where(first, pltpu.roll(x, LANES - quarter, axis=1), pltpu.roll(x, quarter, axis=1))
    return x * cos + partner * sin


def _gqa_kernel(q_ref, k_ref, v_ref, cos_ref, sin_ref, qg_ref, kg_ref, o_ref, kn_ref):
    t = pl.program_id(1)
    lo = _lo_mask((1, LANES))

    @pl.when(t == 0)
    def _():
        def prep(i, carry):
            rows = pl.ds(pl.multiple_of(i * TM, TM), TM)
            kx = _head_rms(k_ref[0, rows, :].astype(F32), kg_ref[...])
            kn_ref[rows, :] = _rope(kx, cos_ref[rows, :], sin_ref[rows, :]).astype(BF16)
            return carry
        lax.fori_loop(0, N_TILES, prep, 0)

    rows = pl.ds(pl.multiple_of(t * TM, TM), TM)
    cos = cos_ref[rows, :]
    sin = sin_ref[rows, :]

    def attend(n_keys):
        kn = kn_ref[0:n_keys, :]
        vv = v_ref[0, 0:n_keys, :]

        qn = []
        for j in range(GQA_HEADS // 2):
            qx = _head_rms(q_ref[0, :, j * LANES:(j + 1) * LANES].astype(F32), qg_ref[...])
            qn.append((_rope(qx, cos, sin) * ATT_SCALE).astype(BF16))
        zero = jnp.zeros_like(qn[0])

        def scores(i):
            j, h = divmod(i, 2)
            return _dot_nt(jnp.where(lo, qn[j], zero) if h == 0 else jnp.where(lo, zero, qn[j]), kn)

        s_next = scores(0)
        outs = []
        for i in range(GQA_HEADS):
            s = s_next
            if i + 1 < GQA_HEADS:
                s_next = scores(i + 1)
            p = jnp.exp(s - jnp.max(s, axis=-1, keepdims=True))
            l = jnp.sum(p, axis=-1, keepdims=True)
            outs.append(jnp.dot(p.astype(BF16), vv, preferred_element_type=F32) / l)
            if i % 2:
                j = i // 2
                o_ref[0, :, j * LANES:(j + 1) * LANES] = jnp.where(lo, outs[i - 1], outs[i]).astype(o_ref.dtype)

    @pl.when(t == 0)
    def _():
        attend(CTX)

    @pl.when(t > 0)
    def _():
        attend(T_ALL)


def _mixer_b(yb, cos, sin, qg, kg):
    b = yb.shape[0]
    kcol = WB // LANES
    full = lambda i, t: (0, 0)
    return pl.pallas_call(
        _gqa_kernel,
        grid=(b, N_TILES),
        in_specs=[pl.BlockSpec((1, TM, WB), lambda i, t: (i, t, 0)),
                  pl.BlockSpec((1, T_ALL, LANES), lambda i, t: (i, 0, kcol)),
                  pl.BlockSpec((1, T_ALL, LANES), lambda i, t: (i, 0, kcol + 1)),
                  pl.BlockSpec((T_ALL, LANES), full),
                  pl.BlockSpec((T_ALL, LANES), full),
                  pl.BlockSpec((1, LANES), full),
                  pl.BlockSpec((1, LANES), full)],
        out_specs=pl.BlockSpec((1, TM, WB), lambda i, t: (i, t, 0)),
        out_shape=jax.ShapeDtypeStruct((b, T_ALL, WB), BF16),
        scratch_shapes=[pltpu.VMEM((T_ALL, LANES), BF16)],
        compiler_params=_params("parallel", "arbitrary"),
        name="mixer_b_gqa",
    )(yb, yb, yb, cos, sin, qg, kg)


def _hgrn_consts():
    c, sc = HG_CHUNK, HG_SUB
    t = np.arange(c)
    out = []
    for reverse in (False, True):
        if not reverse:
            cum = t[None, :] <= t[:, None]
            through = (t[None, :] // sc) <= (t[:, None] // sc)
        else:
            cum = t[None, :] >= t[:, None]
            through = (t[None, :] // sc) >= (t[:, None] // sc)
        out.append(np.concatenate([cum, through], axis=0).astype(np.float32))
    return np.stack(out)


def _forget_gate(z, lb):
    ez = jnp.exp(-jnp.abs(z))
    big = 1.0 / (1.0 + ez)
    small = ez * big
    pos = z >= 0.0
    lb_floor = jnp.maximum(lb, LB_FLOOR)
    f = lb_floor + (1.0 - lb) * jnp.where(pos, big, small)
    one_minus_f = (1.0 - lb) * jnp.where(pos, small, big) - (lb_floor - lb)
    return jnp.log(f), one_minus_f


def _hgrn_kernel(zf_ref, zb_ref, v_ref, q_ref, g_ref, lb_ref, gn_ref, tri_ref, ones_ref,
                 o_ref, acc_ref, st_ref, d_ref, b_ref):
    c, sc = HG_CHUNK, HG_SUB
    n_sub = c // sc
    n_pair = WC // LANES
    acc_ref[...] = jnp.zeros_like(acc_ref)
    st_ref[...] = jnp.zeros_like(st_ref)
    sub_row = lax.broadcasted_iota(jnp.int32, (sc, 1), 0)
    blk = (lax.broadcasted_iota(jnp.int32, (LANES, LANES), 0) // HD
           == lax.broadcasted_iota(jnp.int32, (LANES, LANES), 1) // HD)
    own_head = (lax.broadcasted_iota(jnp.int32, (HG_HEADS * c, WC), 0) // c
                == lax.broadcasted_iota(jnp.int32, (HG_HEADS * c, WC), 1) // HD)
    q_sub = lax.broadcasted_iota(jnp.int32, (c, WC), 0) // sc
    k_sub = (lax.broadcasted_iota(jnp.int32, (c, WC), 1) % c) // sc
    k_lane = lax.broadcasted_iota(jnp.int32, (1, WC), 1) % c

    z_refs = (zf_ref, zb_ref)
    n_ctx = CTX // c
    n_all = T_ALL // c

    def chunk_of(d, n):
        return n if d == 0 else jnp.where(n < n_ctx, n_ctx - 1 - n, n_all + n_ctx - 1 - n)

    def rows_of(d, n):
        return pl.ds(pl.multiple_of(chunk_of(d, n) * c, c), c)

    def gates(d, n, slot):
        g, kk = _forget_gate(z_refs[d][0, rows_of(d, n), :], lb_ref[d:d + 1, :])
        cums = _dot_01(tri_ref[d], g * LOG2E)
        b_ref[d, slot, 0] = cums[0:c]
        b_ref[d, slot, 1] = kk
        b_ref[d, slot, 2] = cums[c:2 * c]

    def state_and_far(d, n, slot):
        reverse = d == 1
        rows = rows_of(d, n)
        b = b_ref[d, slot, 0]
        kk = b_ref[d, slot, 1]
        e = b_ref[d, slot, 2]
        q = q_ref[0, rows, :] * ATT_SCALE
        vb = v_ref[0, rows, :].astype(BF16)
        last = c - 1 if not reverse else 0
        b_last = b_ref[d, slot, 0, last:last + 1, :]

        qe = (q * jnp.exp2(b)).astype(BF16)
        kdec = (kk * jnp.exp2(b_last - b)).astype(BF16)
        o_parts = []
        for p in range(n_pair):
            ln = slice(p * LANES, (p + 1) * LANES)
            st = st_ref[d, p]
            o_parts.append(_dot_nt(qe[:, ln], st.astype(BF16)))
            upd = _dot_tn(vb[:, ln], kdec[:, ln])
            st_ref[d, p] = jnp.where(blk, st * jnp.exp2(b_last[:, ln]) + upd, 0.0)
        o = jnp.concatenate(o_parts, axis=1)

        kx = (kk * jnp.exp2(e - b)).astype(BF16)
        keys = jnp.where(own_head, jnp.concatenate([kx] * HG_HEADS, axis=0), jnp.zeros((), BF16))
        vals = jnp.where(own_head, jnp.concatenate([vb] * HG_HEADS, axis=0), jnp.zeros((), BF16))
        q_stack = []
        for j in range(n_sub - 1):
            jj = j if not reverse else n_sub - 1 - j
            edge = (jj + 1) * sc - 1 if not reverse else jj * sc
            e_j = b_ref[d, slot, 0, edge:edge + 1, :]
            q_stack.append((q * jnp.exp2(jnp.minimum(b - e_j, 0.0))).astype(BF16))
        res = _dot_nt(jnp.concatenate(q_stack, axis=0), keys)
        att = jnp.zeros((c, WC), F32)
        for j in range(n_sub - 1):
            jj = j if not reverse else n_sub - 1 - j
            later = q_sub > jj if not reverse else q_sub < jj
            att = att + jnp.where((k_sub == jj) & later, res[j * c:(j + 1) * c, :], 0.0)
        return dict(rows=rows, b=b, q=q, o=o, att=att, vals=vals)

    def near_products(d, slot, w):
        reverse = d == 1
        for i in range(n_sub):
            bi = w['b'][i * sc:(i + 1) * sc, :]
            qi = w['q'][i * sc:(i + 1) * sc, :]
            for s in range(sc):
                r = i * sc + s
                p = jnp.exp2(bi - b_ref[d, slot, 0, r:r + 1, :]) * (qi * b_ref[d, slot, 1, r:r + 1, :])
                keep = sub_row >= s if not reverse else sub_row <= s
                d_ref[d, r * sc:(r + 1) * sc, :] = jnp.where(keep, p, 0.0).astype(BF16)
        w['sums'] = jnp.dot(d_ref[d], ones_ref[...], preferred_element_type=F32)

    def finish(d, w):
        near = []
        for i in range(n_sub):
            a = jnp.zeros((sc, WC), F32)
            for s in range(sc):
                r = i * sc + s
                a = jnp.where(k_lane == r, w['sums'][r * sc:(r + 1) * sc, :], a)
            near.append(a)
        att = w['att'] + jnp.concatenate(near, axis=0)
        acc_ref[w['rows'], :] += w['o'] + jnp.dot(att.astype(BF16), w['vals'], preferred_element_type=F32)

    for d in range(2):
        gates(d, 0, 0)

    def body(m, carry):
        for slot in range(2):
            n = 2 * m + slot
            for d in range(2):
                gates(d, jnp.minimum(n + 1, n_all - 1), 1 - slot)
            work = [state_and_far(d, n, slot) for d in range(2)]
            for d in range(2):
                near_products(d, slot, work[d])
            for d in range(2):
                finish(d, work[d])
        return carry
    lax.fori_loop(0, n_all // 2, body, 0)

    def readout(i, carry):
        rows = pl.ds(pl.multiple_of(i * TM, TM), TM)
        gate = _silu(g_ref[0, rows, :])
        for p in range(n_pair):
            ln = slice(p * LANES, (p + 1) * LANES)
            y = _head_rms(acc_ref[rows, ln], gn_ref[...]) * gate[:, ln]
            o_ref[0, rows, ln] = y.astype(o_ref.dtype)
        return carry
    lax.fori_loop(0, N_TILES, readout, 0)


def _mixer_c(yc, lb, gn, tri, ones):
    b = yc.shape[0]
    c = HG_CHUNK
    col = lambda j: (lambda i: (i, 0, j))
    return pl.pallas_call(
        _hgrn_kernel,
        grid=(b,),
        in_specs=[pl.BlockSpec((1, T_ALL, WC), col(0)),
                  pl.BlockSpec((1, T_ALL, WC), col(1)),
                  pl.BlockSpec((1, T_ALL, WC), col(2)),
                  pl.BlockSpec((1, T_ALL, WC), col(3)),
                  pl.BlockSpec((1, T_ALL, WC), col(4)),
                  pl.BlockSpec((2, WC), lambda i: (0, 0)),
                  pl.BlockSpec((1, LANES), lambda i: (0, 0)),
                  pl.BlockSpec((2, 2 * c, c), lambda i: (0, 0, 0)),
                  pl.BlockSpec((WC, WC), lambda i: (0, 0))],
        out_specs=pl.BlockSpec((1, T_ALL, WC), lambda i: (i, 0, 0)),
        out_shape=jax.ShapeDtypeStruct((b, T_ALL, WC), BF16),
        scratch_shapes=[pltpu.VMEM((T_ALL, WC), F32),
                        pltpu.VMEM((2, WC // LANES, LANES, LANES), F32),
                        pltpu.VMEM((2, c * HG_SUB, WC), BF16),
                        pltpu.VMEM((2, 2, 3, c, WC), F32)],
        compiler_params=_params("parallel"),
        name="mixer_c_hgrn2",
    )(yc, yc, yc, yc, yc, lb, gn, tri, ones)


def _outproj_ffn_kernel(t0, ap_ref, a_ref, an_ref, bp_ref, b_ref, bn_ref, cp_ref, c_ref, cn_ref, xp_ref, x_ref, xn_ref,
                        mod_ref, wout_ref, g1_ref, b1_ref, wup_ref, cw_ref, cb_ref, wdn_ref, g2_ref, b2_ref, o_ref):
    t = pl.program_id(1) + t0
    n_e = HALO + TM + HALO

    mix = jnp.concatenate([jnp.concatenate([r[0] for r in refs], axis=1)
                           for refs in ((ap_ref, bp_ref, cp_ref), (a_ref, b_ref, c_ref), (an_ref, bn_ref, cn_ref))],
                          axis=0)
    y = jnp.dot(mix, wout_ref[...], preferred_element_type=F32)[MIX_HALO - HALO:MIX_HALO + TM + HALO, :]
    xe = jnp.concatenate([xp_ref[0], x_ref[0], xn_ref[0]], axis=0)
    x1e = _ln(ALPHA * xe + mod_ref[0, 0, 2:3, :] * y) * g1_ref[...] + b1_ref[...]
    x1 = x1e[HALO:HALO + TM, :]

    h = _ln(x1e) * (1.0 + mod_ref[0, 0, 4:5, :]) + mod_ref[0, 0, 3:4, :]
    r = lax.broadcasted_iota(jnp.int32, (n_e, 1), 0)
    keep = ((r >= HALO) | (t >= 2)) & ((r < HALO + TM) | ((t >= 1) & (t < N_TILES - 1)))
    h = jnp.where(keep, h, 0.0).astype(BF16)
    n_chunk = D_FF // FF_CHUNK

    def up(c, base):
        return jnp.dot(h, wup_ref[:, base + c * FF_CHUNK:base + (c + 1) * FF_CHUNK], preferred_element_type=F32)

    def conv(u, c, base):
        cols = slice(base + c * FF_CHUNK, base + (c + 1) * FF_CHUNK)
        u = (pltpu.roll(u, 1, axis=0) * cw_ref[0:1, cols] + u * cw_ref[1:2, cols]
             + pltpu.roll(u, n_e - 1, axis=0) * cw_ref[2:3, cols])
        return u[HALO:HALO + TM, :] + cb_ref[:, cols]

    acc = jnp.zeros((TM, D), F32)
    ahead = 2
    pending = [(up(c, 0), up(c, D_FF)) for c in range(ahead)]
    for c in range(n_chunk):
        u_gate, u_val = pending.pop(0)
        if c + ahead < n_chunk:
            pending.append((up(c + ahead, 0), up(c + ahead, D_FF)))
        a = (_silu(conv(u_gate, c, 0)) * conv(u_val, c, D_FF)).astype(BF16)
        acc = acc + jnp.dot(a, wdn_ref[c * FF_CHUNK:(c + 1) * FF_CHUNK, :], preferred_element_type=F32)
    z = _ln(ALPHA * x1 + mod_ref[0, 0, 5:6, :] * acc)
    o_ref[0] = z * g2_ref[...] + b2_ref[...]


def _outproj_ffn(oa, ob, oc, xs, modsel, w_out_p, ln1_g, ln1_b, w_up, conv_w, conv_b, w_down, ln2_g, ln2_b, t0):
    b = xs.shape[0]
    row = lambda i, t: (i, t + t0, 0)
    full = lambda i, t: (0, 0)

    def halos(width, rows):
        per, last = TM // rows, T_ALL // rows - 1
        return (pl.BlockSpec((1, rows, width), lambda i, t: (i, jnp.maximum((t + t0) * per - 1, t0 * per), 0)),
                pl.BlockSpec((1, TM, width), row),
                pl.BlockSpec((1, rows, width), lambda i, t: (i, jnp.minimum((t + t0 + 1) * per, last), 0)))

    return pl.pallas_call(
        functools.partial(_outproj_ffn_kernel, t0),
        grid=(b, N_TILES - t0),
        in_specs=[*halos(WA, MIX_HALO), *halos(WB, MIX_HALO), *halos(WC, MIX_HALO), *halos(D, HALO),
                  pl.BlockSpec((1, 1, 8, D), lambda i, t: (i, jnp.minimum(t + t0, 1), 0, 0)),
                  pl.BlockSpec((D, D), full),
                  pl.BlockSpec((1, D), full),
                  pl.BlockSpec((1, D), full),
                  pl.BlockSpec((D, 2 * D_FF), full),
                  pl.BlockSpec((3, 2 * D_FF), full),
                  pl.BlockSpec((1, 2 * D_FF), full),
                  pl.BlockSpec((D_FF, D), full),
                  pl.BlockSpec((1, D), full),
                  pl.BlockSpec((1, D), full)],
        out_specs=pl.BlockSpec((1, TM, D), lambda i, t: (i, t, 0)),
        out_shape=jax.ShapeDtypeStruct((b, T_ALL - t0 * TM, D), F32),
        compiler_params=_params("parallel", "arbitrary"),
        name="outproj_ffn",
    )(oa, oa, oa, ob, ob, ob, oc, oc, oc, xs, xs, xs, modsel, w_out_p, ln1_g, ln1_b,
      w_up, conv_w, conv_b, w_down, ln2_g, ln2_b)


def _in_col_perm():
    off, o = {}, 0
    for name, width in (('a_k', WA), ('a_v', WA), ('b_k', WKV), ('b_v', WKV), ('c_f', WC), ('c_b', WC),
                        ('c_i', WC), ('a_q', WA), ('b_q', WB), ('c_q', WC), ('c_g', WC)):
        off[name] = np.arange(o, o + width)
        o += width
    bq = off['b_q'].reshape(GQA_HEADS, HD)[list(GQA_ORDER)].reshape(-1)
    return np.concatenate([off['a_q'], off['a_k'], off['a_v'], bq, off['b_k'], off['b_v'],
                           off['c_f'], off['c_b'], off['c_i'], off['c_q'], off['c_g']])


def _out_row_perm():
    ob = (WA + np.arange(WB)).reshape(GQA_HEADS, HD)[list(GQA_ORDER)].reshape(-1)
    return np.concatenate([np.arange(WA), ob, WA + WB + np.arange(WC)])


def _take_runs(w, perm, axis):
    cuts = [0] + [i for i in range(1, len(perm)) if perm[i] != perm[i - 1] + 1] + [len(perm)]
    parts = [lax.slice_in_dim(w, int(perm[a]), int(perm[z - 1]) + 1, axis=axis) for a, z in zip(cuts[:-1], cuts[1:])]
    return jnp.concatenate(parts, axis=axis)


def kernel(x, c, ctx, c_ctx, w_ada, b_ada, w_in, w_out, na_rpb, q_norm, k_norm, hg_lower_bound, hg_norm,
           ln1_g, ln1_b, w_up, conv_w, conv_b, w_down, ln2_g, ln2_b):
    b = x.shape[0]
    assert x.shape == (b, SEQ, D) and ctx.shape == (b, CTX, D)
    n_rows = -(-(b + 1) // 8) * 8
    s_in = jnp.concatenate([c, c_ctx[None, :], jnp.zeros((n_rows - b - 1, D), F32)], axis=0)
    mods = _modulation(s_in, w_ada, b_ada).reshape(DEPTH, n_rows, N_MOD, D)
    lbs = _lower_bounds(hg_lower_bound).reshape(DEPTH, 2, WC)

    cos, sin = (jnp.asarray(a) for a in _rope_tables())
    tri = jnp.asarray(_hgrn_consts(), BF16)
    ones = jnp.asarray(np.kron(np.eye(HG_HEADS), np.ones((HD, HD))), BF16)
    in_perm = _in_col_perm()
    out_perm = _out_row_perm()
    tile2 = lambda g: jnp.tile(g.astype(F32), 2).reshape(1, LANES)

    xs = jnp.concatenate([ctx, x], axis=1)
    for l in range(DEPTH):
        last = l == DEPTH - 1
        t0 = 1 if last else 0
        lat = mods[l, :b]
        cx = jnp.broadcast_to(mods[l, b][None], lat.shape)
        modsel = jnp.pad(jnp.stack([cx, lat], axis=1), ((0, 0), (0, 0), (0, 8 - N_MOD), (0, 0)))
        ya, yb, yc = _in_projection(xs, modsel, _take_runs(w_in[l], in_perm, 1).astype(BF16))
        oa = _mixer_a(ya, _na_slabs(na_rpb[l]))
        ob = _mixer_b(yb, cos, sin, tile2(q_norm[l]), tile2(k_norm[l]))
        oc = _mixer_c(yc, lbs[l], tile2(hg_norm[l]), tri, ones)
        xs = _outproj_ffn(oa, ob, oc, xs, modsel, _take_runs(w_out[l], out_perm, 0).astype(BF16),
                          ln1_g[l].reshape(1, D), ln1_b[l].reshape(1, D), w_up[l].astype(BF16), conv_w[l],
                          conv_b[l].reshape(1, 2 * D_FF), w_down[l].astype(BF16),
                          ln2_g[l].reshape(1, D), ln2_b[l].reshape(1, D), t0)
    return xs
```

```python
import functools

import numpy as np
import jax
import jax.numpy as jnp
from jax import lax
from jax.experimental import pallas as pl
from jax.experimental.pallas import tpu as pltpu

D = 1024
DEPTH = 4
GRID_W = 64
CTX = 256
SEQ = 2048
ROWS = SEQ // GRID_W
T_ALL = CTX + SEQ
HD = 64
NA_HEADS = 6
NA_WIN_ROWS = 8
NA_WIN_COLS = 16
GQA_HEADS = 6
GQA_KV = 2
HG_HEADS = 4
HG_CHUNK = 64
HG_SUB = 16
ROPE_THETA = 10000.0
LB_FLOOR = 1e-30
WA = NA_HEADS * HD
WB = GQA_HEADS * HD
WKV = GQA_KV * HD
WC = HG_HEADS * HD
IN_COLS = 3072
D_FF = 2816
N_MOD = 6
ALPHA = (2.0 * DEPTH) ** 0.25
LN_EPS = 1e-6
RMS_EPS = 1e-6
ATT_SCALE = HD ** -0.5
LOG2E = 1.4426950408889634
NEG = -1e30

LANES = 128
TM = 256
N_TILES = T_ALL // TM
NA_RB = 4
NA_WR = NA_WIN_ROWS + NA_RB - 1
NA_NLOC = NA_WR * GRID_W
FF_CHUNK = 256
FFN_TILES_PER_STEP = 1
IN_TILES = 3
HALO = 8
MIX_HALO = 16
VMEM_LIMIT = 56 * 1024 * 1024

YA_COLS = 3 * WA
YB_COLS = WB + 2 * WKV
YC_COLS = 5 * WC
GQA_ORDER = (0, 3, 1, 4, 2, 5)

HI = lax.Precision.HIGHEST
F32 = jnp.float32
BF16 = jnp.bfloat16


def _params(*sem):
    return pltpu.CompilerParams(dimension_semantics=sem, vmem_limit_bytes=VMEM_LIMIT)


def _dot_nt(a, b):
    return lax.dot_general(a, b, (((1,), (1,)), ((), ())), preferred_element_type=F32)


def _dot_tn(a, b):
    return lax.dot_general(a, b, (((0,), (0,)), ((), ())), preferred_element_type=F32)


def _ln(x):
    mu = jnp.mean(x, axis=-1, keepdims=True)
    xc = x - mu
    var = jnp.mean(xc * xc, axis=-1, keepdims=True)
    return xc * lax.rsqrt(var + LN_EPS)


def _dot_01(m, x):
    hi = x.astype(BF16)
    rest = x - hi.astype(F32)
    mid = rest.astype(BF16)
    lo = (rest - mid.astype(F32)).astype(BF16)
    return (jnp.dot(m, hi, preferred_element_type=F32) + jnp.dot(m, mid, preferred_element_type=F32)
            + jnp.dot(m, lo, preferred_element_type=F32))


def _silu(x):
    return x * jax.nn.sigmoid(x)


def _lo_mask(shape):
    return lax.broadcasted_iota(jnp.int32, shape, len(shape) - 1) % LANES < HD


def _head_rms(x, gain):
    lo = _lo_mask(x.shape)
    x2 = x * x
    s_lo = jnp.sum(jnp.where(lo, x2, 0.0), axis=-1, keepdims=True)
    s_hi = jnp.sum(jnp.where(lo, 0.0, x2), axis=-1, keepdims=True)
    ms = jnp.where(lo, s_lo, s_hi) * (1.0 / HD)
    return x * lax.rsqrt(ms + RMS_EPS) * gain


def _mod_kernel(s_ref, w_ref, b_ref, o_ref):
    s = _silu(s_ref[...])
    o_ref[0] = jnp.dot(s, w_ref[0], precision=HI, preferred_element_type=F32) + b_ref[0]


def _modulation(s_in, w_ada, b_ada):
    r = s_in.shape[0]
    cb = 1536
    return pl.pallas_call(
        _mod_kernel,
        grid=(DEPTH, N_MOD * D // cb),
        in_specs=[pl.BlockSpec((r, D), lambda l, j: (0, 0)),
                  pl.BlockSpec((1, D, cb), lambda l, j: (l, 0, j)),
                  pl.BlockSpec((1, 1, cb), lambda l, j: (l, 0, j))],
        out_specs=pl.BlockSpec((1, r, cb), lambda l, j: (l, 0, j)),
        out_shape=jax.ShapeDtypeStruct((DEPTH, r, N_MOD * D), F32),
        compiler_params=_params("arbitrary", "arbitrary"),
        name="adaln_mod",
    )(s_in, w_ada, b_ada.reshape(DEPTH, 1, N_MOD * D))


def _lb_kernel(p_ref, o_ref):
    p = p_ref[...]
    e = jnp.exp(p - jnp.max(p, axis=0, keepdims=True))
    sm = e / jnp.sum(e, axis=0, keepdims=True)
    acc = sm[0:1]
    rows = [acc - sm[0:1]]
    for l in range(1, DEPTH):
        acc = acc + sm[l:l + 1]
        rows.append(acc - sm[0:1])
    o_ref[...] = jnp.concatenate(rows, axis=0)


def _lower_bounds(hg_lower_bound):
    p = hg_lower_bound.reshape(DEPTH, 2 * WC).astype(F32)
    return pl.pallas_call(
        _lb_kernel,
        out_shape=jax.ShapeDtypeStruct((DEPTH, 2 * WC), F32),
        name="hgrn_lower_bounds",
    )(p)


def _inproj_kernel(x_ref, mod_ref, w_ref, ya_ref, yb_ref, yc_ref):
    step = pl.program_id(1)
    hs = []
    for j in range(IN_TILES):
        mods = mod_ref[0, 1] if j > 0 else jnp.where(step == 0, mod_ref[0, 0], mod_ref[0, 1])
        y = _ln(x_ref[0, j * TM:(j + 1) * TM, :])
        hs.append((y * (1.0 + mods[1:2, :]) + mods[0:1, :]).astype(BF16))
    for j, h in enumerate(hs):
        rows = slice(j * TM, (j + 1) * TM)
        out = jnp.dot(h, w_ref[...], preferred_element_type=F32)
        ya_ref[0, rows, :] = out[:, :YA_COLS].astype(BF16)
        yb_ref[0, rows, :] = out[:, YA_COLS:YA_COLS + YB_COLS].astype(BF16)
        yc_ref[0, rows, :] = out[:, YA_COLS + YB_COLS:]


def _in_projection(xs, modsel, w_in_p):
    b = xs.shape[0]
    row = lambda i, t: (i, t, 0)
    return pl.pallas_call(
        _inproj_kernel,
        grid=(b, N_TILES // IN_TILES),
        in_specs=[pl.BlockSpec((1, IN_TILES * TM, D), row),
                  pl.BlockSpec((1, 2, 8, D), lambda i, t: (i, 0, 0, 0)),
                  pl.BlockSpec((D, IN_COLS), lambda i, t: (0, 0))],
        out_specs=[pl.BlockSpec((1, IN_TILES * TM, YA_COLS), row),
                   pl.BlockSpec((1, IN_TILES * TM, YB_COLS), row),
                   pl.BlockSpec((1, IN_TILES * TM, YC_COLS), row)],
        out_shape=[jax.ShapeDtypeStruct((b, T_ALL, YA_COLS), BF16),
                   jax.ShapeDtypeStruct((b, T_ALL, YB_COLS), BF16),
                   jax.ShapeDtypeStruct((b, T_ALL, YC_COLS), F32)],
        compiler_params=_params("parallel", "arbitrary"),
        name="in_projection",
    )(xs, modsel, w_in_p)


def _na_slab_index():
    def table(rb):
        r0 = NA_RB * rb
        ws = int(np.clip(r0 - NA_WIN_ROWS // 2, 0, ROWS - NA_WR))
        qr = r0 + np.arange(NA_RB)[:, None]
        kr = ws + np.arange(NA_WR)[None, :]
        rs = np.clip(qr - NA_WIN_ROWS // 2, 0, ROWS - NA_WIN_ROWS)
        valid = (kr >= rs) & (kr < rs + NA_WIN_ROWS)
        return np.where(valid, kr - qr + NA_WIN_ROWS - 1, 2 * NA_WIN_ROWS - 1)

    n_rb = ROWS // NA_RB
    for rb in range(2, n_rb - 1):
        assert np.array_equal(table(rb), table(1))
    return np.stack([table(0), table(1), table(n_rb - 1)])


def _na_slabs(rpb):
    qc = np.arange(GRID_W)[:, None]
    kc = np.arange(GRID_W)[None, :]
    cs = np.clip(qc - NA_WIN_COLS // 2, 0, GRID_W - NA_WIN_COLS)
    valid = (kc >= cs) & (kc < cs + NA_WIN_COLS)
    onehot = (valid[None] & ((kc - qc + NA_WIN_COLS - 1)[None] == np.arange(2 * NA_WIN_COLS - 1)[:, None, None]))
    onehot = jnp.asarray(onehot.reshape(2 * NA_WIN_COLS - 1, GRID_W * GRID_W), F32)
    slabs = jnp.einsum('hrd,dx->hrx', rpb.astype(F32), onehot, precision=HI)
    slabs = jnp.where(jnp.asarray(valid.reshape(-1)), slabs, NEG)
    slabs = jnp.concatenate([slabs, jnp.full((NA_HEADS, 1, GRID_W * GRID_W), NEG, F32)], axis=1)
    slabs = slabs.reshape(NA_HEADS, 2 * NA_WIN_ROWS, GRID_W, GRID_W)
    return jnp.concatenate([slabs, slabs], axis=-1)


def _na_kernel(idx, q_ref, k_ref, v_ref, slab_ref, o_ref, bias_ref):
    t = pl.program_id(1)
    n_blk = NA_HEADS // 2
    lo = _lo_mask((1, LANES))

    @pl.when((pl.program_id(0) == 0) & (t == 0))
    def _():
        for head in range(NA_HEADS):
            for typ in range(3):
                for rl in range(NA_RB):
                    rows = slice(rl * GRID_W, (rl + 1) * GRID_W)
                    for m in range(NA_WR // 2):
                        even = slab_ref[head, int(idx[typ, rl, 2 * m])]
                        odd = slab_ref[head, int(idx[typ, rl, 2 * m + 1])]
                        bias_ref[head, typ, rows, m * LANES:(m + 1) * LANES] = jnp.where(lo, even, odd)
                    if NA_WR % 2:
                        tail = slab_ref[head, int(idx[typ, rl, NA_WR - 1])]
                        bias_ref[head, typ, rows, (NA_WR - 1) * GRID_W:NA_NLOC] = tail[:, 0:GRID_W]

    def masked_queries(head):
        blk, half = divmod(head, 2)
        qs = q_ref[0, :, blk * LANES:(blk + 1) * LANES] * ATT_SCALE
        zero = jnp.zeros_like(qs)
        return jnp.where(lo, qs, zero) if half == 0 else jnp.where(lo, zero, qs)

    def cols(head):
        blk = head // 2
        return slice(blk * LANES, (blk + 1) * LANES)

    def attend(scores, finish):
        ahead = 2
        pending = [scores(h) for h in range(ahead)]
        outs = []
        for head in range(NA_HEADS):
            s = pending.pop(0)
            if head + ahead < NA_HEADS:
                pending.append(scores(head + ahead))
            outs.append(finish(head, s))
            if head % 2:
                o_ref[0, :, cols(head)] = jnp.where(lo, outs[head - 1], outs[head]).astype(o_ref.dtype)

    @pl.when(t == 0)
    def _():
        def scores(head):
            return _dot_nt(masked_queries(head), k_ref[0, 0:CTX, cols(head)])

        def finish(head, s):
            p = jnp.exp(s - jnp.max(s, axis=-1, keepdims=True))
            l = jnp.sum(p, axis=-1, keepdims=True)
            return jnp.dot(p.astype(BF16), v_ref[0, 0:CTX, cols(head)], preferred_element_type=F32) / l
        attend(scores, finish)

    @pl.when(t > 0)
    def _():
        rb = t - 1
        ws = jnp.clip(NA_RB * rb - NA_WIN_ROWS // 2, 0, ROWS - NA_WR)
        window = pl.ds(pl.multiple_of(CTX + ws * GRID_W, GRID_W), NA_NLOC)
        typ = jnp.where(rb == 0, 0, jnp.where(rb == ROWS // NA_RB - 1, 2, 1))

        def scores(head):
            qm = masked_queries(head)
            return (_dot_nt(qm, k_ref[0, window, cols(head)]) + bias_ref[head, typ],
                    _dot_nt(qm, k_ref[0, 0:CTX, cols(head)]))

        def finish(head, s):
            s_loc, s_ctx = s
            m = jnp.maximum(jnp.max(s_loc, axis=-1, keepdims=True), jnp.max(s_ctx, axis=-1, keepdims=True))
            p_loc = jnp.exp(s_loc - m)
            p_ctx = jnp.exp(s_ctx - m)
            l = jnp.sum(p_loc, axis=-1, keepdims=True) + jnp.sum(p_ctx, axis=-1, keepdims=True)
            o = (jnp.dot(p_loc.astype(BF16), v_ref[0, window, cols(head)], preferred_element_type=F32)
                 + jnp.dot(p_ctx.astype(BF16), v_ref[0, 0:CTX, cols(head)], preferred_element_type=F32))
            return o / l
        attend(scores, finish)


def _mixer_a(ya, slabs):
    b = ya.shape[0]
    return pl.pallas_call(
        functools.partial(_na_kernel, _na_slab_index()),
        grid=(b, N_TILES),
        in_specs=[pl.BlockSpec((1, TM, WA), lambda i, t: (i, t, 0)),
                  pl.BlockSpec((1, T_ALL, WA), lambda i, t: (i, 0, 1)),
                  pl.BlockSpec((1, T_ALL, WA), lambda i, t: (i, 0, 2)),
                  pl.BlockSpec((NA_HEADS, 2 * NA_WIN_ROWS, GRID_W, LANES), lambda i, t: (0, 0, 0, 0))],
        out_specs=pl.BlockSpec((1, TM, WA), lambda i, t: (i, t, 0)),
        out_shape=jax.ShapeDtypeStruct((b, T_ALL, WA), BF16),
        scratch_shapes=[pltpu.VMEM((NA_HEADS, 3, NA_RB * GRID_W, NA_NLOC), F32)],
        compiler_params=_params("arbitrary", "arbitrary"),
        name="mixer_a_neighbourhood",
    )(ya, ya, ya, slabs)


def _rope_tables():
    t = np.arange(SEQ)
    pos = np.stack([t // GRID_W, t % GRID_W]).astype(np.float32)
    n_freq = HD // 4
    inv_freq = (ROPE_THETA ** (-np.arange(n_freq, dtype=np.float32) / n_freq)).astype(np.float32)
    ang = pos[:, :, None] * inv_freq
    cos, sin = np.cos(ang), np.sin(ang)
    cos_h = np.concatenate([cos[0], cos[0], cos[1], cos[1]], axis=-1)
    sin_h = np.concatenate([-sin[0], sin[0], -sin[1], sin[1]], axis=-1)
    cos_all = np.concatenate([np.ones((CTX, HD), np.float32), cos_h], axis=0)
    sin_all = np.concatenate([np.zeros((CTX, HD), np.float32), sin_h], axis=0)
    return (np.tile(cos_all, (1, 2)).astype(np.float32), np.tile(sin_all, (1, 2)).astype(np.float32))


def _rope(x, cos, sin):
    quarter = HD // 4
    first = lax.broadcasted_iota(jnp.int32, x.shape, 1) % (2 * quarter) < quarter
    partner = jnp.where(first, pltpu.roll(x, LANES - quarter, axis=1), pltpu.roll(x, quarter, axis=1))
    return x * cos + partner * sin


def _gqa_kernel(q_ref, k_ref, v_ref, cos_ref, sin_ref, qg_ref, kg_ref, o_ref, kn_ref):
    t = pl.program_id(1)
    lo = _lo_mask((1, LANES))

    @pl.when(t == 0)
    def _():
        def prep(i, carry):
            rows = pl.ds(pl.multiple_of(i * TM, TM), TM)
            kx = _head_rms(k_ref[0, rows, :].astype(F32), kg_ref[...])
            kn_ref[rows, :] = _rope(kx, cos_ref[rows, :], sin_ref[rows, :]).astype(BF16)
            return carry
        lax.fori_loop(0, N_TILES, prep, 0)

    rows = pl.ds(pl.multiple_of(t * TM, TM), TM)
    cos = cos_ref[rows, :]
    sin = sin_ref[rows, :]

    def attend(n_keys):
        kn = kn_ref[0:n_keys, :]
        vv = v_ref[0, 0:n_keys, :]

        qn = []
        for j in range(GQA_HEADS // 2):
            qx = _head_rms(q_ref[0, :, j * LANES:(j + 1) * LANES].astype(F32), qg_ref[...])
            qn.append((_rope(qx, cos, sin) * ATT_SCALE).astype(BF16))
        zero = jnp.zeros_like(qn[0])

        def scores(i):
            j, h = divmod(i, 2)
            return _dot_nt(jnp.where(lo, qn[j], zero) if h == 0 else jnp.where(lo, zero, qn[j]), kn)

        ahead = 1
        pending = [scores(i) for i in range(ahead)]
        outs = []
        for i in range(GQA_HEADS):
            s = pending.pop(0)
            if i + ahead < GQA_HEADS:
                pending.append(scores(i + ahead))
            p = jnp.exp(s - jnp.max(s, axis=-1, keepdims=True))
            l = jnp.sum(p, axis=-1, keepdims=True)
            outs.append(jnp.dot(p.astype(BF16), vv, preferred_element_type=F32) / l)
            if i % 2:
                j = i // 2
                o_ref[0, :, j * LANES:(j + 1) * LANES] = jnp.where(lo, outs[i - 1], outs[i]).astype(o_ref.dtype)

    @pl.when(t == 0)
    def _():
        attend(CTX)

    @pl.when(t > 0)
    def _():
        attend(T_ALL)


def _mixer_b(yb, cos, sin, qg, kg):
    b = yb.shape[0]
    kcol = WB // LANES
    full = lambda i, t: (0, 0)
    return pl.pallas_call(
        _gqa_kernel,
        grid=(b, N_TILES),
        in_specs=[pl.BlockSpec((1, TM, WB), lambda i, t: (i, t, 0)),
                  pl.BlockSpec((1, T_ALL, LANES), lambda i, t: (i, 0, kcol)),
                  pl.BlockSpec((1, T_ALL, LANES), lambda i, t: (i, 0, kcol + 1)),
                  pl.BlockSpec((T_ALL, LANES), full),
                  pl.BlockSpec((T_ALL, LANES), full),
                  pl.BlockSpec((1, LANES), full),
                  pl.BlockSpec((1, LANES), full)],
        out_specs=pl.BlockSpec((1, TM, WB), lambda i, t: (i, t, 0)),
        out_shape=jax.ShapeDtypeStruct((b, T_ALL, WB), BF16),
        scratch_shapes=[pltpu.VMEM((T_ALL, LANES), BF16)],
        compiler_params=_params("parallel", "arbitrary"),
        name="mixer_b_gqa",
    )(yb, yb, yb, cos, sin, qg, kg)


def _hgrn_consts():
    c, sc = HG_CHUNK, HG_SUB
    t = np.arange(c)
    out = []
    for reverse in (False, True):
        if not reverse:
            cum = t[None, :] <= t[:, None]
            through = (t[None, :] // sc) <= (t[:, None] // sc)
        else:
            cum = t[None, :] >= t[:, None]
            through = (t[None, :] // sc) >= (t[:, None] // sc)
        out.append(np.concatenate([cum, through], axis=0).astype(np.float32))
    return np.stack(out)


def _forget_gate(z, lb):
    ez = jnp.exp(-jnp.abs(z))
    big = 1.0 / (1.0 + ez)
    small = ez * big
    pos = z >= 0.0
    lb_floor = jnp.maximum(lb, LB_FLOOR)
    f = lb_floor + (1.0 - lb) * jnp.where(pos, big, small)
    one_minus_f = (1.0 - lb) * jnp.where(pos, small, big) - (lb_floor - lb)
    return jnp.log(f), one_minus_f


def _hgrn_kernel(zf_ref, zb_ref, v_ref, q_ref, g_ref, lb_ref, gn_ref, tri_ref, ones_ref,
                 o_ref, acc_ref, st_ref, d_ref, b_ref):
    c, sc = HG_CHUNK, HG_SUB
    n_sub = c // sc
    n_pair = WC // LANES
    acc_ref[...] = jnp.zeros_like(acc_ref)
    st_ref[...] = jnp.zeros_like(st_ref)
    sub_row = lax.broadcasted_iota(jnp.int32, (sc, 1), 0)
    blk = (lax.broadcasted_iota(jnp.int32, (LANES, LANES), 0) // HD
           == lax.broadcasted_iota(jnp.int32, (LANES, LANES), 1) // HD)
    own_head = (lax.broadcasted_iota(jnp.int32, (HG_HEADS * c, WC), 0) // c
                == lax.broadcasted_iota(jnp.int32, (HG_HEADS * c, WC), 1) // HD)
    q_sub = lax.broadcasted_iota(jnp.int32, (c, WC), 0) // sc
    k_sub = (lax.broadcasted_iota(jnp.int32, (c, WC), 1) % c) // sc
    k_lane = lax.broadcasted_iota(jnp.int32, (1, WC), 1) % c

    z_refs = (zf_ref, zb_ref)
    n_ctx = CTX // c
    n_all = T_ALL // c

    def chunk_of(d, n):
        return n if d == 0 else jnp.where(n < n_ctx, n_ctx - 1 - n, n_all + n_ctx - 1 - n)

    def rows_of(d, n):
        return pl.ds(pl.multiple_of(chunk_of(d, n) * c, c), c)

    def gates(d, n, slot):
        g, kk = _forget_gate(z_refs[d][0, rows_of(d, n), :], lb_ref[d:d + 1, :])
        cums = _dot_01(tri_ref[d], g * LOG2E)
        b_ref[d, slot, 0] = cums[0:c]
        b_ref[d, slot, 1] = kk
        b_ref[d, slot, 2] = cums[c:2 * c]

    def state_and_far(d, n, slot):
        reverse = d == 1
        rows = rows_of(d, n)
        b = b_ref[d, slot, 0]
        kk = b_ref[d, slot, 1]
        e = b_ref[d, slot, 2]
        q = q_ref[0, rows, :] * ATT_SCALE
        vb = v_ref[0, rows, :].astype(BF16)
        last = c - 1 if not reverse else 0
        b_last = b_ref[d, slot, 0, last:last + 1, :]

        qe = (q * jnp.exp2(b)).astype(BF16)
        kdec = (kk * jnp.exp2(b_last - b)).astype(BF16)
        o_parts = []
        for p in range(n_pair):
            ln = slice(p * LANES, (p + 1) * LANES)
            st = st_ref[d, p]
            o_parts.append(_dot_nt(qe[:, ln], st.astype(BF16)))
            upd = _dot_tn(vb[:, ln], kdec[:, ln])
            st_ref[d, p] = jnp.where(blk, st * jnp.exp2(b_last[:, ln]) + upd, 0.0)
        o = jnp.concatenate(o_parts, axis=1)

        kx = (kk * jnp.exp2(e - b)).astype(BF16)
        keys = jnp.where(own_head, jnp.concatenate([kx] * HG_HEADS, axis=0), jnp.zeros((), BF16))
        vals = jnp.where(own_head, jnp.concatenate([vb] * HG_HEADS, axis=0), jnp.zeros((), BF16))
        q_stack = []
        for j in range(n_sub - 1):
            jj = j if not reverse else n_sub - 1 - j
            edge = (jj + 1) * sc - 1 if not reverse else jj * sc
            e_j = b_ref[d, slot, 0, edge:edge + 1, :]
            q_stack.append((q * jnp.exp2(jnp.minimum(b - e_j, 0.0))).astype(BF16))
        res = _dot_nt(jnp.concatenate(q_stack, axis=0), keys)
        att = jnp.zeros((c, WC), F32)
        for j in range(n_sub - 1):
            jj = j if not reverse else n_sub - 1 - j
            later = q_sub > jj if not reverse else q_sub < jj
            att = att + jnp.where((k_sub == jj) & later, res[j * c:(j + 1) * c, :], 0.0)
        return dict(rows=rows, b=b, q=q, o=o, att=att, vals=vals)

    def near_products(d, slot, w):
        reverse = d == 1
        for i in range(n_sub):
            bi = w['b'][i * sc:(i + 1) * sc, :]
            qi = w['q'][i * sc:(i + 1) * sc, :]
            for s in range(sc):
                r = i * sc + s
                p = jnp.exp2(bi - b_ref[d, slot, 0, r:r + 1, :]) * (qi * b_ref[d, slot, 1, r:r + 1, :])
                keep = sub_row >= s if not reverse else sub_row <= s
                d_ref[d, r * sc:(r + 1) * sc, :] = jnp.where(keep, p, 0.0).astype(BF16)
        w['sums'] = jnp.dot(d_ref[d], ones_ref[...], preferred_element_type=F32)

    def finish(d, w):
        near = []
        for i in range(n_sub):
            a = jnp.zeros((sc, WC), F32)
            for s in range(sc):
                r = i * sc + s
                a = jnp.where(k_lane == r, w['sums'][r * sc:(r + 1) * sc, :], a)
            near.append(a)
        att = w['att'] + jnp.concatenate(near, axis=0)
        acc_ref[w['rows'], :] += w['o'] + jnp.dot(att.astype(BF16), w['vals'], preferred_element_type=F32)

    for d in range(2):
        gates(d, 0, 0)

    def body(m, carry):
        for slot in range(2):
            n = 2 * m + slot
            for d in range(2):
                gates(d, jnp.minimum(n + 1, n_all - 1), 1 - slot)
            work = [state_and_far(d, n, slot) for d in range(2)]
            for d in range(2):
                near_products(d, slot, work[d])
            for d in range(2):
                finish(d, work[d])
        return carry
    lax.fori_loop(0, n_all // 2, body, 0)

    def readout(i, carry):
        rows = pl.ds(pl.multiple_of(i * TM, TM), TM)
        gate = _silu(g_ref[0, rows, :])
        for p in range(n_pair):
            ln = slice(p * LANES, (p + 1) * LANES)
            y = _head_rms(acc_ref[rows, ln], gn_ref[...]) * gate[:, ln]
            o_ref[0, rows, ln] = y.astype(o_ref.dtype)
        return carry
    lax.fori_loop(0, N_TILES, readout, 0)


def _mixer_c(yc, lb, gn, tri, ones):
    b = yc.shape[0]
    c = HG_CHUNK
    col = lambda j: (lambda i: (i, 0, j))
    return pl.pallas_call(
        _hgrn_kernel,
        grid=(b,),
        in_specs=[pl.BlockSpec((1, T_ALL, WC), col(0)),
                  pl.BlockSpec((1, T_ALL, WC), col(1)),
                  pl.BlockSpec((1, T_ALL, WC), col(2)),
                  pl.BlockSpec((1, T_ALL, WC), col(3)),
                  pl.BlockSpec((1, T_ALL, WC), col(4)),
                  pl.BlockSpec((2, WC), lambda i: (0, 0)),
                  pl.BlockSpec((1, LANES), lambda i: (0, 0)),
                  pl.BlockSpec((2, 2 * c, c), lambda i: (0, 0, 0)),
                  pl.BlockSpec((WC, WC), lambda i: (0, 0))],
        out_specs=pl.BlockSpec((1, T_ALL, WC), lambda i: (i, 0, 0)),
        out_shape=jax.ShapeDtypeStruct((b, T_ALL, WC), BF16),
        scratch_shapes=[pltpu.VMEM((T_ALL, WC), F32),
                        pltpu.VMEM((2, WC // LANES, LANES, LANES), F32),
                        pltpu.VMEM((2, c * HG_SUB, WC), BF16),
                        pltpu.VMEM((2, 2, 3, c, WC), F32)],
        compiler_params=_params("parallel"),
        name="mixer_c_hgrn2",
    )(yc, yc, yc, yc, yc, lb, gn, tri, ones)


def _outproj_ffn_kernel(n_sub, t0, ap_ref, a_ref, an_ref, bp_ref, b_ref, bn_ref, cp_ref, c_ref, cn_ref,
                        xp_ref, x_ref, xn_ref, mod_ref, wout_ref, g1_ref, b1_ref, wup_ref, cw_ref, cb_ref, wdn_ref,
                        g2_ref, b2_ref, o_ref):
    step = pl.program_id(1)
    n_e = HALO + TM + HALO
    n_chunk = D_FF // FF_CHUNK

    def with_halo(prev_ref, ref, next_ref, j, halo):
        lo, hi = j * TM - halo, (j + 1) * TM + halo
        parts = [prev_ref[0]] if j == 0 else []
        parts.append(ref[0, max(lo, 0):min(hi, n_sub * TM), :])
        if j == n_sub - 1:
            parts.append(next_ref[0])
        return parts[0] if len(parts) == 1 else jnp.concatenate(parts, axis=0)

    def prologue(j):
        tile = step * n_sub + j + t0
        mods = mod_ref[0, 1] if j > 0 else jnp.where(tile == 0, mod_ref[0, 0], mod_ref[0, 1])
        mix = jnp.concatenate([with_halo(p, m, n, j, MIX_HALO) for p, m, n in
                               ((ap_ref, a_ref, an_ref), (bp_ref, b_ref, bn_ref), (cp_ref, c_ref, cn_ref))], axis=1)
        y = jnp.dot(mix, wout_ref[...], preferred_element_type=F32)[MIX_HALO - HALO:MIX_HALO + TM + HALO, :]
        xe = with_halo(xp_ref, x_ref, xn_ref, j, HALO)
        x1e = _ln(ALPHA * xe + mods[2:3, :] * y) * g1_ref[...] + b1_ref[...]
        h = _ln(x1e) * (1.0 + mods[4:5, :]) + mods[3:4, :]
        r = lax.broadcasted_iota(jnp.int32, (n_e, 1), 0)
        keep = ((r >= HALO) | (tile >= 2)) & ((r < HALO + TM) | ((tile >= 1) & (tile < N_TILES - 1)))
        return dict(h=jnp.where(keep, h, 0.0).astype(BF16), x1=x1e[HALO:HALO + TM, :], gate=mods[5:6, :])

    def epilogue(j, w):
        z = _ln(ALPHA * w['x1'] + w['gate'] * w['acc'])
        o_ref[0, j * TM:(j + 1) * TM, :] = z * g2_ref[...] + b2_ref[...]

    def up(h, c, base):
        return jnp.dot(h, wup_ref[:, base + c * FF_CHUNK:base + (c + 1) * FF_CHUNK], preferred_element_type=F32)

    def conv(u, c, base):
        cols = slice(base + c * FF_CHUNK, base + (c + 1) * FF_CHUNK)
        u = (pltpu.roll(u, 1, axis=0) * cw_ref[0:1, cols] + u * cw_ref[1:2, cols]
             + pltpu.roll(u, n_e - 1, axis=0) * cw_ref[2:3, cols])
        return u[HALO:HALO + TM, :] + cb_ref[:, cols]

    def hidden_chunks(w, hooks):
        ahead = 2
        acc = jnp.zeros((TM, D), F32)
        pending = [(up(w['h'], c, 0), up(w['h'], c, D_FF)) for c in range(ahead)]
        for c in range(n_chunk):
            u_gate, u_val = pending.pop(0)
            if c + ahead < n_chunk:
                pending.append((up(w['h'], c + ahead, 0), up(w['h'], c + ahead, D_FF)))
            if c in hooks:
                hooks[c]()
            a = (_silu(conv(u_gate, c, 0)) * conv(u_val, c, D_FF)).astype(BF16)
            acc = acc + jnp.dot(a, wdn_ref[c * FF_CHUNK:(c + 1) * FF_CHUNK, :], preferred_element_type=F32)
        w['acc'] = acc

    work = [None] * n_sub
    work[0] = prologue(0)
    for j in range(n_sub):
        hooks = {}
        if j > 0:
            hooks[0] = functools.partial(epilogue, j - 1, work[j - 1])
        if j + 1 < n_sub:
            hooks[n_chunk // 2] = functools.partial(lambda k: work.__setitem__(k, prologue(k)), j + 1)
        hidden_chunks(work[j], hooks)
    epilogue(n_sub - 1, work[n_sub - 1])


def _outproj_ffn(oa, ob, oc, xs, modsel, w_out_p, ln1_g, ln1_b, w_up, conv_w, conv_b, w_down, ln2_g, ln2_b, t0):
    b = xs.shape[0]
    n_sub = FFN_TILES_PER_STEP if t0 == 0 else 1
    assert (N_TILES - t0) % n_sub == 0
    rows = n_sub * TM
    full = lambda i, s: (0, 0)

    def halos(width, halo):
        per, last = rows // halo, T_ALL // halo - 1
        first = t0 * TM // halo
        return (pl.BlockSpec((1, halo, width), lambda i, s: (i, jnp.maximum(first + s * per - 1, first), 0)),
                pl.BlockSpec((1, rows, width), lambda i, s: (i, s + t0, 0)),
                pl.BlockSpec((1, halo, width), lambda i, s: (i, jnp.minimum(first + (s + 1) * per, last), 0)))

    return pl.pallas_call(
        functools.partial(_outproj_ffn_kernel, n_sub, t0),
        grid=(b, (N_TILES - t0) // n_sub),
        in_specs=[*halos(WA, MIX_HALO), *halos(WB, MIX_HALO), *halos(WC, MIX_HALO), *halos(D, HALO),
                  pl.BlockSpec((1, 2, 8, D), lambda i, s: (i, 0, 0, 0)),
                  pl.BlockSpec((D, D), full),
                  pl.BlockSpec((1, D), full),
                  pl.BlockSpec((1, D), full),
                  pl.BlockSpec((D, 2 * D_FF), full),
                  pl.BlockSpec((3, 2 * D_FF), full),
                  pl.BlockSpec((1, 2 * D_FF), full),
                  pl.BlockSpec((D_FF, D), full),
                  pl.BlockSpec((1, D), full),
                  pl.BlockSpec((1, D), full)],
        out_specs=pl.BlockSpec((1, rows, D), lambda i, s: (i, s, 0)),
        out_shape=jax.ShapeDtypeStruct((b, T_ALL - t0 * TM, D), F32),
        compiler_params=_params("parallel", "arbitrary"),
        name="outproj_ffn",
    )(oa, oa, oa, ob, ob, ob, oc, oc, oc, xs, xs, xs, modsel, w_out_p, ln1_g, ln1_b,
      w_up, conv_w, conv_b, w_down, ln2_g, ln2_b)


def _in_col_perm():
    off, o = {}, 0
    for name, width in (('a_k', WA), ('a_v', WA), ('b_k', WKV), ('b_v', WKV), ('c_f', WC), ('c_b', WC),
                        ('c_i', WC), ('a_q', WA), ('b_q', WB), ('c_q', WC), ('c_g', WC)):
        off[name] = np.arange(o, o + width)
        o += width
    bq = off['b_q'].reshape(GQA_HEADS, HD)[list(GQA_ORDER)].reshape(-1)
    return np.concatenate([off['a_q'], off['a_k'], off['a_v'], bq, off['b_k'], off['b_v'],
                           off['c_f'], off['c_b'], off['c_i'], off['c_q'], off['c_g']])


def _out_row_perm():
    ob = (WA + np.arange(WB)).reshape(GQA_HEADS, HD)[list(GQA_ORDER)].reshape(-1)
    return np.concatenate([np.arange(WA), ob, WA + WB + np.arange(WC)])


def _take_runs(w, perm, axis):
    cuts = [0] + [i for i in range(1, len(perm)) if perm[i] != perm[i - 1] + 1] + [len(perm)]
    parts = [lax.slice_in_dim(w, int(perm[a]), int(perm[z - 1]) + 1, axis=axis) for a, z in zip(cuts[:-1], cuts[1:])]
    return jnp.concatenate(parts, axis=axis)


def kernel(x, c, ctx, c_ctx, w_ada, b_ada, w_in, w_out, na_rpb, q_norm, k_norm, hg_lower_bound, hg_norm,
           ln1_g, ln1_b, w_up, conv_w, conv_b, w_down, ln2_g, ln2_b):
    b = x.shape[0]
    assert x.shape == (b, SEQ, D) and ctx.shape == (b, CTX, D)
    n_rows = -(-(b + 1) // 8) * 8
    s_in = jnp.concatenate([c, c_ctx[None, :], jnp.zeros((n_rows - b - 1, D), F32)], axis=0)
    mods = _modulation(s_in, w_ada, b_ada).reshape(DEPTH, n_rows, N_MOD, D)
    lbs = _lower_bounds(hg_lower_bound).reshape(DEPTH, 2, WC)

    cos, sin = (jnp.asarray(a) for a in _rope_tables())
    tri = jnp.asarray(_hgrn_consts(), BF16)
    ones = jnp.asarray(np.kron(np.eye(HG_HEADS), np.ones((HD, HD))), BF16)
    in_perm = _in_col_perm()
    out_perm = _out_row_perm()
    tile2 = lambda g: jnp.tile(g.astype(F32), 2).reshape(1, LANES)

    xs = jnp.concatenate([ctx, x], axis=1)
    for l in range(DEPTH):
        last = l == DEPTH - 1
        t0 = 1 if last else 0
        lat = mods[l, :b]
        cx = jnp.broadcast_to(mods[l, b][None], lat.shape)
        modsel = jnp.pad(jnp.stack([cx, lat], axis=1), ((0, 0), (0, 0), (0, 8 - N_MOD), (0, 0)))
        ya, yb, yc = _in_projection(xs, modsel, _take_runs(w_in[l], in_perm, 1).astype(BF16))
        oa = _mixer_a(ya, _na_slabs(na_rpb[l]))
        ob = _mixer_b(yb, cos, sin, tile2(q_norm[l]), tile2(k_norm[l]))
        oc = _mixer_c(yc, lbs[l], tile2(hg_norm[l]), tri, ones)
        xs = _outproj_ffn(oa, ob, oc, xs, modsel, _take_runs(w_out[l], out_perm, 0).astype(BF16),
                          ln1_g[l].reshape(1, D), ln1_b[l].reshape(1, D), w_up[l].astype(BF16), conv_w[l],
                          conv_b[l].reshape(1, 2 * D_FF), w_down[l].astype(BF16),
                          ln2_g[l].reshape(1, D), ln2_b[l].reshape(1, D), t0)
    return xs
```

```python
import functools

import numpy as np
import jax
import jax.numpy as jnp
from jax import lax
from jax.experimental import pallas as pl
from jax.experimental.pallas import tpu as pltpu

D = 1024
DEPTH = 4
GRID_W = 64
CTX = 256
SEQ = 2048
ROWS = SEQ // GRID_W
T_ALL = CTX + SEQ
HD = 64
NA_HEADS = 6
NA_WIN_ROWS = 8
NA_WIN_COLS = 16
GQA_HEADS = 6
GQA_KV = 2
HG_HEADS = 4
HG_CHUNK = 64
HG_SUB = 16
ROPE_THETA = 10000.0
LB_FLOOR = 1e-30
WA = NA_HEADS * HD
WB = GQA_HEADS * HD
WKV = GQA_KV * HD
WC = HG_HEADS * HD
IN_COLS = 3072
D_FF = 2816
N_MOD = 6
ALPHA = (2.0 * DEPTH) ** 0.25
LN_EPS = 1e-6
RMS_EPS = 1e-6
ATT_SCALE = HD ** -0.5
LOG2E = 1.4426950408889634
NEG = -1e30

LANES = 128
TM = 256
N_TILES = T_ALL // TM
NA_RB = 4
NA_WR = NA_WIN_ROWS + NA_RB - 1
NA_NLOC = NA_WR * GRID_W
FF_CHUNK = 256
FFN_TILES_PER_STEP = 3
IN_TILES = 3
HALO = 8
MIX_HALO = 16
VMEM_LIMIT = 56 * 1024 * 1024

YA_COLS = 3 * WA
YB_COLS = WB + 2 * WKV
YC_COLS = 5 * WC
GQA_ORDER = (0, 3, 1, 4, 2, 5)

HI = lax.Precision.HIGHEST
F32 = jnp.float32
BF16 = jnp.bfloat16


def _params(*sem):
    return pltpu.CompilerParams(dimension_semantics=sem, vmem_limit_bytes=VMEM_LIMIT)


def _dot_nt(a, b):
    return lax.dot_general(a, b, (((1,), (1,)), ((), ())), preferred_element_type=F32)


def _dot_tn(a, b):
    return lax.dot_general(a, b, (((0,), (0,)), ((), ())), preferred_element_type=F32)


def _ln(x):
    mu = jnp.mean(x, axis=-1, keepdims=True)
    xc = x - mu
    var = jnp.mean(xc * xc, axis=-1, keepdims=True)
    return xc * lax.rsqrt(var + LN_EPS)


def _dot_01(m, x):
    hi = x.astype(BF16)
    rest = x - hi.astype(F32)
    mid = rest.astype(BF16)
    lo = (rest - mid.astype(F32)).astype(BF16)
    return (jnp.dot(m, hi, preferred_element_type=F32) + jnp.dot(m, mid, preferred_element_type=F32)
            + jnp.dot(m, lo, preferred_element_type=F32))


def _silu(x):
    return x * jax.nn.sigmoid(x)


def _lo_mask(shape):
    return lax.broadcasted_iota(jnp.int32, shape, len(shape) - 1) % LANES < HD


def _head_rms(x, gain):
    lo = _lo_mask(x.shape)
    x2 = x * x
    s_lo = jnp.sum(jnp.where(lo, x2, 0.0), axis=-1, keepdims=True)
    s_hi = jnp.sum(jnp.where(lo, 0.0, x2), axis=-1, keepdims=True)
    ms = jnp.where(lo, s_lo, s_hi) * (1.0 / HD)
    return x * lax.rsqrt(ms + RMS_EPS) * gain


def _mod_kernel(s_ref, w_ref, b_ref, o_ref):
    s = _silu(s_ref[...])
    o_ref[0] = jnp.dot(s, w_ref[0], precision=HI, preferred_element_type=F32) + b_ref[0]


def _modulation(s_in, w_ada, b_ada):
    r = s_in.shape[0]
    cb = 1536
    return pl.pallas_call(
        _mod_kernel,
        grid=(DEPTH, N_MOD * D // cb),
        in_specs=[pl.BlockSpec((r, D), lambda l, j: (0, 0)),
                  pl.BlockSpec((1, D, cb), lambda l, j: (l, 0, j)),
                  pl.BlockSpec((1, 1, cb), lambda l, j: (l, 0, j))],
        out_specs=pl.BlockSpec((1, r, cb), lambda l, j: (l, 0, j)),
        out_shape=jax.ShapeDtypeStruct((DEPTH, r, N_MOD * D), F32),
        compiler_params=_params("arbitrary", "arbitrary"),
        name="adaln_mod",
    )(s_in, w_ada, b_ada.reshape(DEPTH, 1, N_MOD * D))


def _lb_kernel(p_ref, o_ref):
    p = p_ref[...]
    e = jnp.exp(p - jnp.max(p, axis=0, keepdims=True))
    sm = e / jnp.sum(e, axis=0, keepdims=True)
    acc = sm[0:1]
    rows = [acc - sm[0:1]]
    for l in range(1, DEPTH):
        acc = acc + sm[l:l + 1]
        rows.append(acc - sm[0:1])
    o_ref[...] = jnp.concatenate(rows, axis=0)


def _lower_bounds(hg_lower_bound):
    p = hg_lower_bound.reshape(DEPTH, 2 * WC).astype(F32)
    return pl.pallas_call(
        _lb_kernel,
        out_shape=jax.ShapeDtypeStruct((DEPTH, 2 * WC), F32),
        name="hgrn_lower_bounds",
    )(p)


def _inproj_kernel(x_ref, mod_ref, w_ref, ya_ref, yb_ref, yc_ref):
    step = pl.program_id(1)
    hs = []
    for j in range(IN_TILES):
        mods = mod_ref[0, 1] if j > 0 else jnp.where(step == 0, mod_ref[0, 0], mod_ref[0, 1])
        y = _ln(x_ref[0, j * TM:(j + 1) * TM, :])
        hs.append((y * (1.0 + mods[1:2, :]) + mods[0:1, :]).astype(BF16))
    for j, h in enumerate(hs):
        rows = slice(j * TM, (j + 1) * TM)
        out = jnp.dot(h, w_ref[...], preferred_element_type=F32)
        ya_ref[0, rows, :] = out[:, :YA_COLS].astype(BF16)
        yb_ref[0, rows, :] = out[:, YA_COLS:YA_COLS + YB_COLS].astype(BF16)
        yc_ref[0, rows, :] = out[:, YA_COLS + YB_COLS:]


def _in_projection(xs, modsel, w_in_p):
    b = xs.shape[0]
    row = lambda i, t: (i, t, 0)
    return pl.pallas_call(
        _inproj_kernel,
        grid=(b, N_TILES // IN_TILES),
        in_specs=[pl.BlockSpec((1, IN_TILES * TM, D), row),
                  pl.BlockSpec((1, 2, 8, D), lambda i, t: (i, 0, 0, 0)),
                  pl.BlockSpec((D, IN_COLS), lambda i, t: (0, 0))],
        out_specs=[pl.BlockSpec((1, IN_TILES * TM, YA_COLS), row),
                   pl.BlockSpec((1, IN_TILES * TM, YB_COLS), row),
                   pl.BlockSpec((1, IN_TILES * TM, YC_COLS), row)],
        out_shape=[jax.ShapeDtypeStruct((b, T_ALL, YA_COLS), BF16),
                   jax.ShapeDtypeStruct((b, T_ALL, YB_COLS), BF16),
                   jax.ShapeDtypeStruct((b, T_ALL, YC_COLS), F32)],
        compiler_params=_params("parallel", "arbitrary"),
        name="in_projection",
    )(xs, modsel, w_in_p)


def _na_slab_index():
    def table(rb):
        r0 = NA_RB * rb
        ws = int(np.clip(r0 - NA_WIN_ROWS // 2, 0, ROWS - NA_WR))
        qr = r0 + np.arange(NA_RB)[:, None]
        kr = ws + np.arange(NA_WR)[None, :]
        rs = np.clip(qr - NA_WIN_ROWS // 2, 0, ROWS - NA_WIN_ROWS)
        valid = (kr >= rs) & (kr < rs + NA_WIN_ROWS)
        return np.where(valid, kr - qr + NA_WIN_ROWS - 1, 2 * NA_WIN_ROWS - 1)

    n_rb = ROWS // NA_RB
    for rb in range(2, n_rb - 1):
        assert np.array_equal(table(rb), table(1))
    return np.stack([table(0), table(1), table(n_rb - 1)])


def _na_slabs(rpb):
    qc = np.arange(GRID_W)[:, None]
    kc = np.arange(GRID_W)[None, :]
    cs = np.clip(qc - NA_WIN_COLS // 2, 0, GRID_W - NA_WIN_COLS)
    valid = (kc >= cs) & (kc < cs + NA_WIN_COLS)
    onehot = (valid[None] & ((kc - qc + NA_WIN_COLS - 1)[None] == np.arange(2 * NA_WIN_COLS - 1)[:, None, None]))
    onehot = jnp.asarray(onehot.reshape(2 * NA_WIN_COLS - 1, GRID_W * GRID_W), F32)
    slabs = jnp.einsum('hrd,dx->hrx', rpb.astype(F32), onehot, precision=HI)
    slabs = jnp.where(jnp.asarray(valid.reshape(-1)), slabs, NEG)
    slabs = jnp.concatenate([slabs, jnp.full((NA_HEADS, 1, GRID_W * GRID_W), NEG, F32)], axis=1)
    slabs = slabs.reshape(NA_HEADS, 2 * NA_WIN_ROWS, GRID_W, GRID_W)
    return jnp.concatenate([slabs, slabs], axis=-1)


def _na_kernel(idx, q_ref, k_ref, v_ref, slab_ref, o_ref, bias_ref):
    t = pl.program_id(1)
    n_blk = NA_HEADS // 2
    lo = _lo_mask((1, LANES))

    @pl.when((pl.program_id(0) == 0) & (t == 0))
    def _():
        for head in range(NA_HEADS):
            for typ in range(3):
                for rl in range(NA_RB):
                    rows = slice(rl * GRID_W, (rl + 1) * GRID_W)
                    for m in range(NA_WR // 2):
                        even = slab_ref[head, int(idx[typ, rl, 2 * m])]
                        odd = slab_ref[head, int(idx[typ, rl, 2 * m + 1])]
                        bias_ref[head, typ, rows, m * LANES:(m + 1) * LANES] = jnp.where(lo, even, odd)
                    if NA_WR % 2:
                        tail = slab_ref[head, int(idx[typ, rl, NA_WR - 1])]
                        bias_ref[head, typ, rows, (NA_WR - 1) * GRID_W:NA_NLOC] = tail[:, 0:GRID_W]

    def masked_queries(head):
        blk, half = divmod(head, 2)
        qs = q_ref[0, :, blk * LANES:(blk + 1) * LANES] * ATT_SCALE
        zero = jnp.zeros_like(qs)
        return jnp.where(lo, qs, zero) if half == 0 else jnp.where(lo, zero, qs)

    def cols(head):
        blk = head // 2
        return slice(blk * LANES, (blk + 1) * LANES)

    def attend(scores, finish):
        ahead = 2
        pending = [scores(h) for h in range(ahead)]
        outs = []
        for head in range(NA_HEADS):
            s = pending.pop(0)
            if head + ahead < NA_HEADS:
                pending.append(scores(head + ahead))
            outs.append(finish(head, s))
            if head % 2:
                o_ref[0, :, cols(head)] = jnp.where(lo, outs[head - 1], outs[head]).astype(o_ref.dtype)

    @pl.when(t == 0)
    def _():
        def scores(head):
            return _dot_nt(masked_queries(head), k_ref[0, 0:CTX, cols(head)])

        def finish(head, s):
            p = jnp.exp(s - jnp.max(s, axis=-1, keepdims=True))
            l = jnp.sum(p, axis=-1, keepdims=True)
            return jnp.dot(p.astype(BF16), v_ref[0, 0:CTX, cols(head)], preferred_element_type=F32) / l
        attend(scores, finish)

    @pl.when(t > 0)
    def _():
        rb = t - 1
        ws = jnp.clip(NA_RB * rb - NA_WIN_ROWS // 2, 0, ROWS - NA_WR)
        window = pl.ds(pl.multiple_of(CTX + ws * GRID_W, GRID_W), NA_NLOC)
        typ = jnp.where(rb == 0, 0, jnp.where(rb == ROWS // NA_RB - 1, 2, 1))

        def scores(head):
            qm = masked_queries(head)
            return (_dot_nt(qm, k_ref[0, window, cols(head)]) + bias_ref[head, typ],
                    _dot_nt(qm, k_ref[0, 0:CTX, cols(head)]))

        def finish(head, s):
            s_loc, s_ctx = s
            m = jnp.maximum(jnp.max(s_loc, axis=-1, keepdims=True), jnp.max(s_ctx, axis=-1, keepdims=True))
            p_loc = jnp.exp(s_loc - m)
            p_ctx = jnp.exp(s_ctx - m)
            l = jnp.sum(p_loc, axis=-1, keepdims=True) + jnp.sum(p_ctx, axis=-1, keepdims=True)
            o = (jnp.dot(p_loc.astype(BF16), v_ref[0, window, cols(head)], preferred_element_type=F32)
                 + jnp.dot(p_ctx.astype(BF16), v_ref[0, 0:CTX, cols(head)], preferred_element_type=F32))
            return o / l
        attend(scores, finish)


def _rope_tables():
    t = np.arange(SEQ)
    pos = np.stack([t // GRID_W, t % GRID_W]).astype(np.float32)
    n_freq = HD // 4
    inv_freq = (ROPE_THETA ** (-np.arange(n_freq, dtype=np.float32) / n_freq)).astype(np.float32)
    ang = pos[:, :, None] * inv_freq
    cos, sin = np.cos(ang), np.sin(ang)
    cos_h = np.concatenate([cos[0], cos[0], cos[1], cos[1]], axis=-1)
    sin_h = np.concatenate([-sin[0], sin[0], -sin[1], sin[1]], axis=-1)
    cos_all = np.concatenate([np.ones((CTX, HD), np.float32), cos_h], axis=0)
    sin_all = np.concatenate([np.zeros((CTX, HD), np.float32), sin_h], axis=0)
    return (np.tile(cos_all, (1, 2)).astype(np.float32), np.tile(sin_all, (1, 2)).astype(np.float32))


def _rope(x, cos, sin):
    quarter = HD // 4
    first = lax.broadcasted_iota(jnp.int32, x.shape, 1) % (2 * quarter) < quarter
    partner = jnp.where(first, pltpu.roll(x, LANES - quarter, axis=1), pltpu.roll(x, quarter, axis=1))
    return x * cos + partner * sin


def _gqa_kernel(q_ref, k_ref, v_ref, cos_ref, sin_ref, qg_ref, kg_ref, o_ref, kn_ref):
    t = pl.program_id(1)
    lo = _lo_mask((1, LANES))

    @pl.when(t == 0)
    def _():
        def prep(i, carry):
            rows = pl.ds(pl.multiple_of(i * TM, TM), TM)
            kx = _head_rms(k_ref[0, rows, :].astype(F32), kg_ref[...])
            kn_ref[rows, :] = _rope(kx, cos_ref[rows, :], sin_ref[rows, :]).astype(BF16)
            return carry
        lax.fori_loop(0, N_TILES, prep, 0)

    rows = pl.ds(pl.multiple_of(t * TM, TM), TM)
    cos = cos_ref[rows, :]
    sin = sin_ref[rows, :]

    def attend(n_keys):
        kn = kn_ref[0:n_keys, :]
        vv = v_ref[0, 0:n_keys, :]

        qn = []
        for j in range(GQA_HEADS // 2):
            qx = _head_rms(q_ref[0, :, j * LANES:(j + 1) * LANES].astype(F32), qg_ref[...])
            qn.append((_rope(qx, cos, sin) * ATT_SCALE).astype(BF16))
        zero = jnp.zeros_like(qn[0])

        def scores(i):
            j, h = divmod(i, 2)
            return _dot_nt(jnp.where(lo, qn[j], zero) if h == 0 else jnp.where(lo, zero, qn[j]), kn)

        ahead = 1
        pending = [scores(i) for i in range(ahead)]
        outs = []
        for i in range(GQA_HEADS):
            s = pending.pop(0)
            if i + ahead < GQA_HEADS:
                pending.append(scores(i + ahead))
            p = jnp.exp(s - jnp.max(s, axis=-1, keepdims=True))
            l = jnp.sum(p, axis=-1, keepdims=True)
            outs.append(jnp.dot(p.astype(BF16), vv, preferred_element_type=F32) / l)
            if i % 2:
                j = i // 2
                o_ref[0, :, j * LANES:(j + 1) * LANES] = jnp.where(lo, outs[i - 1], outs[i]).astype(o_ref.dtype)

    @pl.when(t == 0)
    def _():
        attend(CTX)

    @pl.when(t > 0)
    def _():
        attend(T_ALL)


def _attention_kernel(idx, qa_ref, ka_ref, va_ref, slab_ref, qb_ref, kb_ref, vb_ref, cos_ref, sin_ref, qg_ref, kg_ref,
                      oa_ref, ob_ref, bias_ref, kn_ref):
    _na_kernel(idx, qa_ref, ka_ref, va_ref, slab_ref, oa_ref, bias_ref)
    _gqa_kernel(qb_ref, kb_ref, vb_ref, cos_ref, sin_ref, qg_ref, kg_ref, ob_ref, kn_ref)


def _mixers_ab(ya, yb, slabs, cos, sin, qg, kg):
    b = ya.shape[0]
    kcol = WB // LANES
    tile = lambda i, t: (i, t, 0)
    full = lambda i, t: (0, 0)
    return pl.pallas_call(
        functools.partial(_attention_kernel, _na_slab_index()),
        grid=(b, N_TILES),
        in_specs=[pl.BlockSpec((1, TM, WA), tile),
                  pl.BlockSpec((1, T_ALL, WA), lambda i, t: (i, 0, 1)),
                  pl.BlockSpec((1, T_ALL, WA), lambda i, t: (i, 0, 2)),
                  pl.BlockSpec((NA_HEADS, 2 * NA_WIN_ROWS, GRID_W, LANES), lambda i, t: (0, 0, 0, 0)),
                  pl.BlockSpec((1, TM, WB), tile),
                  pl.BlockSpec((1, T_ALL, LANES), lambda i, t: (i, 0, kcol)),
                  pl.BlockSpec((1, T_ALL, LANES), lambda i, t: (i, 0, kcol + 1)),
                  pl.BlockSpec((T_ALL, LANES), full),
                  pl.BlockSpec((T_ALL, LANES), full),
                  pl.BlockSpec((1, LANES), full),
                  pl.BlockSpec((1, LANES), full)],
        out_specs=[pl.BlockSpec((1, TM, WA), tile),
                   pl.BlockSpec((1, TM, WB), tile)],
        out_shape=[jax.ShapeDtypeStruct((b, T_ALL, WA), BF16),
                   jax.ShapeDtypeStruct((b, T_ALL, WB), BF16)],
        scratch_shapes=[pltpu.VMEM((NA_HEADS, 3, NA_RB * GRID_W, NA_NLOC), F32),
                        pltpu.VMEM((T_ALL, LANES), BF16)],
        compiler_params=_params("arbitrary", "arbitrary"),
        name="mixers_ab_attention",
    )(ya, ya, ya, slabs, yb, yb, yb, cos, sin, qg, kg)


def _hgrn_consts():
    c, sc = HG_CHUNK, HG_SUB
    t = np.arange(c)
    out = []
    for reverse in (False, True):
        if not reverse:
            cum = t[None, :] <= t[:, None]
            through = (t[None, :] // sc) <= (t[:, None] // sc)
        else:
            cum = t[None, :] >= t[:, None]
            through = (t[None, :] // sc) >= (t[:, None] // sc)
        out.append(np.concatenate([cum, through], axis=0).astype(np.float32))
    return np.stack(out)


def _forget_gate(z, lb):
    ez = jnp.exp(-jnp.abs(z))
    big = 1.0 / (1.0 + ez)
    small = ez * big
    pos = z >= 0.0
    lb_floor = jnp.maximum(lb, LB_FLOOR)
    f = lb_floor + (1.0 - lb) * jnp.where(pos, big, small)
    one_minus_f = (1.0 - lb) * jnp.where(pos, small, big) - (lb_floor - lb)
    return jnp.log(f), one_minus_f


def _hgrn_kernel(zf_ref, zb_ref, v_ref, q_ref, g_ref, lb_ref, gn_ref, tri_ref, ones_ref,
                 o_ref, acc_ref, st_ref, d_ref, b_ref):
    c, sc = HG_CHUNK, HG_SUB
    n_sub = c // sc
    n_pair = WC // LANES
    acc_ref[...] = jnp.zeros_like(acc_ref)
    st_ref[...] = jnp.zeros_like(st_ref)
    sub_row = lax.broadcasted_iota(jnp.int32, (sc, 1), 0)
    blk = (lax.broadcasted_iota(jnp.int32, (LANES, LANES), 0) // HD
           == lax.broadcasted_iota(jnp.int32, (LANES, LANES), 1) // HD)
    own_head = (lax.broadcasted_iota(jnp.int32, (HG_HEADS * c, WC), 0) // c
                == lax.broadcasted_iota(jnp.int32, (HG_HEADS * c, WC), 1) // HD)
    q_sub = lax.broadcasted_iota(jnp.int32, (c, WC), 0) // sc
    k_sub = (lax.broadcasted_iota(jnp.int32, (c, WC), 1) % c) // sc
    k_lane = lax.broadcasted_iota(jnp.int32, (1, WC), 1) % c

    z_refs = (zf_ref, zb_ref)
    n_ctx = CTX // c
    n_all = T_ALL // c

    def chunk_of(d, n):
        return n if d == 0 else jnp.where(n < n_ctx, n_ctx - 1 - n, n_all + n_ctx - 1 - n)

    def rows_of(d, n):
        return pl.ds(pl.multiple_of(chunk_of(d, n) * c, c), c)

    def gates(d, n, slot):
        g, kk = _forget_gate(z_refs[d][0, rows_of(d, n), :], lb_ref[d:d + 1, :])
        cums = _dot_01(tri_ref[d], g * LOG2E)
        b_ref[d, slot, 0] = cums[0:c]
        b_ref[d, slot, 1] = kk
        b_ref[d, slot, 2] = cums[c:2 * c]

    def state_and_far(d, n, slot):
        reverse = d == 1
        rows = rows_of(d, n)
        b = b_ref[d, slot, 0]
        kk = b_ref[d, slot, 1]
        e = b_ref[d, slot, 2]
        q = q_ref[0, rows, :] * ATT_SCALE
        vb = v_ref[0, rows, :].astype(BF16)
        last = c - 1 if not reverse else 0
        b_last = b_ref[d, slot, 0, last:last + 1, :]

        qe = (q * jnp.exp2(b)).astype(BF16)
        kdec = (kk * jnp.exp2(b_last - b)).astype(BF16)
        o_parts = []
        for p in range(n_pair):
            ln = slice(p * LANES, (p + 1) * LANES)
            st = st_ref[d, p]
            o_parts.append(_dot_nt(qe[:, ln], st.astype(BF16)))
            upd = _dot_tn(vb[:, ln], kdec[:, ln])
            st_ref[d, p] = jnp.where(blk, st * jnp.exp2(b_last[:, ln]) + upd, 0.0)
        o = jnp.concatenate(o_parts, axis=1)

        kx = (kk * jnp.exp2(e - b)).astype(BF16)
        keys = jnp.where(own_head, jnp.concatenate([kx] * HG_HEADS, axis=0), jnp.zeros((), BF16))
        vals = jnp.where(own_head, jnp.concatenate([vb] * HG_HEADS, axis=0), jnp.zeros((), BF16))
        q_stack = []
        for j in range(n_sub - 1):
            jj = j if not reverse else n_sub - 1 - j
            edge = (jj + 1) * sc - 1 if not reverse else jj * sc
            e_j = b_ref[d, slot, 0, edge:edge + 1, :]
            q_stack.append((q * jnp.exp2(jnp.minimum(b - e_j, 0.0))).astype(BF16))
        res = _dot_nt(jnp.concatenate(q_stack, axis=0), keys)
        att = jnp.zeros((c, WC), F32)
        for j in range(n_sub - 1):
            jj = j if not reverse else n_sub - 1 - j
            later = q_sub > jj if not reverse else q_sub < jj
            att = att + jnp.where((k_sub == jj) & later, res[j * c:(j + 1) * c, :], 0.0)
        return dict(rows=rows, b=b, q=q, o=o, att=att, vals=vals)

    def near_products(d, slot, w):
        reverse = d == 1
        for i in range(n_sub):
            bi = w['b'][i * sc:(i + 1) * sc, :]
            qi = w['q'][i * sc:(i + 1) * sc, :]
            for s in range(sc):
                r = i * sc + s
                p = jnp.exp2(bi - b_ref[d, slot, 0, r:r + 1, :]) * (qi * b_ref[d, slot, 1, r:r + 1, :])
                keep = sub_row >= s if not reverse else sub_row <= s
                d_ref[d, r * sc:(r + 1) * sc, :] = jnp.where(keep, p, 0.0).astype(BF16)
        w['sums'] = jnp.dot(d_ref[d], ones_ref[...], preferred_element_type=F32)

    def finish(d, w):
        near = []
        for i in range(n_sub):
            a = jnp.zeros((sc, WC), F32)
            for s in range(sc):
                r = i * sc + s
                a = jnp.where(k_lane == r, w['sums'][r * sc:(r + 1) * sc, :], a)
            near.append(a)
        att = w['att'] + jnp.concatenate(near, axis=0)
        acc_ref[w['rows'], :] += w['o'] + jnp.dot(att.astype(BF16), w['vals'], preferred_element_type=F32)

    for d in range(2):
        gates(d, 0, 0)

    def body(m, carry):
        for slot in range(2):
            n = 2 * m + slot
            for d in range(2):
                gates(d, jnp.minimum(n + 1, n_all - 1), 1 - slot)
            work = [state_and_far(d, n, slot) for d in range(2)]
            for d in range(2):
                near_products(d, slot, work[d])
            for d in range(2):
                finish(d, work[d])
        return carry
    lax.fori_loop(0, n_all // 2, body, 0)

    def readout(i, carry):
        rows = pl.ds(pl.multiple_of(i * TM, TM), TM)
        gate = _silu(g_ref[0, rows, :])
        for p in range(n_pair):
            ln = slice(p * LANES, (p + 1) * LANES)
            y = _head_rms(acc_ref[rows, ln], gn_ref[...]) * gate[:, ln]
            o_ref[0, rows, ln] = y.astype(o_ref.dtype)
        return carry
    lax.fori_loop(0, N_TILES, readout, 0)


def _mixer_c(yc, lb, gn, tri, ones):
    b = yc.shape[0]
    c = HG_CHUNK
    col = lambda j: (lambda i: (i, 0, j))
    return pl.pallas_call(
        _hgrn_kernel,
        grid=(b,),
        in_specs=[pl.BlockSpec((1, T_ALL, WC), col(0)),
                  pl.BlockSpec((1, T_ALL, WC), col(1)),
                  pl.BlockSpec((1, T_ALL, WC), col(2)),
                  pl.BlockSpec((1, T_ALL, WC), col(3)),
                  pl.BlockSpec((1, T_ALL, WC), col(4)),
                  pl.BlockSpec((2, WC), lambda i: (0, 0)),
                  pl.BlockSpec((1, LANES), lambda i: (0, 0)),
                  pl.BlockSpec((2, 2 * c, c), lambda i: (0, 0, 0)),
                  pl.BlockSpec((WC, WC), lambda i: (0, 0))],
        out_specs=pl.BlockSpec((1, T_ALL, WC), lambda i: (i, 0, 0)),
        out_shape=jax.ShapeDtypeStruct((b, T_ALL, WC), BF16),
        scratch_shapes=[pltpu.VMEM((T_ALL, WC), F32),
                        pltpu.VMEM((2, WC // LANES, LANES, LANES), F32),
                        pltpu.VMEM((2, c * HG_SUB, WC), BF16),
                        pltpu.VMEM((2, 2, 3, c, WC), F32)],
        compiler_params=_params("parallel"),
        name="mixer_c_hgrn2",
    )(yc, yc, yc, yc, yc, lb, gn, tri, ones)


def _outproj_ffn_kernel(n_sub, t0, ap_ref, a_ref, an_ref, bp_ref, b_ref, bn_ref, cp_ref, c_ref, cn_ref,
                        xp_ref, x_ref, xn_ref, mod_ref, wout_ref, g1_ref, b1_ref, wup_ref, cw_ref, cb_ref, wdn_ref,
                        g2_ref, b2_ref, o_ref):
    step = pl.program_id(1)
    n_e = HALO + TM + HALO
    n_chunk = D_FF // FF_CHUNK

    def with_halo(prev_ref, ref, next_ref, j, halo):
        lo, hi = j * TM - halo, (j + 1) * TM + halo
        parts = [prev_ref[0]] if j == 0 else []
        parts.append(ref[0, max(lo, 0):min(hi, n_sub * TM), :])
        if j == n_sub - 1:
            parts.append(next_ref[0])
        return parts[0] if len(parts) == 1 else jnp.concatenate(parts, axis=0)

    def prologue(j):
        tile = step * n_sub + j + t0
        mods = mod_ref[0, 1] if j > 0 else jnp.where(tile == 0, mod_ref[0, 0], mod_ref[0, 1])
        mix = jnp.concatenate([with_halo(p, m, n, j, MIX_HALO) for p, m, n in
                               ((ap_ref, a_ref, an_ref), (bp_ref, b_ref, bn_ref), (cp_ref, c_ref, cn_ref))], axis=1)
        y = jnp.dot(mix, wout_ref[...], preferred_element_type=F32)[MIX_HALO - HALO:MIX_HALO + TM + HALO, :]
        xe = with_halo(xp_ref, x_ref, xn_ref, j, HALO)
        x1e = _ln(ALPHA * xe + mods[2:3, :] * y) * g1_ref[...] + b1_ref[...]
        h = _ln(x1e) * (1.0 + mods[4:5, :]) + mods[3:4, :]
        r = lax.broadcasted_iota(jnp.int32, (n_e, 1), 0)
        keep = ((r >= HALO) | (tile >= 2)) & ((r < HALO + TM) | ((tile >= 1) & (tile < N_TILES - 1)))
        return dict(h=jnp.where(keep, h, 0.0).astype(BF16), x1=x1e[HALO:HALO + TM, :], gate=mods[5:6, :])

    def epilogue(j, w):
        z = _ln(ALPHA * w['x1'] + w['gate'] * w['acc'])
        o_ref[0, j * TM:(j + 1) * TM, :] = z * g2_ref[...] + b2_ref[...]

    def up(h, c, base):
        return jnp.dot(h, wup_ref[:, base + c * FF_CHUNK:base + (c + 1) * FF_CHUNK], preferred_element_type=F32)

    def conv(u, c, base):
        cols = slice(base + c * FF_CHUNK, base + (c + 1) * FF_CHUNK)
        u = (pltpu.roll(u, 1, axis=0) * cw_ref[0:1, cols] + u * cw_ref[1:2, cols]
             + pltpu.roll(u, n_e - 1, axis=0) * cw_ref[2:3, cols])
        return u[HALO:HALO + TM, :] + cb_ref[:, cols]

    def hidden_chunks(w, hooks):
        ahead = 2
        acc = jnp.zeros((TM, D), F32)
        pending = [(up(w['h'], c, 0), up(w['h'], c, D_FF)) for c in range(ahead)]
        for c in range(n_chunk):
            u_gate, u_val = pending.pop(0)
            if c + ahead < n_chunk:
                pending.append((up(w['h'], c + ahead, 0), up(w['h'], c + ahead, D_FF)))
            if c in hooks:
                hooks[c]()
            a = (_silu(conv(u_gate, c, 0)) * conv(u_val, c, D_FF)).astype(BF16)
            acc = acc + jnp.dot(a, wdn_ref[c * FF_CHUNK:(c + 1) * FF_CHUNK, :], preferred_element_type=F32)
        w['acc'] = acc

    work = [None] * n_sub
    work[0] = prologue(0)
    for j in range(n_sub):
        hooks = {}
        if j > 0:
            hooks[0] = functools.partial(epilogue, j - 1, work[j - 1])
        if j + 1 < n_sub:
            hooks[n_chunk // 2] = functools.partial(lambda k: work.__setitem__(k, prologue(k)), j + 1)
        hidden_chunks(work[j], hooks)
    epilogue(n_sub - 1, work[n_sub - 1])


def _outproj_ffn(oa, ob, oc, xs, modsel, w_out_p, ln1_g, ln1_b, w_up, conv_w, conv_b, w_down, ln2_g, ln2_b, t0):
    b = xs.shape[0]
    n_sub = FFN_TILES_PER_STEP if t0 == 0 else 1
    assert (N_TILES - t0) % n_sub == 0
    rows = n_sub * TM
    full = lambda i, s: (0, 0)

    def halos(width, halo):
        per, last = rows // halo, T_ALL // halo - 1
        first = t0 * TM // halo
        return (pl.BlockSpec((1, halo, width), lambda i, s: (i, jnp.maximum(first + s * per - 1, first), 0)),
                pl.BlockSpec((1, rows, width), lambda i, s: (i, s + t0, 0)),
                pl.BlockSpec((1, halo, width), lambda i, s: (i, jnp.minimum(first + (s + 1) * per, last), 0)))

    return pl.pallas_call(
        functools.partial(_outproj_ffn_kernel, n_sub, t0),
        grid=(b, (N_TILES - t0) // n_sub),
        in_specs=[*halos(WA, MIX_HALO), *halos(WB, MIX_HALO), *halos(WC, MIX_HALO), *halos(D, HALO),
                  pl.BlockSpec((1, 2, 8, D), lambda i, s: (i, 0, 0, 0)),
                  pl.BlockSpec((D, D), full),
                  pl.BlockSpec((1, D), full),
                  pl.BlockSpec((1, D), full),
                  pl.BlockSpec((D, 2 * D_FF), full),
                  pl.BlockSpec((3, 2 * D_FF), full),
                  pl.BlockSpec((1, 2 * D_FF), full),
                  pl.BlockSpec((D_FF, D), full),
                  pl.BlockSpec((1, D), full),
                  pl.BlockSpec((1, D), full)],
        out_specs=pl.BlockSpec((1, rows, D), lambda i, s: (i, s, 0)),
        out_shape=jax.ShapeDtypeStruct((b, T_ALL - t0 * TM, D), F32),
        compiler_params=_params("parallel", "arbitrary"),
        name="outproj_ffn",
    )(oa, oa, oa, ob, ob, ob, oc, oc, oc, xs, xs, xs, modsel, w_out_p, ln1_g, ln1_b,
      w_up, conv_w, conv_b, w_down, ln2_g, ln2_b)


def _in_col_perm():
    off, o = {}, 0
    for name, width in (('a_k', WA), ('a_v', WA), ('b_k', WKV), ('b_v', WKV), ('c_f', WC), ('c_b', WC),
                        ('c_i', WC), ('a_q', WA), ('b_q', WB), ('c_q', WC), ('c_g', WC)):
        off[name] = np.arange(o, o + width)
        o += width
    bq = off['b_q'].reshape(GQA_HEADS, HD)[list(GQA_ORDER)].reshape(-1)
    return np.concatenate([off['a_q'], off['a_k'], off['a_v'], bq, off['b_k'], off['b_v'],
                           off['c_f'], off['c_b'], off['c_i'], off['c_q'], off['c_g']])


def _out_row_perm():
    ob = (WA + np.arange(WB)).reshape(GQA_HEADS, HD)[list(GQA_ORDER)].reshape(-1)
    return np.concatenate([np.arange(WA), ob, WA + WB + np.arange(WC)])


def _take_runs(w, perm, axis):
    cuts = [0] + [i for i in range(1, len(perm)) if perm[i] != perm[i - 1] + 1] + [len(perm)]
    parts = [lax.slice_in_dim(w, int(perm[a]), int(perm[z - 1]) + 1, axis=axis) for a, z in zip(cuts[:-1], cuts[1:])]
    return jnp.concatenate(parts, axis=axis)


def kernel(x, c, ctx, c_ctx, w_ada, b_ada, w_in, w_out, na_rpb, q_norm, k_norm, hg_lower_bound, hg_norm,
           ln1_g, ln1_b, w_up, conv_w, conv_b, w_down, ln2_g, ln2_b):
    b = x.shape[0]
    assert x.shape == (b, SEQ, D) and ctx.shape == (b, CTX, D)
    n_rows = -(-(b + 1) // 8) * 8
    s_in = jnp.concatenate([c, c_ctx[None, :], jnp.zeros((n_rows - b - 1, D), F32)], axis=0)
    mods = _modulation(s_in, w_ada, b_ada).reshape(DEPTH, n_rows, N_MOD, D)
    lbs = _lower_bounds(hg_lower_bound).reshape(DEPTH, 2, WC)

    cos, sin = (jnp.asarray(a) for a in _rope_tables())
    tri = jnp.asarray(_hgrn_consts(), BF16)
    ones = jnp.asarray(np.kron(np.eye(HG_HEADS), np.ones((HD, HD))), BF16)
    in_perm = _in_col_perm()
    out_perm = _out_row_perm()
    tile2 = lambda g: jnp.tile(g.astype(F32), 2).reshape(1, LANES)

    xs = jnp.concatenate([ctx, x], axis=1)
    for l in range(DEPTH):
        last = l == DEPTH - 1
        t0 = 1 if last else 0
        lat = mods[l, :b]
        cx = jnp.broadcast_to(mods[l, b][None], lat.shape)
        modsel = jnp.pad(jnp.stack([cx, lat], axis=1), ((0, 0), (0, 0), (0, 8 - N_MOD), (0, 0)))
        ya, yb, yc = _in_projection(xs, modsel, _take_runs(w_in[l], in_perm, 1).astype(BF16))
        oa, ob = _mixers_ab(ya, yb, _na_slabs(na_rpb[l]), cos, sin, tile2(q_norm[l]), tile2(k_norm[l]))
        oc = _mixer_c(yc, lbs[l], tile2(hg_norm[l]), tri, ones)
        xs = _outproj_ffn(oa, ob, oc, xs, modsel, _take_runs(w_out[l], out_perm, 0).astype(BF16),
                          ln1_g[l].reshape(1, D), ln1_b[l].reshape(1, D), w_up[l].astype(BF16), conv_w[l],
                          conv_b[l].reshape(1, 2 * D_FF), w_down[l].astype(BF16),
                          ln2_g[l].reshape(1, D), ln2_b[l].reshape(1, D), t0)
    return xs
```

```python
import functools

import numpy as np
import jax
import jax.numpy as jnp
from jax import lax
from jax.experimental import pallas as pl
from jax.experimental.pallas import tpu as pltpu

D = 1024
DEPTH = 4
GRID_W = 64
CTX = 256
SEQ = 2048
ROWS = SEQ // GRID_W
T_ALL = CTX + SEQ
HD = 64
NA_HEADS = 6
NA_WIN_ROWS = 8
NA_WIN_COLS = 16
GQA_HEADS = 6
GQA_KV = 2
HG_HEADS = 4
HG_CHUNK = 64
HG_SUB = 16
ROPE_THETA = 10000.0
LB_FLOOR = 1e-30
WA = NA_HEADS * HD
WB = GQA_HEADS * HD
WKV = GQA_KV * HD
WC = HG_HEADS * HD
IN_COLS = 3072
D_FF = 2816
N_MOD = 6
ALPHA = (2.0 * DEPTH) ** 0.25
LN_EPS = 1e-6
RMS_EPS = 1e-6
ATT_SCALE = HD ** -0.5
LOG2E = 1.4426950408889634
NEG = -1e30

LANES = 128
TM = 256
N_TILES = T_ALL // TM
NA_RB = 4
NA_WR = NA_WIN_ROWS + NA_RB - 1
NA_NLOC = NA_WR * GRID_W
FF_CHUNK = 256
FFN_TILES_PER_STEP = 3
IN_TILES = 3
HALO = 8
MIX_HALO = 16
VMEM_LIMIT = 56 * 1024 * 1024

YA_COLS = 3 * WA
YB_COLS = WB + 2 * WKV
YC_COLS = 5 * WC
GQA_ORDER = (0, 3, 1, 4, 2, 5)

HI = lax.Precision.HIGHEST
F32 = jnp.float32
BF16 = jnp.bfloat16


def _params(*sem):
    return pltpu.CompilerParams(dimension_semantics=sem, vmem_limit_bytes=VMEM_LIMIT)


def _dot_nt(a, b):
    return lax.dot_general(a, b, (((1,), (1,)), ((), ())), preferred_element_type=F32)


def _dot_tn(a, b):
    return lax.dot_general(a, b, (((0,), (0,)), ((), ())), preferred_element_type=F32)


def _ln(x):
    mu = jnp.mean(x, axis=-1, keepdims=True)
    xc = x - mu
    var = jnp.mean(xc * xc, axis=-1, keepdims=True)
    return xc * lax.rsqrt(var + LN_EPS)


def _dot_01(m, x):
    hi = x.astype(BF16)
    rest = x - hi.astype(F32)
    mid = rest.astype(BF16)
    lo = (rest - mid.astype(F32)).astype(BF16)
    return (jnp.dot(m, hi, preferred_element_type=F32) + jnp.dot(m, mid, preferred_element_type=F32)
            + jnp.dot(m, lo, preferred_element_type=F32))


def _silu(x):
    return x * jax.nn.sigmoid(x)


def _lo_mask(shape):
    return lax.broadcasted_iota(jnp.int32, shape, len(shape) - 1) % LANES < HD


def _head_rms(x, gain):
    lo = _lo_mask(x.shape)
    x2 = x * x
    s_lo = jnp.sum(jnp.where(lo, x2, 0.0), axis=-1, keepdims=True)
    s_hi = jnp.sum(jnp.where(lo, 0.0, x2), axis=-1, keepdims=True)
    ms = jnp.where(lo, s_lo, s_hi) * (1.0 / HD)
    return x * lax.rsqrt(ms + RMS_EPS) * gain


def _mod_kernel(s_ref, w_ref, b_ref, o_ref):
    s = _silu(s_ref[...])
    o_ref[0] = jnp.dot(s, w_ref[0], precision=HI, preferred_element_type=F32) + b_ref[0]


def _modulation(s_in, w_ada, b_ada):
    r = s_in.shape[0]
    cb = 1536
    return pl.pallas_call(
        _mod_kernel,
        grid=(DEPTH, N_MOD * D // cb),
        in_specs=[pl.BlockSpec((r, D), lambda l, j: (0, 0)),
                  pl.BlockSpec((1, D, cb), lambda l, j: (l, 0, j)),
                  pl.BlockSpec((1, 1, cb), lambda l, j: (l, 0, j))],
        out_specs=pl.BlockSpec((1, r, cb), lambda l, j: (l, 0, j)),
        out_shape=jax.ShapeDtypeStruct((DEPTH, r, N_MOD * D), F32),
        compiler_params=_params("arbitrary", "arbitrary"),
        name="adaln_mod",
    )(s_in, w_ada, b_ada.reshape(DEPTH, 1, N_MOD * D))


def _lb_kernel(p_ref, o_ref):
    p = p_ref[...]
    e = jnp.exp(p - jnp.max(p, axis=0, keepdims=True))
    sm = e / jnp.sum(e, axis=0, keepdims=True)
    acc = sm[0:1]
    rows = [acc - sm[0:1]]
    for l in range(1, DEPTH):
        acc = acc + sm[l:l + 1]
        rows.append(acc - sm[0:1])
    o_ref[...] = jnp.concatenate(rows, axis=0)


def _lower_bounds(hg_lower_bound):
    p = hg_lower_bound.reshape(DEPTH, 2 * WC).astype(F32)
    return pl.pallas_call(
        _lb_kernel,
        out_shape=jax.ShapeDtypeStruct((DEPTH, 2 * WC), F32),
        name="hgrn_lower_bounds",
    )(p)


def _inproj_kernel(x_ref, mod_ref, w_ref, ya_ref, yb_ref, yc_ref):
    step = pl.program_id(1)
    hs = []
    for j in range(IN_TILES):
        mods = mod_ref[0, 1] if j > 0 else jnp.where(step == 0, mod_ref[0, 0], mod_ref[0, 1])
        y = _ln(x_ref[0, j * TM:(j + 1) * TM, :])
        hs.append((y * (1.0 + mods[1:2, :]) + mods[0:1, :]).astype(BF16))
    for j, h in enumerate(hs):
        rows = slice(j * TM, (j + 1) * TM)
        out = jnp.dot(h, w_ref[...], preferred_element_type=F32)
        ya_ref[0, rows, :] = out[:, :YA_COLS].astype(BF16)
        yb_ref[0, rows, :] = out[:, YA_COLS:YA_COLS + YB_COLS].astype(BF16)
        yc_ref[0, rows, :] = out[:, YA_COLS + YB_COLS:]


def _in_projection(xs, modsel, w_in_p):
    b = xs.shape[0]
    row = lambda i, t: (i, t, 0)
    return pl.pallas_call(
        _inproj_kernel,
        grid=(b, N_TILES // IN_TILES),
        in_specs=[pl.BlockSpec((1, IN_TILES * TM, D), row),
                  pl.BlockSpec((1, 2, 8, D), lambda i, t: (i, 0, 0, 0)),
                  pl.BlockSpec((D, IN_COLS), lambda i, t: (0, 0))],
        out_specs=[pl.BlockSpec((1, IN_TILES * TM, YA_COLS), row),
                   pl.BlockSpec((1, IN_TILES * TM, YB_COLS), row),
                   pl.BlockSpec((1, IN_TILES * TM, YC_COLS), row)],
        out_shape=[jax.ShapeDtypeStruct((b, T_ALL, YA_COLS), BF16),
                   jax.ShapeDtypeStruct((b, T_ALL, YB_COLS), BF16),
                   jax.ShapeDtypeStruct((b, T_ALL, YC_COLS), F32)],
        compiler_params=_params("parallel", "arbitrary"),
        name="in_projection",
    )(xs, modsel, w_in_p)


def _na_slab_index():
    def table(rb):
        r0 = NA_RB * rb
        ws = int(np.clip(r0 - NA_WIN_ROWS // 2, 0, ROWS - NA_WR))
        qr = r0 + np.arange(NA_RB)[:, None]
        kr = ws + np.arange(NA_WR)[None, :]
        rs = np.clip(qr - NA_WIN_ROWS // 2, 0, ROWS - NA_WIN_ROWS)
        valid = (kr >= rs) & (kr < rs + NA_WIN_ROWS)
        return np.where(valid, kr - qr + NA_WIN_ROWS - 1, 2 * NA_WIN_ROWS - 1)

    n_rb = ROWS // NA_RB
    for rb in range(2, n_rb - 1):
        assert np.array_equal(table(rb), table(1))
    return np.stack([table(0), table(1), table(n_rb - 1)])


def _na_slabs(rpb):
    qc = np.arange(GRID_W)[:, None]
    kc = np.arange(GRID_W)[None, :]
    cs = np.clip(qc - NA_WIN_COLS // 2, 0, GRID_W - NA_WIN_COLS)
    valid = (kc >= cs) & (kc < cs + NA_WIN_COLS)
    onehot = (valid[None] & ((kc - qc + NA_WIN_COLS - 1)[None] == np.arange(2 * NA_WIN_COLS - 1)[:, None, None]))
    onehot = jnp.asarray(onehot.reshape(2 * NA_WIN_COLS - 1, GRID_W * GRID_W), F32)
    slabs = jnp.einsum('hrd,dx->hrx', rpb.astype(F32), onehot, precision=HI)
    slabs = jnp.where(jnp.asarray(valid.reshape(-1)), slabs, NEG)
    slabs = jnp.concatenate([slabs, jnp.full((NA_HEADS, 1, GRID_W * GRID_W), NEG, F32)], axis=1)
    slabs = slabs.reshape(NA_HEADS, 2 * NA_WIN_ROWS, GRID_W, GRID_W)
    return jnp.concatenate([slabs, slabs], axis=-1)


def _na_kernel(idx, q_ref, k_ref, v_ref, slab_ref, o_ref, bias_ref):
    t = pl.program_id(1)
    n_blk = NA_HEADS // 2
    lo = _lo_mask((1, LANES))

    @pl.when((pl.program_id(0) == 0) & (t == 0))
    def _():
        for head in range(NA_HEADS):
            for typ in range(3):
                for rl in range(NA_RB):
                    rows = slice(rl * GRID_W, (rl + 1) * GRID_W)
                    for m in range(NA_WR // 2):
                        even = slab_ref[head, int(idx[typ, rl, 2 * m])]
                        odd = slab_ref[head, int(idx[typ, rl, 2 * m + 1])]
                        bias_ref[head, typ, rows, m * LANES:(m + 1) * LANES] = jnp.where(lo, even, odd)
                    if NA_WR % 2:
                        tail = slab_ref[head, int(idx[typ, rl, NA_WR - 1])]
                        bias_ref[head, typ, rows, (NA_WR - 1) * GRID_W:NA_NLOC] = tail[:, 0:GRID_W]

    def masked_queries(head):
        blk, half = divmod(head, 2)
        qs = q_ref[0, :, blk * LANES:(blk + 1) * LANES] * ATT_SCALE
        zero = jnp.zeros_like(qs)
        return jnp.where(lo, qs, zero) if half == 0 else jnp.where(lo, zero, qs)

    def cols(head):
        blk = head // 2
        return slice(blk * LANES, (blk + 1) * LANES)

    def attend(scores, finish):
        ahead = 2
        pending = [scores(h) for h in range(ahead)]
        outs = []
        for head in range(NA_HEADS):
            s = pending.pop(0)
            if head + ahead < NA_HEADS:
                pending.append(scores(head + ahead))
            outs.append(finish(head, s))
            if head % 2:
                o_ref[0, :, cols(head)] = jnp.where(lo, outs[head - 1], outs[head]).astype(o_ref.dtype)

    @pl.when(t == 0)
    def _():
        def scores(head):
            return _dot_nt(masked_queries(head), k_ref[0, 0:CTX, cols(head)])

        def finish(head, s):
            p = jnp.exp(s - jnp.max(s, axis=-1, keepdims=True))
            l = jnp.sum(p, axis=-1, keepdims=True)
            return jnp.dot(p.astype(BF16), v_ref[0, 0:CTX, cols(head)], preferred_element_type=F32) / l
        attend(scores, finish)

    @pl.when(t > 0)
    def _():
        rb = t - 1
        ws = jnp.clip(NA_RB * rb - NA_WIN_ROWS // 2, 0, ROWS - NA_WR)
        window = pl.ds(pl.multiple_of(CTX + ws * GRID_W, GRID_W), NA_NLOC)
        typ = jnp.where(rb == 0, 0, jnp.where(rb == ROWS // NA_RB - 1, 2, 1))

        def scores(head):
            qm = masked_queries(head)
            return (_dot_nt(qm, k_ref[0, window, cols(head)]) + bias_ref[head, typ],
                    _dot_nt(qm, k_ref[0, 0:CTX, cols(head)]))

        def finish(head, s):
            s_loc, s_ctx = s
            m = jnp.maximum(jnp.max(s_loc, axis=-1, keepdims=True), jnp.max(s_ctx, axis=-1, keepdims=True))
            p_loc = jnp.exp(s_loc - m)
            p_ctx = jnp.exp(s_ctx - m)
            l = jnp.sum(p_loc, axis=-1, keepdims=True) + jnp.sum(p_ctx, axis=-1, keepdims=True)
            o = (jnp.dot(p_loc.astype(BF16), v_ref[0, window, cols(head)], preferred_element_type=F32)
                 + jnp.dot(p_ctx.astype(BF16), v_ref[0, 0:CTX, cols(head)], preferred_element_type=F32))
            return o / l
        attend(scores, finish)


def _rope_tables():
    t = np.arange(SEQ)
    pos = np.stack([t // GRID_W, t % GRID_W]).astype(np.float32)
    n_freq = HD // 4
    inv_freq = (ROPE_THETA ** (-np.arange(n_freq, dtype=np.float32) / n_freq)).astype(np.float32)
    ang = pos[:, :, None] * inv_freq
    cos, sin = np.cos(ang), np.sin(ang)
    cos_h = np.concatenate([cos[0], cos[0], cos[1], cos[1]], axis=-1)
    sin_h = np.concatenate([-sin[0], sin[0], -sin[1], sin[1]], axis=-1)
    cos_all = np.concatenate([np.ones((CTX, HD), np.float32), cos_h], axis=0)
    sin_all = np.concatenate([np.zeros((CTX, HD), np.float32), sin_h], axis=0)
    return (np.tile(cos_all, (1, 2)).astype(np.float32), np.tile(sin_all, (1, 2)).astype(np.float32))


def _rope(x, cos, sin):
    quarter = HD // 4
    first = lax.broadcasted_iota(jnp.int32, x.shape, 1) % (2 * quarter) < quarter
    partner = jnp.where(first, pltpu.roll(x, LANES - quarter, axis=1), pltpu.roll(x, quarter, axis=1))
    return x * cos + partner * sin


def _gqa_kernel(q_ref, k_ref, v_ref, cos_ref, sin_ref, qg_ref, kg_ref, o_ref, kn_ref):
    t = pl.program_id(1)
    lo = _lo_mask((1, LANES))

    @pl.when(t == 0)
    def _():
        def prep(i, carry):
            rows = pl.ds(pl.multiple_of(i * TM, TM), TM)
            kx = _head_rms(k_ref[0, rows, :].astype(F32), kg_ref[...])
            kn_ref[rows, :] = _rope(kx, cos_ref[rows, :], sin_ref[rows, :]).astype(BF16)
            return carry
        lax.fori_loop(0, N_TILES, prep, 0)

    rows = pl.ds(pl.multiple_of(t * TM, TM), TM)
    cos = cos_ref[rows, :]
    sin = sin_ref[rows, :]

    def attend(n_keys):
        kn = kn_ref[0:n_keys, :]
        vv = v_ref[0, 0:n_keys, :]

        qn = []
        for j in range(GQA_HEADS // 2):
            qx = _head_rms(q_ref[0, :, j * LANES:(j + 1) * LANES].astype(F32), qg_ref[...])
            qn.append((_rope(qx, cos, sin) * ATT_SCALE).astype(BF16))
        zero = jnp.zeros_like(qn[0])

        def scores(i):
            j, h = divmod(i, 2)
            return _dot_nt(jnp.where(lo, qn[j], zero) if h == 0 else jnp.where(lo, zero, qn[j]), kn)

        ahead = 1
        pending = [scores(i) for i in range(ahead)]
        outs = []
        for i in range(GQA_HEADS):
            s = pending.pop(0)
            if i + ahead < GQA_HEADS:
                pending.append(scores(i + ahead))
            p = jnp.exp(s - jnp.max(s, axis=-1, keepdims=True))
            l = jnp.sum(p, axis=-1, keepdims=True)
            outs.append(jnp.dot(p.astype(BF16), vv, preferred_element_type=F32) / l)
            if i % 2:
                j = i // 2
                o_ref[0, :, j * LANES:(j + 1) * LANES] = jnp.where(lo, outs[i - 1], outs[i]).astype(o_ref.dtype)

    @pl.when(t == 0)
    def _():
        attend(CTX)

    @pl.when(t > 0)
    def _():
        attend(T_ALL)


def _attention_kernel(idx, qa_ref, ka_ref, va_ref, slab_ref, qb_ref, kb_ref, vb_ref, cos_ref, sin_ref, qg_ref, kg_ref,
                      oa_ref, ob_ref, bias_ref, kn_ref):
    _na_kernel(idx, qa_ref, ka_ref, va_ref, slab_ref, oa_ref, bias_ref)
    _gqa_kernel(qb_ref, kb_ref, vb_ref, cos_ref, sin_ref, qg_ref, kg_ref, ob_ref, kn_ref)


def _mixers_ab(ya, yb, slabs, cos, sin, qg, kg):
    b = ya.shape[0]
    kcol = WB // LANES
    tile = lambda i, t: (i, t, 0)
    full = lambda i, t: (0, 0)
    return pl.pallas_call(
        functools.partial(_attention_kernel, _na_slab_index()),
        grid=(b, N_TILES),
        in_specs=[pl.BlockSpec((1, TM, WA), tile),
                  pl.BlockSpec((1, T_ALL, WA), lambda i, t: (i, 0, 1)),
                  pl.BlockSpec((1, T_ALL, WA), lambda i, t: (i, 0, 2)),
                  pl.BlockSpec((NA_HEADS, 2 * NA_WIN_ROWS, GRID_W, LANES), lambda i, t: (0, 0, 0, 0)),
                  pl.BlockSpec((1, TM, WB), tile),
                  pl.BlockSpec((1, T_ALL, LANES), lambda i, t: (i, 0, kcol)),
                  pl.BlockSpec((1, T_ALL, LANES), lambda i, t: (i, 0, kcol + 1)),
                  pl.BlockSpec((T_ALL, LANES), full),
                  pl.BlockSpec((T_ALL, LANES), full),
                  pl.BlockSpec((1, LANES), full),
                  pl.BlockSpec((1, LANES), full)],
        out_specs=[pl.BlockSpec((1, TM, WA), tile),
                   pl.BlockSpec((1, TM, WB), tile)],
        out_shape=[jax.ShapeDtypeStruct((b, T_ALL, WA), BF16),
                   jax.ShapeDtypeStruct((b, T_ALL, WB), BF16)],
        scratch_shapes=[pltpu.VMEM((NA_HEADS, 3, NA_RB * GRID_W, NA_NLOC), F32),
                        pltpu.VMEM((T_ALL, LANES), BF16)],
        compiler_params=_params("arbitrary", "arbitrary"),
        name="mixers_ab_attention",
    )(ya, ya, ya, slabs, yb, yb, yb, cos, sin, qg, kg)


def _hgrn_consts():
    c, sc = HG_CHUNK, HG_SUB
    t = np.arange(c)
    out = []
    for reverse in (False, True):
        if not reverse:
            cum = t[None, :] <= t[:, None]
            through = (t[None, :] // sc) <= (t[:, None] // sc)
        else:
            cum = t[None, :] >= t[:, None]
            through = (t[None, :] // sc) >= (t[:, None] // sc)
        out.append(np.concatenate([cum, through], axis=0).astype(np.float32))
    return np.stack(out)


def _forget_gate(z, lb):
    ez = jnp.exp(-jnp.abs(z))
    big = 1.0 / (1.0 + ez)
    small = ez * big
    pos = z >= 0.0
    lb_floor = jnp.maximum(lb, LB_FLOOR)
    f = lb_floor + (1.0 - lb) * jnp.where(pos, big, small)
    one_minus_f = (1.0 - lb) * jnp.where(pos, small, big) - (lb_floor - lb)
    return jnp.log(f), one_minus_f


def _hgrn_kernel(zf_ref, zb_ref, v_ref, q_ref, g_ref, lb_ref, gn_ref, tri_ref, ones_ref,
                 o_ref, acc_ref, st_ref, d_ref, b_ref):
    c, sc = HG_CHUNK, HG_SUB
    n_sub = c // sc
    n_pair = WC // LANES
    acc_ref[...] = jnp.zeros_like(acc_ref)
    st_ref[...] = jnp.zeros_like(st_ref)
    sub_row = lax.broadcasted_iota(jnp.int32, (sc, 1), 0)
    blk = (lax.broadcasted_iota(jnp.int32, (LANES, LANES), 0) // HD
           == lax.broadcasted_iota(jnp.int32, (LANES, LANES), 1) // HD)
    own_head = (lax.broadcasted_iota(jnp.int32, (HG_HEADS * c, WC), 0) // c
                == lax.broadcasted_iota(jnp.int32, (HG_HEADS * c, WC), 1) // HD)
    q_sub = lax.broadcasted_iota(jnp.int32, (c, WC), 0) // sc
    k_sub = (lax.broadcasted_iota(jnp.int32, (c, WC), 1) % c) // sc
    k_lane = lax.broadcasted_iota(jnp.int32, (1, WC), 1) % c

    z_refs = (zf_ref, zb_ref)
    n_ctx = CTX // c
    n_all = T_ALL // c

    def chunk_of(d, n):
        return n if d == 0 else jnp.where(n < n_ctx, n_ctx - 1 - n, n_all + n_ctx - 1 - n)

    def rows_of(d, n):
        return pl.ds(pl.multiple_of(chunk_of(d, n) * c, c), c)

    def gates(d, n, slot):
        g, kk = _forget_gate(z_refs[d][0, rows_of(d, n), :], lb_ref[d:d + 1, :])
        cums = _dot_01(tri_ref[d], g * LOG2E)
        b_ref[d, slot, 0] = cums[0:c]
        b_ref[d, slot, 1] = kk
        b_ref[d, slot, 2] = cums[c:2 * c]

    def state_and_far(d, n, slot):
        reverse = d == 1
        rows = rows_of(d, n)
        b = b_ref[d, slot, 0]
        kk = b_ref[d, slot, 1]
        e = b_ref[d, slot, 2]
        q = q_ref[0, rows, :] * ATT_SCALE
        vb = v_ref[0, rows, :].astype(BF16)
        last = c - 1 if not reverse else 0
        b_last = b_ref[d, slot, 0, last:last + 1, :]

        qe = (q * jnp.exp2(b)).astype(BF16)
        kdec = (kk * jnp.exp2(b_last - b)).astype(BF16)
        o_parts = []
        for p in range(n_pair):
            ln = slice(p * LANES, (p + 1) * LANES)
            st = st_ref[d, p]
            o_parts.append(_dot_nt(qe[:, ln], st.astype(BF16)))
            upd = _dot_tn(vb[:, ln], kdec[:, ln])
            st_ref[d, p] = jnp.where(blk, st * jnp.exp2(b_last[:, ln]) + upd, 0.0)
        o = jnp.concatenate(o_parts, axis=1)

        kx = (kk * jnp.exp2(e - b)).astype(BF16)
        keys = jnp.where(own_head, jnp.concatenate([kx] * HG_HEADS, axis=0), jnp.zeros((), BF16))
        vals = jnp.where(own_head, jnp.concatenate([vb] * HG_HEADS, axis=0), jnp.zeros((), BF16))
        q_stack = []
        for j in range(n_sub - 1):
            jj = j if not reverse else n_sub - 1 - j
            edge = (jj + 1) * sc - 1 if not reverse else jj * sc
            e_j = b_ref[d, slot, 0, edge:edge + 1, :]
            q_stack.append((q * jnp.exp2(jnp.minimum(b - e_j, 0.0))).astype(BF16))
        res = _dot_nt(jnp.concatenate(q_stack, axis=0), keys)
        att = jnp.zeros((c, WC), F32)
        for j in range(n_sub - 1):
            jj = j if not reverse else n_sub - 1 - j
            later = q_sub > jj if not reverse else q_sub < jj
            att = att + jnp.where((k_sub == jj) & later, res[j * c:(j + 1) * c, :], 0.0)
        return dict(rows=rows, b=b, q=q, o=o, att=att, vals=vals)

    def near_products(d, slot, w):
        dk = d * 2 + slot % 2
        reverse = d == 1
        for i in range(n_sub):
            bi = w['b'][i * sc:(i + 1) * sc, :]
            qi = w['q'][i * sc:(i + 1) * sc, :]
            for s in range(sc):
                r = i * sc + s
                p = jnp.exp2(bi - b_ref[d, slot, 0, r:r + 1, :]) * (qi * b_ref[d, slot, 1, r:r + 1, :])
                keep = sub_row >= s if not reverse else sub_row <= s
                d_ref[dk, r * sc:(r + 1) * sc, :] = jnp.where(keep, p, 0.0).astype(BF16)
        w['sums'] = jnp.dot(d_ref[dk], ones_ref[...], preferred_element_type=F32)

    def finish(d, w):
        near = []
        for i in range(n_sub):
            a = jnp.zeros((sc, WC), F32)
            for s in range(sc):
                r = i * sc + s
                a = jnp.where(k_lane == r, w['sums'][r * sc:(r + 1) * sc, :], a)
            near.append(a)
        att = w['att'] + jnp.concatenate(near, axis=0)
        acc_ref[w['rows'], :] += w['o'] + jnp.dot(att.astype(BF16), w['vals'], preferred_element_type=F32)

    def two_steps(n0, slots, next_slots):
        streams = [(k, d) for k in range(2) for d in range(2)]
        for k, d in streams:
            gates(d, jnp.minimum(n0 + 2 + k, n_all - 1), next_slots[k])
        work = {(k, d): state_and_far(d, n0 + k, slots[k]) for k, d in streams}
        for k, d in streams:
            near_products(d, slots[k], work[k, d])
        for k, d in streams:
            finish(d, work[k, d])

    for k in range(2):
        for d in range(2):
            gates(d, k, k)

    def body(m, carry):
        two_steps(4 * m, (0, 1), (2, 3))
        two_steps(4 * m + 2, (2, 3), (0, 1))
        return carry
    lax.fori_loop(0, n_all // 4, body, 0)

    def readout(i, carry):
        rows = pl.ds(pl.multiple_of(i * TM, TM), TM)
        gate = _silu(g_ref[0, rows, :])
        for p in range(n_pair):
            ln = slice(p * LANES, (p + 1) * LANES)
            y = _head_rms(acc_ref[rows, ln], gn_ref[...]) * gate[:, ln]
            o_ref[0, rows, ln] = y.astype(o_ref.dtype)
        return carry
    lax.fori_loop(0, N_TILES, readout, 0)


def _mixer_c(yc, lb, gn, tri, ones):
    b = yc.shape[0]
    c = HG_CHUNK
    col = lambda j: (lambda i: (i, 0, j))
    return pl.pallas_call(
        _hgrn_kernel,
        grid=(b,),
        in_specs=[pl.BlockSpec((1, T_ALL, WC), col(0)),
                  pl.BlockSpec((1, T_ALL, WC), col(1)),
                  pl.BlockSpec((1, T_ALL, WC), col(2)),
                  pl.BlockSpec((1, T_ALL, WC), col(3)),
                  pl.BlockSpec((1, T_ALL, WC), col(4)),
                  pl.BlockSpec((2, WC), lambda i: (0, 0)),
                  pl.BlockSpec((1, LANES), lambda i: (0, 0)),
                  pl.BlockSpec((2, 2 * c, c), lambda i: (0, 0, 0)),
                  pl.BlockSpec((WC, WC), lambda i: (0, 0))],
        out_specs=pl.BlockSpec((1, T_ALL, WC), lambda i: (i, 0, 0)),
        out_shape=jax.ShapeDtypeStruct((b, T_ALL, WC), BF16),
        scratch_shapes=[pltpu.VMEM((T_ALL, WC), F32),
                        pltpu.VMEM((2, WC // LANES, LANES, LANES), F32),
                        pltpu.VMEM((4, c * HG_SUB, WC), BF16),
                        pltpu.VMEM((2, 4, 3, c, WC), F32)],
        compiler_params=_params("parallel"),
        name="mixer_c_hgrn2",
    )(yc, yc, yc, yc, yc, lb, gn, tri, ones)


def _outproj_ffn_kernel(n_sub, t0, ap_ref, a_ref, an_ref, bp_ref, b_ref, bn_ref, cp_ref, c_ref, cn_ref,
                        xp_ref, x_ref, xn_ref, mod_ref, wout_ref, g1_ref, b1_ref, wup_ref, cw_ref, cb_ref, wdn_ref,
                        g2_ref, b2_ref, o_ref):
    step = pl.program_id(1)
    n_e = HALO + TM + HALO
    n_chunk = D_FF // FF_CHUNK

    def with_halo(prev_ref, ref, next_ref, j, halo):
        lo, hi = j * TM - halo, (j + 1) * TM + halo
        parts = [prev_ref[0]] if j == 0 else []
        parts.append(ref[0, max(lo, 0):min(hi, n_sub * TM), :])
        if j == n_sub - 1:
            parts.append(next_ref[0])
        return parts[0] if len(parts) == 1 else jnp.concatenate(parts, axis=0)

    def prologue(j):
        tile = step * n_sub + j + t0
        mods = mod_ref[0, 1] if j > 0 else jnp.where(tile == 0, mod_ref[0, 0], mod_ref[0, 1])
        mix = jnp.concatenate([with_halo(p, m, n, j, MIX_HALO) for p, m, n in
                               ((ap_ref, a_ref, an_ref), (bp_ref, b_ref, bn_ref), (cp_ref, c_ref, cn_ref))], axis=1)
        y = jnp.dot(mix, wout_ref[...], preferred_element_type=F32)[MIX_HALO - HALO:MIX_HALO + TM + HALO, :]
        xe = with_halo(xp_ref, x_ref, xn_ref, j, HALO)
        x1e = _ln(ALPHA * xe + mods[2:3, :] * y) * g1_ref[...] + b1_ref[...]
        h = _ln(x1e) * (1.0 + mods[4:5, :]) + mods[3:4, :]
        r = lax.broadcasted_iota(jnp.int32, (n_e, 1), 0)
        keep = ((r >= HALO) | (tile >= 2)) & ((r < HALO + TM) | ((tile >= 1) & (tile < N_TILES - 1)))
        return dict(h=jnp.where(keep, h, 0.0).astype(BF16), x1=x1e[HALO:HALO + TM, :], gate=mods[5:6, :])

    def epilogue(j, w):
        z = _ln(ALPHA * w['x1'] + w['gate'] * w['acc'])
        o_ref[0, j * TM:(j + 1) * TM, :] = z * g2_ref[...] + b2_ref[...]

    def up(h, c, base):
        return jnp.dot(h, wup_ref[:, base + c * FF_CHUNK:base + (c + 1) * FF_CHUNK], preferred_element_type=F32)

    def conv(u, c, base):
        cols = slice(base + c * FF_CHUNK, base + (c + 1) * FF_CHUNK)
        u = (pltpu.roll(u, 1, axis=0) * cw_ref[0:1, cols] + u * cw_ref[1:2, cols]
             + pltpu.roll(u, n_e - 1, axis=0) * cw_ref[2:3, cols])
        return u[HALO:HALO + TM, :] + cb_ref[:, cols]

    def hidden_chunks(w, hooks):
        ahead = 2
        acc = jnp.zeros((TM, D), F32)
        pending = [(up(w['h'], c, 0), up(w['h'], c, D_FF)) for c in range(ahead)]
        for c in range(n_chunk):
            u_gate, u_val = pending.pop(0)
            if c + ahead < n_chunk:
                pending.append((up(w['h'], c + ahead, 0), up(w['h'], c + ahead, D_FF)))
            if c in hooks:
                hooks[c]()
            a = (_silu(conv(u_gate, c, 0)) * conv(u_val, c, D_FF)).astype(BF16)
            acc = acc + jnp.dot(a, wdn_ref[c * FF_CHUNK:(c + 1) * FF_CHUNK, :], preferred_element_type=F32)
        w['acc'] = acc

    work = [None] * n_sub
    work[0] = prologue(0)
    for j in range(n_sub):
        hooks = {}
        if j > 0:
            hooks[0] = functools.partial(epilogue, j - 1, work[j - 1])
        if j + 1 < n_sub:
            hooks[n_chunk // 2] = functools.partial(lambda k: work.__setitem__(k, prologue(k)), j + 1)
        hidden_chunks(work[j], hooks)
    epilogue(n_sub - 1, work[n_sub - 1])


def _outproj_ffn(oa, ob, oc, xs, modsel, w_out_p, ln1_g, ln1_b, w_up, conv_w, conv_b, w_down, ln2_g, ln2_b, t0):
    b = xs.shape[0]
    n_sub = FFN_TILES_PER_STEP if t0 == 0 else 1
    assert (N_TILES - t0) % n_sub == 0
    rows = n_sub * TM
    full = lambda i, s: (0, 0)

    def halos(width, halo):
        per, last = rows // halo, T_ALL // halo - 1
        first = t0 * TM // halo
        return (pl.BlockSpec((1, halo, width), lambda i, s: (i, jnp.maximum(first + s * per - 1, first), 0)),
                pl.BlockSpec((1, rows, width), lambda i, s: (i, s + t0, 0)),
                pl.BlockSpec((1, halo, width), lambda i, s: (i, jnp.minimum(first + (s + 1) * per, last), 0)))

    return pl.pallas_call(
        functools.partial(_outproj_ffn_kernel, n_sub, t0),
        grid=(b, (N_TILES - t0) // n_sub),
        in_specs=[*halos(WA, MIX_HALO), *halos(WB, MIX_HALO), *halos(WC, MIX_HALO), *halos(D, HALO),
                  pl.BlockSpec((1, 2, 8, D), lambda i, s: (i, 0, 0, 0)),
                  pl.BlockSpec((D, D), full),
                  pl.BlockSpec((1, D), full),
                  pl.BlockSpec((1, D), full),
                  pl.BlockSpec((D, 2 * D_FF), full),
                  pl.BlockSpec((3, 2 * D_FF), full),
                  pl.BlockSpec((1, 2 * D_FF), full),
                  pl.BlockSpec((D_FF, D), full),
                  pl.BlockSpec((1, D), full),
                  pl.BlockSpec((1, D), full)],
        out_specs=pl.BlockSpec((1, rows, D), lambda i, s: (i, s, 0)),
        out_shape=jax.ShapeDtypeStruct((b, T_ALL - t0 * TM, D), F32),
        compiler_params=_params("parallel", "arbitrary"),
        name="outproj_ffn",
    )(oa, oa, oa, ob, ob, ob, oc, oc, oc, xs, xs, xs, modsel, w_out_p, ln1_g, ln1_b,
      w_up, conv_w, conv_b, w_down, ln2_g, ln2_b)


def _in_col_perm():
    off, o = {}, 0
    for name, width in (('a_k', WA), ('a_v', WA), ('b_k', WKV), ('b_v', WKV), ('c_f', WC), ('c_b', WC),
                        ('c_i', WC), ('a_q', WA), ('b_q', WB), ('c_q', WC), ('c_g', WC)):
        off[name] = np.arange(o, o + width)
        o += width
    bq = off['b_q'].reshape(GQA_HEADS, HD)[list(GQA_ORDER)].reshape(-1)
    return np.concatenate([off['a_q'], off['a_k'], off['a_v'], bq, off['b_k'], off['b_v'],
                           off['c_f'], off['c_b'], off['c_i'], off['c_q'], off['c_g']])


def _out_row_perm():
    ob = (WA + np.arange(WB)).reshape(GQA_HEADS, HD)[list(GQA_ORDER)].reshape(-1)
    return np.concatenate([np.arange(WA), ob, WA + WB + np.arange(WC)])


def _take_runs(w, perm, axis):
    cuts = [0] + [i for i in range(1, len(perm)) if perm[i] != perm[i - 1] + 1] + [len(perm)]
    parts = [lax.slice_in_dim(w, int(perm[a]), int(perm[z - 1]) + 1, axis=axis) for a, z in zip(cuts[:-1], cuts[1:])]
    return jnp.concatenate(parts, axis=axis)


def kernel(x, c, ctx, c_ctx, w_ada, b_ada, w_in, w_out, na_rpb, q_norm, k_norm, hg_lower_bound, hg_norm,
           ln1_g, ln1_b, w_up, conv_w, conv_b, w_down, ln2_g, ln2_b):
    b = x.shape[0]
    assert x.shape == (b, SEQ, D) and ctx.shape == (b, CTX, D)
    n_rows = -(-(b + 1) // 8) * 8
    s_in = jnp.concatenate([c, c_ctx[None, :], jnp.zeros((n_rows - b - 1, D), F32)], axis=0)
    mods = _modulation(s_in, w_ada, b_ada).reshape(DEPTH, n_rows, N_MOD, D)
    lbs = _lower_bounds(hg_lower_bound).reshape(DEPTH, 2, WC)

    cos, sin = (jnp.asarray(a) for a in _rope_tables())
    tri = jnp.asarray(_hgrn_consts(), BF16)
    ones = jnp.asarray(np.kron(np.eye(HG_HEADS), np.ones((HD, HD))), BF16)
    in_perm = _in_col_perm()
    out_perm = _out_row_perm()
    tile2 = lambda g: jnp.tile(g.astype(F32), 2).reshape(1, LANES)

    xs = jnp.concatenate([ctx, x], axis=1)
    for l in range(DEPTH):
        last = l == DEPTH - 1
        t0 = 1 if last else 0
        lat = mods[l, :b]
        cx = jnp.broadcast_to(mods[l, b][None], lat.shape)
        modsel = jnp.pad(jnp.stack([cx, lat], axis=1), ((0, 0), (0, 0), (0, 8 - N_MOD), (0, 0)))
        ya, yb, yc = _in_projection(xs, modsel, _take_runs(w_in[l], in_perm, 1).astype(BF16))
        oa, ob = _mixers_ab(ya, yb, _na_slabs(na_rpb[l]), cos, sin, tile2(q_norm[l]), tile2(k_norm[l]))
        oc = _mixer_c(yc, lbs[l], tile2(hg_norm[l]), tri, ones)
        xs = _outproj_ffn(oa, ob, oc, xs, modsel, _take_runs(w_out[l], out_perm, 0).astype(BF16),
                          ln1_g[l].reshape(1, D), ln1_b[l].reshape(1, D), w_up[l].astype(BF16), conv_w[l],
                          conv_b[l].reshape(1, 2 * D_FF), w_down[l].astype(BF16),
                          ln2_g[l].reshape(1, D), ln2_b[l].reshape(1, D), t0)
    return xs
```

```python
import functools

import numpy as np
import jax
import jax.numpy as jnp
from jax import lax
from jax.experimental import pallas as pl
from jax.experimental.pallas import tpu as pltpu

D = 1024
DEPTH = 4
GRID_W = 64
CTX = 256
SEQ = 2048
ROWS = SEQ // GRID_W
T_ALL = CTX + SEQ
HD = 64
NA_HEADS = 6
NA_WIN_ROWS = 8
NA_WIN_COLS = 16
GQA_HEADS = 6
GQA_KV = 2
HG_HEADS = 4
HG_CHUNK = 64
HG_SUB = 16
ROPE_THETA = 10000.0
LB_FLOOR = 1e-30
WA = NA_HEADS * HD
WB = GQA_HEADS * HD
WKV = GQA_KV * HD
WC = HG_HEADS * HD
IN_COLS = 3072
D_FF = 2816
N_MOD = 6
ALPHA = (2.0 * DEPTH) ** 0.25
LN_EPS = 1e-6
RMS_EPS = 1e-6
ATT_SCALE = HD ** -0.5
LOG2E = 1.4426950408889634
NEG = -1e30

LANES = 128
TM = 256
N_TILES = T_ALL // TM
NA_RB = 4
NA_WR = NA_WIN_ROWS + NA_RB - 1
NA_NLOC = NA_WR * GRID_W
FF_CHUNK = 256
FFN_TILES_PER_STEP = 3
IN_TILES = 3
HALO = 8
MIX_HALO = 16
VMEM_LIMIT = 56 * 1024 * 1024

YA_COLS = 3 * WA
YB_COLS = WB + 2 * WKV
YC_COLS = 5 * WC
GQA_ORDER = (0, 3, 1, 4, 2, 5)

HI = lax.Precision.HIGHEST
F32 = jnp.float32
BF16 = jnp.bfloat16


def _params(*sem):
    return pltpu.CompilerParams(dimension_semantics=sem, vmem_limit_bytes=VMEM_LIMIT)


def _dot_nt(a, b):
    return lax.dot_general(a, b, (((1,), (1,)), ((), ())), preferred_element_type=F32)


def _dot_tn(a, b):
    return lax.dot_general(a, b, (((0,), (0,)), ((), ())), preferred_element_type=F32)


def _ln(x):
    mu = jnp.mean(x, axis=-1, keepdims=True)
    xc = x - mu
    var = jnp.mean(xc * xc, axis=-1, keepdims=True)
    return xc * lax.rsqrt(var + LN_EPS)


def _dot_01(m, x):
    hi = x.astype(BF16)
    rest = x - hi.astype(F32)
    mid = rest.astype(BF16)
    lo = (rest - mid.astype(F32)).astype(BF16)
    return (jnp.dot(m, hi, preferred_element_type=F32) + jnp.dot(m, mid, preferred_element_type=F32)
            + jnp.dot(m, lo, preferred_element_type=F32))


def _silu(x):
    return x * jax.nn.sigmoid(x)


def _lo_mask(shape):
    return lax.broadcasted_iota(jnp.int32, shape, len(shape) - 1) % LANES < HD


def _head_rms(x, gain):
    lo = _lo_mask(x.shape)
    x2 = x * x
    s_lo = jnp.sum(jnp.where(lo, x2, 0.0), axis=-1, keepdims=True)
    s_hi = jnp.sum(jnp.where(lo, 0.0, x2), axis=-1, keepdims=True)
    ms = jnp.where(lo, s_lo, s_hi) * (1.0 / HD)
    return x * lax.rsqrt(ms + RMS_EPS) * gain


def _mod_kernel(s_ref, w_ref, b_ref, o_ref):
    s = _silu(s_ref[...])
    o_ref[0] = jnp.dot(s, w_ref[0], precision=HI, preferred_element_type=F32) + b_ref[0]


def _modulation(s_in, w_ada, b_ada):
    r = s_in.shape[0]
    cb = 1536
    return pl.pallas_call(
        _mod_kernel,
        grid=(DEPTH, N_MOD * D // cb),
        in_specs=[pl.BlockSpec((r, D), lambda l, j: (0, 0)),
                  pl.BlockSpec((1, D, cb), lambda l, j: (l, 0, j)),
                  pl.BlockSpec((1, 1, cb), lambda l, j: (l, 0, j))],
        out_specs=pl.BlockSpec((1, r, cb), lambda l, j: (l, 0, j)),
        out_shape=jax.ShapeDtypeStruct((DEPTH, r, N_MOD * D), F32),
        compiler_params=_params("arbitrary", "arbitrary"),
        name="adaln_mod",
    )(s_in, w_ada, b_ada.reshape(DEPTH, 1, N_MOD * D))


def _lb_kernel(p_ref, o_ref):
    p = p_ref[...]
    e = jnp.exp(p - jnp.max(p, axis=0, keepdims=True))
    sm = e / jnp.sum(e, axis=0, keepdims=True)
    acc = sm[0:1]
    rows = [acc - sm[0:1]]
    for l in range(1, DEPTH):
        acc = acc + sm[l:l + 1]
        rows.append(acc - sm[0:1])
    o_ref[...] = jnp.concatenate(rows, axis=0)


def _lower_bounds(hg_lower_bound):
    p = hg_lower_bound.reshape(DEPTH, 2 * WC).astype(F32)
    return pl.pallas_call(
        _lb_kernel,
        out_shape=jax.ShapeDtypeStruct((DEPTH, 2 * WC), F32),
        name="hgrn_lower_bounds",
    )(p)


def _inproj_kernel(x_ref, mod_ref, w_ref, ya_ref, yb_ref, yc_ref):
    step = pl.program_id(1)
    hs = []
    for j in range(IN_TILES):
        mods = mod_ref[0, 1] if j > 0 else jnp.where(step == 0, mod_ref[0, 0], mod_ref[0, 1])
        y = _ln(x_ref[0, j * TM:(j + 1) * TM, :])
        hs.append((y * (1.0 + mods[1:2, :]) + mods[0:1, :]).astype(BF16))
    for j, h in enumerate(hs):
        rows = slice(j * TM, (j + 1) * TM)
        out = jnp.dot(h, w_ref[...], preferred_element_type=F32)
        ya_ref[0, rows, :] = out[:, :YA_COLS].astype(BF16)
        yb_ref[0, rows, :] = out[:, YA_COLS:YA_COLS + YB_COLS].astype(BF16)
        yc_ref[0, rows, :] = out[:, YA_COLS + YB_COLS:]


def _in_projection(xs, modsel, w_in_p):
    b = xs.shape[0]
    row = lambda i, t: (i, t, 0)
    return pl.pallas_call(
        _inproj_kernel,
        grid=(b, N_TILES // IN_TILES),
        in_specs=[pl.BlockSpec((1, IN_TILES * TM, D), row),
                  pl.BlockSpec((1, 2, 8, D), lambda i, t: (i, 0, 0, 0)),
                  pl.BlockSpec((D, IN_COLS), lambda i, t: (0, 0))],
        out_specs=[pl.BlockSpec((1, IN_TILES * TM, YA_COLS), row),
                   pl.BlockSpec((1, IN_TILES * TM, YB_COLS), row),
                   pl.BlockSpec((1, IN_TILES * TM, YC_COLS), row)],
        out_shape=[jax.ShapeDtypeStruct((b, T_ALL, YA_COLS), BF16),
                   jax.ShapeDtypeStruct((b, T_ALL, YB_COLS), BF16),
                   jax.ShapeDtypeStruct((b, T_ALL, YC_COLS), F32)],
        compiler_params=_params("parallel", "arbitrary"),
        name="in_projection",
    )(xs, modsel, w_in_p)


def _na_slab_index():
    def table(rb):
        r0 = NA_RB * rb
        ws = int(np.clip(r0 - NA_WIN_ROWS // 2, 0, ROWS - NA_WR))
        qr = r0 + np.arange(NA_RB)[:, None]
        kr = ws + np.arange(NA_WR)[None, :]
        rs = np.clip(qr - NA_WIN_ROWS // 2, 0, ROWS - NA_WIN_ROWS)
        valid = (kr >= rs) & (kr < rs + NA_WIN_ROWS)
        return np.where(valid, kr - qr + NA_WIN_ROWS - 1, 2 * NA_WIN_ROWS - 1)

    n_rb = ROWS // NA_RB
    for rb in range(2, n_rb - 1):
        assert np.array_equal(table(rb), table(1))
    return np.stack([table(0), table(1), table(n_rb - 1)])


def _na_slabs(rpb):
    qc = np.arange(GRID_W)[:, None]
    kc = np.arange(GRID_W)[None, :]
    cs = np.clip(qc - NA_WIN_COLS // 2, 0, GRID_W - NA_WIN_COLS)
    valid = (kc >= cs) & (kc < cs + NA_WIN_COLS)
    onehot = (valid[None] & ((kc - qc + NA_WIN_COLS - 1)[None] == np.arange(2 * NA_WIN_COLS - 1)[:, None, None]))
    onehot = jnp.asarray(onehot.reshape(2 * NA_WIN_COLS - 1, GRID_W * GRID_W), F32)
    slabs = jnp.einsum('hrd,dx->hrx', rpb.astype(F32), onehot, precision=HI)
    slabs = jnp.where(jnp.asarray(valid.reshape(-1)), slabs, NEG)
    slabs = jnp.concatenate([slabs, jnp.full((NA_HEADS, 1, GRID_W * GRID_W), NEG, F32)], axis=1)
    slabs = slabs.reshape(NA_HEADS, 2 * NA_WIN_ROWS, GRID_W, GRID_W)
    return jnp.concatenate([slabs, slabs], axis=-1)


def _na_stages(idx, q_ref, k_ref, v_ref, slab_ref, o_ref, bias_ref):
    t = pl.program_id(1)
    lo = _lo_mask((1, LANES))

    def build_bias():
        for head in range(NA_HEADS):
            for typ in range(3):
                for rl in range(NA_RB):
                    rows = slice(rl * GRID_W, (rl + 1) * GRID_W)
                    for m in range(NA_WR // 2):
                        even = slab_ref[head, int(idx[typ, rl, 2 * m])]
                        odd = slab_ref[head, int(idx[typ, rl, 2 * m + 1])]
                        bias_ref[head, typ, rows, m * LANES:(m + 1) * LANES] = jnp.where(lo, even, odd)
                    if NA_WR % 2:
                        tail = slab_ref[head, int(idx[typ, rl, NA_WR - 1])]
                        bias_ref[head, typ, rows, (NA_WR - 1) * GRID_W:NA_NLOC] = tail[:, 0:GRID_W]

    def masked_queries(head):
        blk, half = divmod(head, 2)
        qs = q_ref[0, :, blk * LANES:(blk + 1) * LANES] * ATT_SCALE
        zero = jnp.zeros_like(qs)
        return jnp.where(lo, qs, zero) if half == 0 else jnp.where(lo, zero, qs)

    def cols(head):
        blk = head // 2
        return slice(blk * LANES, (blk + 1) * LANES)

    def head_tasks(scores, finish):
        outs = {}

        def done(head, s):
            outs[head] = finish(head, s)
            if head % 2:
                o_ref[0, :, cols(head)] = jnp.where(lo, outs[head - 1], outs[head]).astype(o_ref.dtype)
        return [(functools.partial(scores, head), functools.partial(done, head)) for head in range(NA_HEADS)]

    def context_tile():
        def scores(head):
            return _dot_nt(masked_queries(head), k_ref[0, 0:CTX, cols(head)])

        def finish(head, s):
            p = jnp.exp(s - jnp.max(s, axis=-1, keepdims=True))
            l = jnp.sum(p, axis=-1, keepdims=True)
            return jnp.dot(p.astype(BF16), v_ref[0, 0:CTX, cols(head)], preferred_element_type=F32) / l
        return head_tasks(scores, finish)

    def latent_tile():
        rb = t - 1
        ws = jnp.clip(NA_RB * rb - NA_WIN_ROWS // 2, 0, ROWS - NA_WR)
        window = pl.ds(pl.multiple_of(CTX + ws * GRID_W, GRID_W), NA_NLOC)
        typ = jnp.where(rb == 0, 0, jnp.where(rb == ROWS // NA_RB - 1, 2, 1))

        def scores(head):
            qm = masked_queries(head)
            return (_dot_nt(qm, k_ref[0, window, cols(head)]) + bias_ref[head, typ],
                    _dot_nt(qm, k_ref[0, 0:CTX, cols(head)]))

        def finish(head, s):
            s_loc, s_ctx = s
            m = jnp.maximum(jnp.max(s_loc, axis=-1, keepdims=True), jnp.max(s_ctx, axis=-1, keepdims=True))
            p_loc = jnp.exp(s_loc - m)
            p_ctx = jnp.exp(s_ctx - m)
            l = jnp.sum(p_loc, axis=-1, keepdims=True) + jnp.sum(p_ctx, axis=-1, keepdims=True)
            o = (jnp.dot(p_loc.astype(BF16), v_ref[0, window, cols(head)], preferred_element_type=F32)
                 + jnp.dot(p_ctx.astype(BF16), v_ref[0, 0:CTX, cols(head)], preferred_element_type=F32))
            return o / l
        return head_tasks(scores, finish)

    return build_bias, context_tile, latent_tile


def _rope_tables():
    t = np.arange(SEQ)
    pos = np.stack([t // GRID_W, t % GRID_W]).astype(np.float32)
    n_freq = HD // 4
    inv_freq = (ROPE_THETA ** (-np.arange(n_freq, dtype=np.float32) / n_freq)).astype(np.float32)
    ang = pos[:, :, None] * inv_freq
    cos, sin = np.cos(ang), np.sin(ang)
    cos_h = np.concatenate([cos[0], cos[0], cos[1], cos[1]], axis=-1)
    sin_h = np.concatenate([-sin[0], sin[0], -sin[1], sin[1]], axis=-1)
    cos_all = np.concatenate([np.ones((CTX, HD), np.float32), cos_h], axis=0)
    sin_all = np.concatenate([np.zeros((CTX, HD), np.float32), sin_h], axis=0)
    return (np.tile(cos_all, (1, 2)).astype(np.float32), np.tile(sin_all, (1, 2)).astype(np.float32))


def _rope(x, cos, sin):
    quarter = HD // 4
    first = lax.broadcasted_iota(jnp.int32, x.shape, 1) % (2 * quarter) < quarter
    partner = jnp.where(first, pltpu.roll(x, LANES - quarter, axis=1), pltpu.roll(x, quarter, axis=1))
    return x * cos + partner * sin


def _gqa_stages(q_ref, k_ref, v_ref, cos_ref, sin_ref, qg_ref, kg_ref, o_ref, kn_ref):
    t = pl.program_id(1)
    lo = _lo_mask((1, LANES))

    def prepare_keys():
        def prep(i, carry):
            rows = pl.ds(pl.multiple_of(i * TM, TM), TM)
            kx = _head_rms(k_ref[0, rows, :].astype(F32), kg_ref[...])
            kn_ref[rows, :] = _rope(kx, cos_ref[rows, :], sin_ref[rows, :]).astype(BF16)
            return carry
        lax.fori_loop(0, N_TILES, prep, 0)

    def attend(n_keys):
        rows = pl.ds(pl.multiple_of(t * TM, TM), TM)
        cos = cos_ref[rows, :]
        sin = sin_ref[rows, :]
        kn = kn_ref[0:n_keys, :]
        vv = v_ref[0, 0:n_keys, :]

        qn = []
        for j in range(GQA_HEADS // 2):
            qx = _head_rms(q_ref[0, :, j * LANES:(j + 1) * LANES].astype(F32), qg_ref[...])
            qn.append((_rope(qx, cos, sin) * ATT_SCALE).astype(BF16))
        zero = jnp.zeros_like(qn[0])

        def scores(i):
            j, h = divmod(i, 2)
            return _dot_nt(jnp.where(lo, qn[j], zero) if h == 0 else jnp.where(lo, zero, qn[j]), kn)

        outs = {}

        def done(i, s):
            p = jnp.exp(s - jnp.max(s, axis=-1, keepdims=True))
            l = jnp.sum(p, axis=-1, keepdims=True)
            outs[i] = jnp.dot(p.astype(BF16), vv, preferred_element_type=F32) / l
            if i % 2:
                j = i // 2
                o_ref[0, :, j * LANES:(j + 1) * LANES] = jnp.where(lo, outs[i - 1], outs[i]).astype(o_ref.dtype)
        return [(functools.partial(scores, i), functools.partial(done, i)) for i in range(GQA_HEADS)]

    return prepare_keys, functools.partial(attend, CTX), functools.partial(attend, T_ALL)


def _run_heads(tasks, ahead):
    pending = [issue() for issue, _ in tasks[:ahead]]
    for i, (_, finish) in enumerate(tasks):
        s = pending.pop(0)
        if i + ahead < len(tasks):
            pending.append(tasks[i + ahead][0]())
        finish(s)


def _attention_kernel(idx, qa_ref, ka_ref, va_ref, slab_ref, qb_ref, kb_ref, vb_ref, cos_ref, sin_ref, qg_ref, kg_ref,
                      oa_ref, ob_ref, bias_ref, kn_ref):
    t = pl.program_id(1)
    na_bias, na_context, na_latent = _na_stages(idx, qa_ref, ka_ref, va_ref, slab_ref, oa_ref, bias_ref)
    gqa_keys, gqa_context, gqa_latent = _gqa_stages(qb_ref, kb_ref, vb_ref, cos_ref, sin_ref, qg_ref, kg_ref,
                                                    ob_ref, kn_ref)

    @pl.when((pl.program_id(0) == 0) & (t == 0))
    def _():
        na_bias()

    @pl.when(t == 0)
    def _():
        gqa_keys()
        _run_heads(na_context() + gqa_context(), 2)

    @pl.when(t > 0)
    def _():
        _run_heads([task for pair in zip(na_latent(), gqa_latent()) for task in pair], 2)


def _mixers_ab(ya, yb, slabs, cos, sin, qg, kg):
    b = ya.shape[0]
    kcol = WB // LANES
    tile = lambda i, t: (i, t, 0)
    full = lambda i, t: (0, 0)
    return pl.pallas_call(
        functools.partial(_attention_kernel, _na_slab_index()),
        grid=(b, N_TILES),
        in_specs=[pl.BlockSpec((1, TM, WA), tile),
                  pl.BlockSpec((1, T_ALL, WA), lambda i, t: (i, 0, 1)),
                  pl.BlockSpec((1, T_ALL, WA), lambda i, t: (i, 0, 2)),
                  pl.BlockSpec((NA_HEADS, 2 * NA_WIN_ROWS, GRID_W, LANES), lambda i, t: (0, 0, 0, 0)),
                  pl.BlockSpec((1, TM, WB), tile),
                  pl.BlockSpec((1, T_ALL, LANES), lambda i, t: (i, 0, kcol)),
                  pl.BlockSpec((1, T_ALL, LANES), lambda i, t: (i, 0, kcol + 1)),
                  pl.BlockSpec((T_ALL, LANES), full),
                  pl.BlockSpec((T_ALL, LANES), full),
                  pl.BlockSpec((1, LANES), full),
                  pl.BlockSpec((1, LANES), full)],
        out_specs=[pl.BlockSpec((1, TM, WA), tile),
                   pl.BlockSpec((1, TM, WB), tile)],
        out_shape=[jax.ShapeDtypeStruct((b, T_ALL, WA), BF16),
                   jax.ShapeDtypeStruct((b, T_ALL, WB), BF16)],
        scratch_shapes=[pltpu.VMEM((NA_HEADS, 3, NA_RB * GRID_W, NA_NLOC), F32),
                        pltpu.VMEM((T_ALL, LANES), BF16)],
        compiler_params=_params("arbitrary", "arbitrary"),
        name="mixers_ab_attention",
    )(ya, ya, ya, slabs, yb, yb, yb, cos, sin, qg, kg)


def _hgrn_consts():
    c, sc = HG_CHUNK, HG_SUB
    t = np.arange(c)
    out = []
    for reverse in (False, True):
        if not reverse:
            cum = t[None, :] <= t[:, None]
            through = (t[None, :] // sc) <= (t[:, None] // sc)
        else:
            cum = t[None, :] >= t[:, None]
            through = (t[None, :] // sc) >= (t[:, None] // sc)
        out.append(np.concatenate([cum, through], axis=0).astype(np.float32))
    return np.stack(out)


def _forget_gate(z, lb):
    ez = jnp.exp(-jnp.abs(z))
    big = 1.0 / (1.0 + ez)
    small = ez * big
    pos = z >= 0.0
    lb_floor = jnp.maximum(lb, LB_FLOOR)
    f = lb_floor + (1.0 - lb) * jnp.where(pos, big, small)
    one_minus_f = (1.0 - lb) * jnp.where(pos, small, big) - (lb_floor - lb)
    return jnp.log(f), one_minus_f


def _hgrn_kernel(zf_ref, zb_ref, v_ref, q_ref, g_ref, lb_ref, gn_ref, tri_ref, ones_ref,
                 o_ref, acc_ref, st_ref, d_ref, b_ref):
    c, sc = HG_CHUNK, HG_SUB
    n_sub = c // sc
    n_pair = WC // LANES
    acc_ref[...] = jnp.zeros_like(acc_ref)
    st_ref[...] = jnp.zeros_like(st_ref)
    sub_row = lax.broadcasted_iota(jnp.int32, (sc, 1), 0)
    blk = (lax.broadcasted_iota(jnp.int32, (LANES, LANES), 0) // HD
           == lax.broadcasted_iota(jnp.int32, (LANES, LANES), 1) // HD)
    own_head = (lax.broadcasted_iota(jnp.int32, (HG_HEADS * c, WC), 0) // c
                == lax.broadcasted_iota(jnp.int32, (HG_HEADS * c, WC), 1) // HD)
    q_sub = lax.broadcasted_iota(jnp.int32, (c, WC), 0) // sc
    k_sub = (lax.broadcasted_iota(jnp.int32, (c, WC), 1) % c) // sc
    k_lane = lax.broadcasted_iota(jnp.int32, (1, WC), 1) % c

    z_refs = (zf_ref, zb_ref)
    n_ctx = CTX // c
    n_all = T_ALL // c

    def chunk_of(d, n):
        return n if d == 0 else jnp.where(n < n_ctx, n_ctx - 1 - n, n_all + n_ctx - 1 - n)

    def rows_of(d, n):
        return pl.ds(pl.multiple_of(chunk_of(d, n) * c, c), c)

    def gates(d, n, slot):
        g, kk = _forget_gate(z_refs[d][0, rows_of(d, n), :], lb_ref[d:d + 1, :])
        cums = _dot_01(tri_ref[d], g * LOG2E)
        b_ref[d, slot, 0] = cums[0:c]
        b_ref[d, slot, 1] = kk
        b_ref[d, slot, 2] = cums[c:2 * c]

    def state_and_far(d, n, slot):
        reverse = d == 1
        rows = rows_of(d, n)
        b = b_ref[d, slot, 0]
        kk = b_ref[d, slot, 1]
        e = b_ref[d, slot, 2]
        q = q_ref[0, rows, :] * ATT_SCALE
        vb = v_ref[0, rows, :].astype(BF16)
        last = c - 1 if not reverse else 0
        b_last = b_ref[d, slot, 0, last:last + 1, :]

        qe = (q * jnp.exp2(b)).astype(BF16)
        kdec = (kk * jnp.exp2(b_last - b)).astype(BF16)
        o_parts = []
        for p in range(n_pair):
            ln = slice(p * LANES, (p + 1) * LANES)
            st = st_ref[d, p]
            o_parts.append(_dot_nt(qe[:, ln], st.astype(BF16)))
            upd = _dot_tn(vb[:, ln], kdec[:, ln])
            st_ref[d, p] = jnp.where(blk, st * jnp.exp2(b_last[:, ln]) + upd, 0.0)
        o = jnp.concatenate(o_parts, axis=1)

        kx = (kk * jnp.exp2(e - b)).astype(BF16)
        keys = jnp.where(own_head, jnp.concatenate([kx] * HG_HEADS, axis=0), jnp.zeros((), BF16))
        vals = jnp.where(own_head, jnp.concatenate([vb] * HG_HEADS, axis=0), jnp.zeros((), BF16))
        q_stack = []
        for j in range(n_sub - 1):
            jj = j if not reverse else n_sub - 1 - j
            edge = (jj + 1) * sc - 1 if not reverse else jj * sc
            e_j = b_ref[d, slot, 0, edge:edge + 1, :]
            q_stack.append((q * jnp.exp2(jnp.minimum(b - e_j, 0.0))).astype(BF16))
        res = _dot_nt(jnp.concatenate(q_stack, axis=0), keys)
        att = jnp.zeros((c, WC), F32)
        for j in range(n_sub - 1):
            jj = j if not reverse else n_sub - 1 - j
            later = q_sub > jj if not reverse else q_sub < jj
            att = att + jnp.where((k_sub == jj) & later, res[j * c:(j + 1) * c, :], 0.0)
        return dict(rows=rows, b=b, q=q, o=o, att=att, vals=vals)

    def near_products(d, slot, w):
        dk = d * 2 + slot % 2
        reverse = d == 1
        for i in range(n_sub):
            bi = w['b'][i * sc:(i + 1) * sc, :]
            qi = w['q'][i * sc:(i + 1) * sc, :]
            for s in range(sc):
                r = i * sc + s
                p = jnp.exp2(bi - b_ref[d, slot, 0, r:r + 1, :]) * (qi * b_ref[d, slot, 1, r:r + 1, :])
                keep = sub_row >= s if not reverse else sub_row <= s
                d_ref[dk, r * sc:(r + 1) * sc, :] = jnp.where(keep, p, 0.0).astype(BF16)
        w['sums'] = jnp.dot(d_ref[dk], ones_ref[...], preferred_element_type=F32)

    def finish(d, w):
        near = []
        for i in range(n_sub):
            a = jnp.zeros((sc, WC), F32)
            for s in range(sc):
                r = i * sc + s
                a = jnp.where(k_lane == r, w['sums'][r * sc:(r + 1) * sc, :], a)
            near.append(a)
        att = w['att'] + jnp.concatenate(near, axis=0)
        acc_ref[w['rows'], :] += w['o'] + jnp.dot(att.astype(BF16), w['vals'], preferred_element_type=F32)

    def two_steps(n0, slots, next_slots):
        streams = [(k, d) for k in range(2) for d in range(2)]
        for k, d in streams:
            gates(d, jnp.minimum(n0 + 2 + k, n_all - 1), next_slots[k])
        work = {(k, d): state_and_far(d, n0 + k, slots[k]) for k, d in streams}
        for k, d in streams:
            near_products(d, slots[k], work[k, d])
        for k, d in streams:
            finish(d, work[k, d])

    for k in range(2):
        for d in range(2):
            gates(d, k, k)

    def body(m, carry):
        two_steps(4 * m, (0, 1), (2, 3))
        two_steps(4 * m + 2, (2, 3), (0, 1))
        return carry
    lax.fori_loop(0, n_all // 4, body, 0)

    def readout(i, carry):
        rows = pl.ds(pl.multiple_of(i * TM, TM), TM)
        gate = _silu(g_ref[0, rows, :])
        for p in range(n_pair):
            ln = slice(p * LANES, (p + 1) * LANES)
            y = _head_rms(acc_ref[rows, ln], gn_ref[...]) * gate[:, ln]
            o_ref[0, rows, ln] = y.astype(o_ref.dtype)
        return carry
    lax.fori_loop(0, N_TILES, readout, 0)


def _mixer_c(yc, lb, gn, tri, ones):
    b = yc.shape[0]
    c = HG_CHUNK
    col = lambda j: (lambda i: (i, 0, j))
    return pl.pallas_call(
        _hgrn_kernel,
        grid=(b,),
        in_specs=[pl.BlockSpec((1, T_ALL, WC), col(0)),
                  pl.BlockSpec((1, T_ALL, WC), col(1)),
                  pl.BlockSpec((1, T_ALL, WC), col(2)),
                  pl.BlockSpec((1, T_ALL, WC), col(3)),
                  pl.BlockSpec((1, T_ALL, WC), col(4)),
                  pl.BlockSpec((2, WC), lambda i: (0, 0)),
                  pl.BlockSpec((1, LANES), lambda i: (0, 0)),
                  pl.BlockSpec((2, 2 * c, c), lambda i: (0, 0, 0)),
                  pl.BlockSpec((WC, WC), lambda i: (0, 0))],
        out_specs=pl.BlockSpec((1, T_ALL, WC), lambda i: (i, 0, 0)),
        out_shape=jax.ShapeDtypeStruct((b, T_ALL, WC), BF16),
        scratch_shapes=[pltpu.VMEM((T_ALL, WC), F32),
                        pltpu.VMEM((2, WC // LANES, LANES, LANES), F32),
                        pltpu.VMEM((4, c * HG_SUB, WC), BF16),
                        pltpu.VMEM((2, 4, 3, c, WC), F32)],
        compiler_params=_params("parallel"),
        name="mixer_c_hgrn2",
    )(yc, yc, yc, yc, yc, lb, gn, tri, ones)


def _outproj_ffn_kernel(n_sub, t0, ap_ref, a_ref, an_ref, bp_ref, b_ref, bn_ref, cp_ref, c_ref, cn_ref,
                        xp_ref, x_ref, xn_ref, mod_ref, wout_ref, g1_ref, b1_ref, wup_ref, cw_ref, cb_ref, wdn_ref,
                        g2_ref, b2_ref, o_ref):
    step = pl.program_id(1)
    n_e = HALO + TM + HALO
    n_chunk = D_FF // FF_CHUNK

    def with_halo(prev_ref, ref, next_ref, j, halo):
        lo, hi = j * TM - halo, (j + 1) * TM + halo
        parts = [prev_ref[0]] if j == 0 else []
        parts.append(ref[0, max(lo, 0):min(hi, n_sub * TM), :])
        if j == n_sub - 1:
            parts.append(next_ref[0])
        return parts[0] if len(parts) == 1 else jnp.concatenate(parts, axis=0)

    def prologue(j):
        tile = step * n_sub + j + t0
        mods = mod_ref[0, 1] if j > 0 else jnp.where(tile == 0, mod_ref[0, 0], mod_ref[0, 1])
        mix = jnp.concatenate([with_halo(p, m, n, j, MIX_HALO) for p, m, n in
                               ((ap_ref, a_ref, an_ref), (bp_ref, b_ref, bn_ref), (cp_ref, c_ref, cn_ref))], axis=1)
        y = jnp.dot(mix, wout_ref[...], preferred_element_type=F32)[MIX_HALO - HALO:MIX_HALO + TM + HALO, :]
        xe = with_halo(xp_ref, x_ref, xn_ref, j, HALO)
        x1e = _ln(ALPHA * xe + mods[2:3, :] * y) * g1_ref[...] + b1_ref[...]
        h = _ln(x1e) * (1.0 + mods[4:5, :]) + mods[3:4, :]
        r = lax.broadcasted_iota(jnp.int32, (n_e, 1), 0)
        keep = ((r >= HALO) | (tile >= 2)) & ((r < HALO + TM) | ((tile >= 1) & (tile < N_TILES - 1)))
        return dict(h=jnp.where(keep, h, 0.0).astype(BF16), x1=x1e[HALO:HALO + TM, :], gate=mods[5:6, :])

    def epilogue(j, w):
        z = _ln(ALPHA * w['x1'] + w['gate'] * w['acc'])
        o_ref[0, j * TM:(j + 1) * TM, :] = z * g2_ref[...] + b2_ref[...]

    def up(h, c, base):
        return jnp.dot(h, wup_ref[:, base + c * FF_CHUNK:base + (c + 1) * FF_CHUNK], preferred_element_type=F32)

    def conv(u, c, base):
        cols = slice(base + c * FF_CHUNK, base + (c + 1) * FF_CHUNK)
        u = (pltpu.roll(u, 1, axis=0) * cw_ref[0:1, cols] + u * cw_ref[1:2, cols]
             + pltpu.roll(u, n_e - 1, axis=0) * cw_ref[2:3, cols])
        return u[HALO:HALO + TM, :] + cb_ref[:, cols]

    def hidden_chunks(w, hooks):
        ahead = 2
        acc = jnp.zeros((TM, D), F32)
        pending = [(up(w['h'], c, 0), up(w['h'], c, D_FF)) for c in range(ahead)]
        for c in range(n_chunk):
            u_gate, u_val = pending.pop(0)
            if c + ahead < n_chunk:
                pending.append((up(w['h'], c + ahead, 0), up(w['h'], c + ahead, D_FF)))
            if c in hooks:
                hooks[c]()
            a = (_silu(conv(u_gate, c, 0)) * conv(u_val, c, D_FF)).astype(BF16)
            acc = acc + jnp.dot(a, wdn_ref[c * FF_CHUNK:(c + 1) * FF_CHUNK, :], preferred_element_type=F32)
        w['acc'] = acc

    work = [None] * n_sub
    work[0] = prologue(0)
    for j in range(n_sub):
        hooks = {}
        if j > 0:
            hooks[0] = functools.partial(epilogue, j - 1, work[j - 1])
        if j + 1 < n_sub:
            hooks[n_chunk // 2] = functools.partial(lambda k: work.__setitem__(k, prologue(k)), j + 1)
        hidden_chunks(work[j], hooks)
    epilogue(n_sub - 1, work[n_sub - 1])


def _outproj_ffn(oa, ob, oc, xs, modsel, w_out_p, ln1_g, ln1_b, w_up, conv_w, conv_b, w_down, ln2_g, ln2_b, t0):
    b = xs.shape[0]
    n_sub = FFN_TILES_PER_STEP if t0 == 0 else 1
    assert (N_TILES - t0) % n_sub == 0
    rows = n_sub * TM
    full = lambda i, s: (0, 0)

    def halos(width, halo):
        per, last = rows // halo, T_ALL // halo - 1
        first = t0 * TM // halo
        return (pl.BlockSpec((1, halo, width), lambda i, s: (i, jnp.maximum(first + s * per - 1, first), 0)),
                pl.BlockSpec((1, rows, width), lambda i, s: (i, s + t0, 0)),
                pl.BlockSpec((1, halo, width), lambda i, s: (i, jnp.minimum(first + (s + 1) * per, last), 0)))

    return pl.pallas_call(
        functools.partial(_outproj_ffn_kernel, n_sub, t0),
        grid=(b, (N_TILES - t0) // n_sub),
        in_specs=[*halos(WA, MIX_HALO), *halos(WB, MIX_HALO), *halos(WC, MIX_HALO), *halos(D, HALO),
                  pl.BlockSpec((1, 2, 8, D), lambda i, s: (i, 0, 0, 0)),
                  pl.BlockSpec((D, D), full),
                  pl.BlockSpec((1, D), full),
                  pl.BlockSpec((1, D), full),
                  pl.BlockSpec((D, 2 * D_FF), full),
                  pl.BlockSpec((3, 2 * D_FF), full),
                  pl.BlockSpec((1, 2 * D_FF), full),
                  pl.BlockSpec((D_FF, D), full),
                  pl.BlockSpec((1, D), full),
                  pl.BlockSpec((1, D), full)],
        out_specs=pl.BlockSpec((1, rows, D), lambda i, s: (i, s, 0)),
        out_shape=jax.ShapeDtypeStruct((b, T_ALL - t0 * TM, D), F32),
        compiler_params=_params("parallel", "arbitrary"),
        name="outproj_ffn",
    )(oa, oa, oa, ob, ob, ob, oc, oc, oc, xs, xs, xs, modsel, w_out_p, ln1_g, ln1_b,
      w_up, conv_w, conv_b, w_down, ln2_g, ln2_b)


def _in_col_perm():
    off, o = {}, 0
    for name, width in (('a_k', WA), ('a_v', WA), ('b_k', WKV), ('b_v', WKV), ('c_f', WC), ('c_b', WC),
                        ('c_i', WC), ('a_q', WA), ('b_q', WB), ('c_q', WC), ('c_g', WC)):
        off[name] = np.arange(o, o + width)
        o += width
    bq = off['b_q'].reshape(GQA_HEADS, HD)[list(GQA_ORDER)].reshape(-1)
    return np.concatenate([off['a_q'], off['a_k'], off['a_v'], bq, off['b_k'], off['b_v'],
                           off['c_f'], off['c_b'], off['c_i'], off['c_q'], off['c_g']])


def _out_row_perm():
    ob = (WA + np.arange(WB)).reshape(GQA_HEADS, HD)[list(GQA_ORDER)].reshape(-1)
    return np.concatenate([np.arange(WA), ob, WA + WB + np.arange(WC)])


def _take_runs(w, perm, axis):
    cuts = [0] + [i for i in range(1, len(perm)) if perm[i] != perm[i - 1] + 1] + [len(perm)]
    parts = [lax.slice_in_dim(w, int(perm[a]), int(perm[z - 1]) + 1, axis=axis) for a, z in zip(cuts[:-1], cuts[1:])]
    return jnp.concatenate(parts, axis=axis)


def kernel(x, c, ctx, c_ctx, w_ada, b_ada, w_in, w_out, na_rpb, q_norm, k_norm, hg_lower_bound, hg_norm,
           ln1_g, ln1_b, w_up, conv_w, conv_b, w_down, ln2_g, ln2_b):
    b = x.shape[0]
    assert x.shape == (b, SEQ, D) and ctx.shape == (b, CTX, D)
    n_rows = -(-(b + 1) // 8) * 8
    s_in = jnp.concatenate([c, c_ctx[None, :], jnp.zeros((n_rows - b - 1, D), F32)], axis=0)
    mods = _modulation(s_in, w_ada, b_ada).reshape(DEPTH, n_rows, N_MOD, D)
    lbs = _lower_bounds(hg_lower_bound).reshape(DEPTH, 2, WC)

    cos, sin = (jnp.asarray(a) for a in _rope_tables())
    tri = jnp.asarray(_hgrn_consts(), BF16)
    ones = jnp.asarray(np.kron(np.eye(HG_HEADS), np.ones((HD, HD))), BF16)
    in_perm = _in_col_perm()
    out_perm = _out_row_perm()
    tile2 = lambda g: jnp.tile(g.astype(F32), 2).reshape(1, LANES)

    xs = jnp.concatenate([ctx, x], axis=1)
    for l in range(DEPTH):
        last = l == DEPTH - 1
        t0 = 1 if last else 0
        lat = mods[l, :b]
        cx = jnp.broadcast_to(mods[l, b][None], lat.shape)
        modsel = jnp.pad(jnp.stack([cx, lat], axis=1), ((0, 0), (0, 0), (0, 8 - N_MOD), (0, 0)))
        ya, yb, yc = _in_projection(xs, modsel, _take_runs(w_in[l], in_perm, 1).astype(BF16))
        oa, ob = _mixers_ab(ya, yb, _na_slabs(na_rpb[l]), cos, sin, tile2(q_norm[l]), tile2(k_norm[l]))
        oc = _mixer_c(yc, lbs[l], tile2(hg_norm[l]), tri, ones)
        xs = _outproj_ffn(oa, ob, oc, xs, modsel, _take_runs(w_out[l], out_perm, 0).astype(BF16),
                          ln1_g[l].reshape(1, D), ln1_b[l].reshape(1, D), w_up[l].astype(BF16), conv_w[l],
                          conv_b[l].reshape(1, 2 * D_FF), w_down[l].astype(BF16),
                          ln2_g[l].reshape(1, D), ln2_b[l].reshape(1, D), t0)
    return xs
```

```python
import functools

import numpy as np
import jax
import jax.numpy as jnp
from jax import lax
from jax.experimental import pallas as pl
from jax.experimental.pallas import tpu as pltpu

D = 1024
DEPTH = 4
GRID_W = 64
CTX = 256
SEQ = 2048
ROWS = SEQ // GRID_W
T_ALL = CTX + SEQ
HD = 64
NA_HEADS = 6
NA_WIN_ROWS = 8
NA_WIN_COLS = 16
GQA_HEADS = 6
GQA_KV = 2
HG_HEADS = 4
HG_CHUNK = 64
HG_SUB = 16
ROPE_THETA = 10000.0
LB_FLOOR = 1e-30
WA = NA_HEADS * HD
WB = GQA_HEADS * HD
WKV = GQA_KV * HD
WC = HG_HEADS * HD
IN_COLS = 3072
D_FF = 2816
N_MOD = 6
ALPHA = (2.0 * DEPTH) ** 0.25
LN_EPS = 1e-6
RMS_EPS = 1e-6
ATT_SCALE = HD ** -0.5
LOG2E = 1.4426950408889634
SOFTMAX_SCALE = ATT_SCALE * LOG2E
NEG = -1e30

LANES = 128
TM = 256
N_TILES = T_ALL // TM
NA_RB = 4
NA_WR = NA_WIN_ROWS + NA_RB - 1
NA_NLOC = NA_WR * GRID_W
FF_CHUNK = 256
FFN_TILES_PER_STEP = 3
IN_TILES = 3
HALO = 8
MIX_HALO = 16
VMEM_LIMIT = 56 * 1024 * 1024

YA_COLS = 3 * WA
YB_COLS = WB + 2 * WKV
YC_COLS = 5 * WC
GQA_ORDER = (0, 3, 1, 4, 2, 5)

HI = lax.Precision.HIGHEST
F32 = jnp.float32
BF16 = jnp.bfloat16


def _params(*sem):
    return pltpu.CompilerParams(dimension_semantics=sem, vmem_limit_bytes=VMEM_LIMIT)


def _dot_nt(a, b):
    return lax.dot_general(a, b, (((1,), (1,)), ((), ())), preferred_element_type=F32)


def _dot_tn(a, b):
    return lax.dot_general(a, b, (((0,), (0,)), ((), ())), preferred_element_type=F32)


def _ln(x):
    mu = jnp.mean(x, axis=-1, keepdims=True)
    xc = x - mu
    var = jnp.mean(xc * xc, axis=-1, keepdims=True)
    return xc * lax.rsqrt(var + LN_EPS)


def _dot_01(m, x):
    hi = x.astype(BF16)
    rest = x - hi.astype(F32)
    mid = rest.astype(BF16)
    lo = (rest - mid.astype(F32)).astype(BF16)
    return (jnp.dot(m, hi, preferred_element_type=F32) + jnp.dot(m, mid, preferred_element_type=F32)
            + jnp.dot(m, lo, preferred_element_type=F32))


def _silu(x):
    return x * jax.nn.sigmoid(x)


def _lo_mask(shape):
    return lax.broadcasted_iota(jnp.int32, shape, len(shape) - 1) % LANES < HD


def _head_rms(x, gain):
    lo = _lo_mask(x.shape)
    x2 = x * x
    s_lo = jnp.sum(jnp.where(lo, x2, 0.0), axis=-1, keepdims=True)
    s_hi = jnp.sum(jnp.where(lo, 0.0, x2), axis=-1, keepdims=True)
    ms = jnp.where(lo, s_lo, s_hi) * (1.0 / HD)
    return x * lax.rsqrt(ms + RMS_EPS) * gain


def _mod_kernel(s_ref, w_ref, b_ref, o_ref):
    s = _silu(s_ref[...])
    o_ref[0] = jnp.dot(s, w_ref[0], precision=HI, preferred_element_type=F32) + b_ref[0]


def _modulation(s_in, w_ada, b_ada):
    r = s_in.shape[0]
    cb = 1536
    return pl.pallas_call(
        _mod_kernel,
        grid=(DEPTH, N_MOD * D // cb),
        in_specs=[pl.BlockSpec((r, D), lambda l, j: (0, 0)),
                  pl.BlockSpec((1, D, cb), lambda l, j: (l, 0, j)),
                  pl.BlockSpec((1, 1, cb), lambda l, j: (l, 0, j))],
        out_specs=pl.BlockSpec((1, r, cb), lambda l, j: (l, 0, j)),
        out_shape=jax.ShapeDtypeStruct((DEPTH, r, N_MOD * D), F32),
        compiler_params=_params("arbitrary", "arbitrary"),
        name="adaln_mod",
    )(s_in, w_ada, b_ada.reshape(DEPTH, 1, N_MOD * D))


def _lb_kernel(p_ref, o_ref):
    p = p_ref[...]
    e = jnp.exp(p - jnp.max(p, axis=0, keepdims=True))
    sm = e / jnp.sum(e, axis=0, keepdims=True)
    acc = sm[0:1]
    rows = [acc - sm[0:1]]
    for l in range(1, DEPTH):
        acc = acc + sm[l:l + 1]
        rows.append(acc - sm[0:1])
    o_ref[...] = jnp.concatenate(rows, axis=0)


def _lower_bounds(hg_lower_bound):
    p = hg_lower_bound.reshape(DEPTH, 2 * WC).astype(F32)
    return pl.pallas_call(
        _lb_kernel,
        out_shape=jax.ShapeDtypeStruct((DEPTH, 2 * WC), F32),
        name="hgrn_lower_bounds",
    )(p)


def _inproj_kernel(x_ref, mod_ref, w_ref, ya_ref, yb_ref, yc_ref):
    step = pl.program_id(1)
    hs = []
    for j in range(IN_TILES):
        mods = mod_ref[0, 1] if j > 0 else jnp.where(step == 0, mod_ref[0, 0], mod_ref[0, 1])
        y = _ln(x_ref[0, j * TM:(j + 1) * TM, :])
        hs.append((y * (1.0 + mods[1:2, :]) + mods[0:1, :]).astype(BF16))
    for j, h in enumerate(hs):
        rows = slice(j * TM, (j + 1) * TM)
        out = jnp.dot(h, w_ref[...], preferred_element_type=F32)
        ya_ref[0, rows, :] = out[:, :YA_COLS].astype(BF16)
        yb_ref[0, rows, :] = out[:, YA_COLS:YA_COLS + YB_COLS].astype(BF16)
        yc_ref[0, rows, :] = out[:, YA_COLS + YB_COLS:]


def _in_projection(xs, modsel, w_in_p):
    b = xs.shape[0]
    row = lambda i, t: (i, t, 0)
    return pl.pallas_call(
        _inproj_kernel,
        grid=(b, N_TILES // IN_TILES),
        in_specs=[pl.BlockSpec((1, IN_TILES * TM, D), row),
                  pl.BlockSpec((1, 2, 8, D), lambda i, t: (i, 0, 0, 0)),
                  pl.BlockSpec((D, IN_COLS), lambda i, t: (0, 0))],
        out_specs=[pl.BlockSpec((1, IN_TILES * TM, YA_COLS), row),
                   pl.BlockSpec((1, IN_TILES * TM, YB_COLS), row),
                   pl.BlockSpec((1, IN_TILES * TM, YC_COLS), row)],
        out_shape=[jax.ShapeDtypeStruct((b, T_ALL, YA_COLS), BF16),
                   jax.ShapeDtypeStruct((b, T_ALL, YB_COLS), BF16),
                   jax.ShapeDtypeStruct((b, T_ALL, YC_COLS), F32)],
        compiler_params=_params("parallel", "arbitrary"),
        name="in_projection",
    )(xs, modsel, w_in_p)


def _na_slab_index():
    def table(rb):
        r0 = NA_RB * rb
        ws = int(np.clip(r0 - NA_WIN_ROWS // 2, 0, ROWS - NA_WR))
        qr = r0 + np.arange(NA_RB)[:, None]
        kr = ws + np.arange(NA_WR)[None, :]
        rs = np.clip(qr - NA_WIN_ROWS // 2, 0, ROWS - NA_WIN_ROWS)
        valid = (kr >= rs) & (kr < rs + NA_WIN_ROWS)
        return np.where(valid, kr - qr + NA_WIN_ROWS - 1, 2 * NA_WIN_ROWS - 1)

    n_rb = ROWS // NA_RB
    for rb in range(2, n_rb - 1):
        assert np.array_equal(table(rb), table(1))
    return np.stack([table(0), table(1), table(n_rb - 1)])


def _na_slabs(rpb):
    qc = np.arange(GRID_W)[:, None]
    kc = np.arange(GRID_W)[None, :]
    cs = np.clip(qc - NA_WIN_COLS // 2, 0, GRID_W - NA_WIN_COLS)
    valid = (kc >= cs) & (kc < cs + NA_WIN_COLS)
    onehot = (valid[None] & ((kc - qc + NA_WIN_COLS - 1)[None] == np.arange(2 * NA_WIN_COLS - 1)[:, None, None]))
    onehot = jnp.asarray(onehot.reshape(2 * NA_WIN_COLS - 1, GRID_W * GRID_W), F32)
    slabs = jnp.einsum('hrd,dx->hrx', rpb.astype(F32), onehot, precision=HI) * LOG2E
    slabs = jnp.where(jnp.asarray(valid.reshape(-1)), slabs, NEG)
    slabs = jnp.concatenate([slabs, jnp.full((NA_HEADS, 1, GRID_W * GRID_W), NEG, F32)], axis=1)
    slabs = slabs.reshape(NA_HEADS, 2 * NA_WIN_ROWS, GRID_W, GRID_W)
    return jnp.concatenate([slabs, slabs], axis=-1)


def _na_stages(idx, q_ref, k_ref, v_ref, slab_ref, o_ref, bias_ref):
    t = pl.program_id(1)
    lo = _lo_mask((1, LANES))

    def build_bias():
        for head in range(NA_HEADS):
            for typ in range(3):
                for rl in range(NA_RB):
                    rows = slice(rl * GRID_W, (rl + 1) * GRID_W)
                    for m in range(NA_WR // 2):
                        even = slab_ref[head, int(idx[typ, rl, 2 * m])]
                        odd = slab_ref[head, int(idx[typ, rl, 2 * m + 1])]
                        bias_ref[head, typ, rows, m * LANES:(m + 1) * LANES] = jnp.where(lo, even, odd)
                    if NA_WR % 2:
                        tail = slab_ref[head, int(idx[typ, rl, NA_WR - 1])]
                        bias_ref[head, typ, rows, (NA_WR - 1) * GRID_W:NA_NLOC] = tail[:, 0:GRID_W]

    def masked_queries(head):
        blk, half = divmod(head, 2)
        qs = (q_ref[0, :, blk * LANES:(blk + 1) * LANES].astype(F32) * SOFTMAX_SCALE).astype(BF16)
        zero = jnp.zeros_like(qs)
        return jnp.where(lo, qs, zero) if half == 0 else jnp.where(lo, zero, qs)

    def cols(head):
        blk = head // 2
        return slice(blk * LANES, (blk + 1) * LANES)

    def head_tasks(scores, finish):
        outs = {}

        def done(head, s):
            outs[head] = finish(head, s)
            if head % 2:
                o_ref[0, :, cols(head)] = jnp.where(lo, outs[head - 1], outs[head]).astype(o_ref.dtype)
        return [(functools.partial(scores, head), functools.partial(done, head)) for head in range(NA_HEADS)]

    def context_tile():
        def scores(head):
            return _dot_nt(masked_queries(head), k_ref[0, 0:CTX, cols(head)])

        def finish(head, s):
            p = jnp.exp2(s - jnp.max(s, axis=-1, keepdims=True))
            l = jnp.sum(p, axis=-1, keepdims=True)
            return jnp.dot(p.astype(BF16), v_ref[0, 0:CTX, cols(head)], preferred_element_type=F32) / l
        return head_tasks(scores, finish)

    def latent_tile():
        rb = t - 1
        ws = jnp.clip(NA_RB * rb - NA_WIN_ROWS // 2, 0, ROWS - NA_WR)
        window = pl.ds(pl.multiple_of(CTX + ws * GRID_W, GRID_W), NA_NLOC)
        typ = jnp.where(rb == 0, 0, jnp.where(rb == ROWS // NA_RB - 1, 2, 1))

        def scores(head):
            qm = masked_queries(head)
            return (_dot_nt(qm, k_ref[0, window, cols(head)]) + bias_ref[head, typ],
                    _dot_nt(qm, k_ref[0, 0:CTX, cols(head)]))

        def finish(head, s):
            s_loc, s_ctx = s
            m = jnp.maximum(jnp.max(s_loc, axis=-1, keepdims=True), jnp.max(s_ctx, axis=-1, keepdims=True))
            p_loc = jnp.exp2(s_loc - m)
            p_ctx = jnp.exp2(s_ctx - m)
            l = jnp.sum(p_loc, axis=-1, keepdims=True) + jnp.sum(p_ctx, axis=-1, keepdims=True)
            o = (jnp.dot(p_loc.astype(BF16), v_ref[0, window, cols(head)], preferred_element_type=F32)
                 + jnp.dot(p_ctx.astype(BF16), v_ref[0, 0:CTX, cols(head)], preferred_element_type=F32))
            return o / l
        return head_tasks(scores, finish)

    return build_bias, context_tile, latent_tile


def _rope_tables():
    t = np.arange(SEQ)
    pos = np.stack([t // GRID_W, t % GRID_W]).astype(np.float32)
    n_freq = HD // 4
    inv_freq = (ROPE_THETA ** (-np.arange(n_freq, dtype=np.float32) / n_freq)).astype(np.float32)
    ang = pos[:, :, None] * inv_freq
    cos, sin = np.cos(ang), np.sin(ang)
    cos_h = np.concatenate([cos[0], cos[0], cos[1], cos[1]], axis=-1)
    sin_h = np.concatenate([-sin[0], sin[0], -sin[1], sin[1]], axis=-1)
    cos_all = np.concatenate([np.ones((CTX, HD), np.float32), cos_h], axis=0)
    sin_all = np.concatenate([np.zeros((CTX, HD), np.float32), sin_h], axis=0)
    return (np.tile(cos_all, (1, 2)).astype(np.float32), np.tile(sin_all, (1, 2)).astype(np.float32))


def _rope(x, cos, sin):
    quarter = HD // 4
    first = lax.broadcasted_iota(jnp.int32, x.shape, 1) % (2 * quarter) < quarter
    partner = jnp.where(first, pltpu.roll(x, LANES - quarter, axis=1), pltpu.roll(x, quarter, axis=1))
    return x * cos + partner * sin


def _gqa_stages(q_ref, k_ref, v_ref, cos_ref, sin_ref, qg_ref, kg_ref, o_ref, kn_ref):
    t = pl.program_id(1)
    lo = _lo_mask((1, LANES))

    def prepare_keys():
        def prep(i, carry):
            rows = pl.ds(pl.multiple_of(i * TM, TM), TM)
            kx = _head_rms(k_ref[0, rows, :].astype(F32), kg_ref[...])
            kn_ref[rows, :] = _rope(kx, cos_ref[rows, :], sin_ref[rows, :]).astype(BF16)
            return carry
        lax.fori_loop(0, N_TILES, prep, 0)

    def attend(n_keys):
        rows = pl.ds(pl.multiple_of(t * TM, TM), TM)
        cos = cos_ref[rows, :]
        sin = sin_ref[rows, :]
        kn = kn_ref[0:n_keys, :]
        vv = v_ref[0, 0:n_keys, :]

        qn = []
        for j in range(GQA_HEADS // 2):
            qx = _head_rms(q_ref[0, :, j * LANES:(j + 1) * LANES].astype(F32), qg_ref[...])
            qn.append((_rope(qx, cos, sin) * SOFTMAX_SCALE).astype(BF16))
        zero = jnp.zeros_like(qn[0])

        def scores(i):
            j, h = divmod(i, 2)
            return _dot_nt(jnp.where(lo, qn[j], zero) if h == 0 else jnp.where(lo, zero, qn[j]), kn)

        outs = {}

        def done(i, s):
            p = jnp.exp2(s - jnp.max(s, axis=-1, keepdims=True))
            l = jnp.sum(p, axis=-1, keepdims=True)
            outs[i] = jnp.dot(p.astype(BF16), vv, preferred_element_type=F32) / l
            if i % 2:
                j = i // 2
                o_ref[0, :, j * LANES:(j + 1) * LANES] = jnp.where(lo, outs[i - 1], outs[i]).astype(o_ref.dtype)
        return [(functools.partial(scores, i), functools.partial(done, i)) for i in range(GQA_HEADS)]

    return prepare_keys, functools.partial(attend, CTX), functools.partial(attend, T_ALL)


def _run_heads(tasks, ahead):
    pending = [issue() for issue, _ in tasks[:ahead]]
    for i, (_, finish) in enumerate(tasks):
        s = pending.pop(0)
        if i + ahead < len(tasks):
            pending.append(tasks[i + ahead][0]())
        finish(s)


def _attention_kernel(idx, qa_ref, ka_ref, va_ref, slab_ref, qb_ref, kb_ref, vb_ref, cos_ref, sin_ref, qg_ref, kg_ref,
                      oa_ref, ob_ref, bias_ref, kn_ref):
    t = pl.program_id(1)
    na_bias, na_context, na_latent = _na_stages(idx, qa_ref, ka_ref, va_ref, slab_ref, oa_ref, bias_ref)
    gqa_keys, gqa_context, gqa_latent = _gqa_stages(qb_ref, kb_ref, vb_ref, cos_ref, sin_ref, qg_ref, kg_ref,
                                                    ob_ref, kn_ref)

    @pl.when((pl.program_id(0) == 0) & (t == 0))
    def _():
        na_bias()

    @pl.when(t == 0)
    def _():
        gqa_keys()
        _run_heads(na_context() + gqa_context(), 2)

    @pl.when(t > 0)
    def _():
        _run_heads([task for pair in zip(na_latent(), gqa_latent()) for task in pair], 2)


def _mixers_ab(ya, yb, slabs, cos, sin, qg, kg):
    b = ya.shape[0]
    kcol = WB // LANES
    tile = lambda i, t: (i, t, 0)
    full = lambda i, t: (0, 0)
    return pl.pallas_call(
        functools.partial(_attention_kernel, _na_slab_index()),
        grid=(b, N_TILES),
        in_specs=[pl.BlockSpec((1, TM, WA), tile),
                  pl.BlockSpec((1, T_ALL, WA), lambda i, t: (i, 0, 1)),
                  pl.BlockSpec((1, T_ALL, WA), lambda i, t: (i, 0, 2)),
                  pl.BlockSpec((NA_HEADS, 2 * NA_WIN_ROWS, GRID_W, LANES), lambda i, t: (0, 0, 0, 0)),
                  pl.BlockSpec((1, TM, WB), tile),
                  pl.BlockSpec((1, T_ALL, LANES), lambda i, t: (i, 0, kcol)),
                  pl.BlockSpec((1, T_ALL, LANES), lambda i, t: (i, 0, kcol + 1)),
                  pl.BlockSpec((T_ALL, LANES), full),
                  pl.BlockSpec((T_ALL, LANES), full),
                  pl.BlockSpec((1, LANES), full),
                  pl.BlockSpec((1, LANES), full)],
        out_specs=[pl.BlockSpec((1, TM, WA), tile),
                   pl.BlockSpec((1, TM, WB), tile)],
        out_shape=[jax.ShapeDtypeStruct((b, T_ALL, WA), BF16),
                   jax.ShapeDtypeStruct((b, T_ALL, WB), BF16)],
        scratch_shapes=[pltpu.VMEM((NA_HEADS, 3, NA_RB * GRID_W, NA_NLOC), F32),
                        pltpu.VMEM((T_ALL, LANES), BF16)],
        compiler_params=_params("arbitrary", "arbitrary"),
        name="mixers_ab_attention",
    )(ya, ya, ya, slabs, yb, yb, yb, cos, sin, qg, kg)


def _hgrn_consts():
    c, sc = HG_CHUNK, HG_SUB
    t = np.arange(c)
    out = []
    for reverse in (False, True):
        if not reverse:
            cum = t[None, :] <= t[:, None]
            through = (t[None, :] // sc) <= (t[:, None] // sc)
        else:
            cum = t[None, :] >= t[:, None]
            through = (t[None, :] // sc) >= (t[:, None] // sc)
        out.append(np.concatenate([cum, through], axis=0).astype(np.float32))
    return np.stack(out)


def _forget_gate(z, lb):
    ez = jnp.exp(-jnp.abs(z))
    big = 1.0 / (1.0 + ez)
    small = ez * big
    pos = z >= 0.0
    lb_floor = jnp.maximum(lb, LB_FLOOR)
    f = lb_floor + (1.0 - lb) * jnp.where(pos, big, small)
    one_minus_f = (1.0 - lb) * jnp.where(pos, small, big) - (lb_floor - lb)
    return jnp.log(f), one_minus_f


def _hgrn_kernel(zf_ref, zb_ref, v_ref, q_ref, g_ref, lb_ref, gn_ref, tri_ref, ones_ref,
                 o_ref, acc_ref, st_ref, d_ref, b_ref):
    c, sc = HG_CHUNK, HG_SUB
    n_sub = c // sc
    n_pair = WC // LANES
    acc_ref[...] = jnp.zeros_like(acc_ref)
    st_ref[...] = jnp.zeros_like(st_ref)
    sub_row = lax.broadcasted_iota(jnp.int32, (sc, 1), 0)
    blk = (lax.broadcasted_iota(jnp.int32, (LANES, LANES), 0) // HD
           == lax.broadcasted_iota(jnp.int32, (LANES, LANES), 1) // HD)
    own_head = (lax.broadcasted_iota(jnp.int32, (HG_HEADS * c, WC), 0) // c
                == lax.broadcasted_iota(jnp.int32, (HG_HEADS * c, WC), 1) // HD)
    q_sub = lax.broadcasted_iota(jnp.int32, (c, WC), 0) // sc
    k_sub = (lax.broadcasted_iota(jnp.int32, (c, WC), 1) % c) // sc
    k_lane = lax.broadcasted_iota(jnp.int32, (1, WC), 1) % c

    z_refs = (zf_ref, zb_ref)
    n_ctx = CTX // c
    n_all = T_ALL // c

    def chunk_of(d, n):
        return n if d == 0 else jnp.where(n < n_ctx, n_ctx - 1 - n, n_all + n_ctx - 1 - n)

    def rows_of(d, n):
        return pl.ds(pl.multiple_of(chunk_of(d, n) * c, c), c)

    def gates(d, n, slot):
        g, kk = _forget_gate(z_refs[d][0, rows_of(d, n), :], lb_ref[d:d + 1, :])
        cums = _dot_01(tri_ref[d], g * LOG2E)
        b_ref[d, slot, 0] = cums[0:c]
        b_ref[d, slot, 1] = kk
        b_ref[d, slot, 2] = cums[c:2 * c]

    def state_and_far(d, n, slot):
        reverse = d == 1
        rows = rows_of(d, n)
        b = b_ref[d, slot, 0]
        kk = b_ref[d, slot, 1]
        e = b_ref[d, slot, 2]
        q = q_ref[0, rows, :] * ATT_SCALE
        vb = v_ref[0, rows, :].astype(BF16)
        last = c - 1 if not reverse else 0
        b_last = b_ref[d, slot, 0, last:last + 1, :]

        qe = (q * jnp.exp2(b)).astype(BF16)
        kdec = (kk * jnp.exp2(b_last - b)).astype(BF16)
        o_parts = []
        for p in range(n_pair):
            ln = slice(p * LANES, (p + 1) * LANES)
            st = st_ref[d, p]
            o_parts.append(_dot_nt(qe[:, ln], st.astype(BF16)))
            upd = _dot_tn(vb[:, ln], kdec[:, ln])
            st_ref[d, p] = jnp.where(blk, st * jnp.exp2(b_last[:, ln]) + upd, 0.0)
        o = jnp.concatenate(o_parts, axis=1)

        kx = (kk * jnp.exp2(e - b)).astype(BF16)
        keys = jnp.where(own_head, jnp.concatenate([kx] * HG_HEADS, axis=0), jnp.zeros((), BF16))
        vals = jnp.where(own_head, jnp.concatenate([vb] * HG_HEADS, axis=0), jnp.zeros((), BF16))
        q_stack = []
        for j in range(n_sub - 1):
            jj = j if not reverse else n_sub - 1 - j
            edge = (jj + 1) * sc - 1 if not reverse else jj * sc
            e_j = b_ref[d, slot, 0, edge:edge + 1, :]
            q_stack.append((q * jnp.exp2(jnp.minimum(b - e_j, 0.0))).astype(BF16))
        res = _dot_nt(jnp.concatenate(q_stack, axis=0), keys)
        att = jnp.zeros((c, WC), F32)
        for j in range(n_sub - 1):
            jj = j if not reverse else n_sub - 1 - j
            later = q_sub > jj if not reverse else q_sub < jj
            att = att + jnp.where((k_sub == jj) & later, res[j * c:(j + 1) * c, :], 0.0)
        return dict(rows=rows, b=b, q=q, o=o, att=att, vals=vals)

    def near_products(d, slot, w):
        dk = d * 2 + slot % 2
        reverse = d == 1
        for i in range(n_sub):
            bi = w['b'][i * sc:(i + 1) * sc, :]
            qi = w['q'][i * sc:(i + 1) * sc, :]
            for s in range(sc):
                r = i * sc + s
                p = jnp.exp2(bi - b_ref[d, slot, 0, r:r + 1, :]) * (qi * b_ref[d, slot, 1, r:r + 1, :])
                keep = sub_row >= s if not reverse else sub_row <= s
                d_ref[dk, r * sc:(r + 1) * sc, :] = jnp.where(keep, p, 0.0).astype(BF16)
        w['sums'] = jnp.dot(d_ref[dk], ones_ref[...], preferred_element_type=F32)

    def finish(d, w):
        near = []
        for i in range(n_sub):
            a = jnp.zeros((sc, WC), F32)
            for s in range(sc):
                r = i * sc + s
                a = jnp.where(k_lane == r, w['sums'][r * sc:(r + 1) * sc, :], a)
            near.append(a)
        att = w['att'] + jnp.concatenate(near, axis=0)
        acc_ref[w['rows'], :] += w['o'] + jnp.dot(att.astype(BF16), w['vals'], preferred_element_type=F32)

    def two_steps(n0, slots, next_slots):
        streams = [(k, d) for k in range(2) for d in range(2)]
        for k, d in streams:
            gates(d, jnp.minimum(n0 + 2 + k, n_all - 1), next_slots[k])
        work = {(k, d): state_and_far(d, n0 + k, slots[k]) for k, d in streams}
        for k, d in streams:
            near_products(d, slots[k], work[k, d])
        for k, d in streams:
            finish(d, work[k, d])

    for k in range(2):
        for d in range(2):
            gates(d, k, k)

    def body(m, carry):
        two_steps(4 * m, (0, 1), (2, 3))
        two_steps(4 * m + 2, (2, 3), (0, 1))
        return carry
    lax.fori_loop(0, n_all // 4, body, 0)

    def readout(i, carry):
        rows = pl.ds(pl.multiple_of(i * TM, TM), TM)
        gate = _silu(g_ref[0, rows, :])
        for p in range(n_pair):
            ln = slice(p * LANES, (p + 1) * LANES)
            y = _head_rms(acc_ref[rows, ln], gn_ref[...]) * gate[:, ln]
            o_ref[0, rows, ln] = y.astype(o_ref.dtype)
        return carry
    lax.fori_loop(0, N_TILES, readout, 0)


def _mixer_c(yc, lb, gn, tri, ones):
    b = yc.shape[0]
    c = HG_CHUNK
    col = lambda j: (lambda i: (i, 0, j))
    return pl.pallas_call(
        _hgrn_kernel,
        grid=(b,),
        in_specs=[pl.BlockSpec((1, T_ALL, WC), col(0)),
                  pl.BlockSpec((1, T_ALL, WC), col(1)),
                  pl.BlockSpec((1, T_ALL, WC), col(2)),
                  pl.BlockSpec((1, T_ALL, WC), col(3)),
                  pl.BlockSpec((1, T_ALL, WC), col(4)),
                  pl.BlockSpec((2, WC), lambda i: (0, 0)),
                  pl.BlockSpec((1, LANES), lambda i: (0, 0)),
                  pl.BlockSpec((2, 2 * c, c), lambda i: (0, 0, 0)),
                  pl.BlockSpec((WC, WC), lambda i: (0, 0))],
        out_specs=pl.BlockSpec((1, T_ALL, WC), lambda i: (i, 0, 0)),
        out_shape=jax.ShapeDtypeStruct((b, T_ALL, WC), BF16),
        scratch_shapes=[pltpu.VMEM((T_ALL, WC), F32),
                        pltpu.VMEM((2, WC // LANES, LANES, LANES), F32),
                        pltpu.VMEM((4, c * HG_SUB, WC), BF16),
                        pltpu.VMEM((2, 4, 3, c, WC), F32)],
        compiler_params=_params("parallel"),
        name="mixer_c_hgrn2",
    )(yc, yc, yc, yc, yc, lb, gn, tri, ones)


def _outproj_ffn_kernel(n_sub, t0, ap_ref, a_ref, an_ref, bp_ref, b_ref, bn_ref, cp_ref, c_ref, cn_ref,
                        xp_ref, x_ref, xn_ref, mod_ref, wout_ref, g1_ref, b1_ref, wup_ref, cw_ref, cb_ref, wdn_ref,
                        g2_ref, b2_ref, o_ref):
    step = pl.program_id(1)
    n_e = HALO + TM + HALO
    n_chunk = D_FF // FF_CHUNK

    def with_halo(prev_ref, ref, next_ref, j, halo):
        lo, hi = j * TM - halo, (j + 1) * TM + halo
        parts = [prev_ref[0]] if j == 0 else []
        parts.append(ref[0, max(lo, 0):min(hi, n_sub * TM), :])
        if j == n_sub - 1:
            parts.append(next_ref[0])
        return parts[0] if len(parts) == 1 else jnp.concatenate(parts, axis=0)

    def prologue(j):
        tile = step * n_sub + j + t0
        mods = mod_ref[0, 1] if j > 0 else jnp.where(tile == 0, mod_ref[0, 0], mod_ref[0, 1])
        mix = jnp.concatenate([with_halo(p, m, n, j, MIX_HALO) for p, m, n in
                               ((ap_ref, a_ref, an_ref), (bp_ref, b_ref, bn_ref), (cp_ref, c_ref, cn_ref))], axis=1)
        y = jnp.dot(mix, wout_ref[...], preferred_element_type=F32)[MIX_HALO - HALO:MIX_HALO + TM + HALO, :]
        xe = with_halo(xp_ref, x_ref, xn_ref, j, HALO)
        x1e = _ln(ALPHA * xe + mods[2:3, :] * y) * g1_ref[...] + b1_ref[...]
        h = _ln(x1e) * (1.0 + mods[4:5, :]) + mods[3:4, :]
        r = lax.broadcasted_iota(jnp.int32, (n_e, 1), 0)
        keep = ((r >= HALO) | (tile >= 2)) & ((r < HALO + TM) | ((tile >= 1) & (tile < N_TILES - 1)))
        return dict(h=jnp.where(keep, h, 0.0).astype(BF16), x1=x1e[HALO:HALO + TM, :], gate=mods[5:6, :])

    def epilogue(j, w):
        z = _ln(ALPHA * w['x1'] + w['gate'] * w['acc'])
        o_ref[0, j * TM:(j + 1) * TM, :] = z * g2_ref[...] + b2_ref[...]

    def up(h, c, base):
        return jnp.dot(h, wup_ref[:, base + c * FF_CHUNK:base + (c + 1) * FF_CHUNK], preferred_element_type=F32)

    def conv(u, c, base):
        cols = slice(base + c * FF_CHUNK, base + (c + 1) * FF_CHUNK)
        u = (pltpu.roll(u, 1, axis=0) * cw_ref[0:1, cols] + u * cw_ref[1:2, cols]
             + pltpu.roll(u, n_e - 1, axis=0) * cw_ref[2:3, cols])
        return u[HALO:HALO + TM, :] + cb_ref[:, cols]

    def hidden_chunks(w, hooks):
        ahead = 2
        acc = jnp.zeros((TM, D), F32)
        pending = [(up(w['h'], c, 0), up(w['h'], c, D_FF)) for c in range(ahead)]
        for c in range(n_chunk):
            u_gate, u_val = pending.pop(0)
            if c + ahead < n_chunk:
                pending.append((up(w['h'], c + ahead, 0), up(w['h'], c + ahead, D_FF)))
            if c in hooks:
                hooks[c]()
            a = (_silu(conv(u_gate, c, 0)) * conv(u_val, c, D_FF)).astype(BF16)
            acc = acc + jnp.dot(a, wdn_ref[c * FF_CHUNK:(c + 1) * FF_CHUNK, :], preferred_element_type=F32)
        w['acc'] = acc

    work = [None] * n_sub
    work[0] = prologue(0)
    for j in range(n_sub):
        hooks = {}
        if j > 0:
            hooks[0] = functools.partial(epilogue, j - 1, work[j - 1])
        if j + 1 < n_sub:
            hooks[n_chunk // 2] = functools.partial(lambda k: work.__setitem__(k, prologue(k)), j + 1)
        hidden_chunks(work[j], hooks)
    epilogue(n_sub - 1, work[n_sub - 1])


def _outproj_ffn(oa, ob, oc, xs, modsel, w_out_p, ln1_g, ln1_b, w_up, conv_w, conv_b, w_down, ln2_g, ln2_b, t0):
    b = xs.shape[0]
    n_sub = FFN_TILES_PER_STEP if t0 == 0 else 1
    assert (N_TILES - t0) % n_sub == 0
    rows = n_sub * TM
    full = lambda i, s: (0, 0)

    def halos(width, halo):
        per, last = rows // halo, T_ALL // halo - 1
        first = t0 * TM // halo
        return (pl.BlockSpec((1, halo, width), lambda i, s: (i, jnp.maximum(first + s * per - 1, first), 0)),
                pl.BlockSpec((1, rows, width), lambda i, s: (i, s + t0, 0)),
                pl.BlockSpec((1, halo, width), lambda i, s: (i, jnp.minimum(first + (s + 1) * per, last), 0)))

    return pl.pallas_call(
        functools.partial(_outproj_ffn_kernel, n_sub, t0),
        grid=(b, (N_TILES - t0) // n_sub),
        in_specs=[*halos(WA, MIX_HALO), *halos(WB, MIX_HALO), *halos(WC, MIX_HALO), *halos(D, HALO),
                  pl.BlockSpec((1, 2, 8, D), lambda i, s: (i, 0, 0, 0)),
                  pl.BlockSpec((D, D), full),
                  pl.BlockSpec((1, D), full),
                  pl.BlockSpec((1, D), full),
                  pl.BlockSpec((D, 2 * D_FF), full),
                  pl.BlockSpec((3, 2 * D_FF), full),
                  pl.BlockSpec((1, 2 * D_FF), full),
                  pl.BlockSpec((D_FF, D), full),
                  pl.BlockSpec((1, D), full),
                  pl.BlockSpec((1, D), full)],
        out_specs=pl.BlockSpec((1, rows, D), lambda i, s: (i, s, 0)),
        out_shape=jax.ShapeDtypeStruct((b, T_ALL - t0 * TM, D), F32),
        compiler_params=_params("parallel", "arbitrary"),
        name="outproj_ffn",
    )(oa, oa, oa, ob, ob, ob, oc, oc, oc, xs, xs, xs, modsel, w_out_p, ln1_g, ln1_b,
      w_up, conv_w, conv_b, w_down, ln2_g, ln2_b)


def _in_col_perm():
    off, o = {}, 0
    for name, width in (('a_k', WA), ('a_v', WA), ('b_k', WKV), ('b_v', WKV), ('c_f', WC), ('c_b', WC),
                        ('c_i', WC), ('a_q', WA), ('b_q', WB), ('c_q', WC), ('c_g', WC)):
        off[name] = np.arange(o, o + width)
        o += width
    bq = off['b_q'].reshape(GQA_HEADS, HD)[list(GQA_ORDER)].reshape(-1)
    return np.concatenate([off['a_q'], off['a_k'], off['a_v'], bq, off['b_k'], off['b_v'],
                           off['c_f'], off['c_b'], off['c_i'], off['c_q'], off['c_g']])


def _out_row_perm():
    ob = (WA + np.arange(WB)).reshape(GQA_HEADS, HD)[list(GQA_ORDER)].reshape(-1)
    return np.concatenate([np.arange(WA), ob, WA + WB + np.arange(WC)])


def _take_runs(w, perm, axis):
    cuts = [0] + [i for i in range(1, len(perm)) if perm[i] != perm[i - 1] + 1] + [len(perm)]
    parts = [lax.slice_in_dim(w, int(perm[a]), int(perm[z - 1]) + 1, axis=axis) for a, z in zip(cuts[:-1], cuts[1:])]
    return jnp.concatenate(parts, axis=axis)


def kernel(x, c, ctx, c_ctx, w_ada, b_ada, w_in, w_out, na_rpb, q_norm, k_norm, hg_lower_bound, hg_norm,
           ln1_g, ln1_b, w_up, conv_w, conv_b, w_down, ln2_g, ln2_b):
    b = x.shape[0]
    assert x.shape == (b, SEQ, D) and ctx.shape == (b, CTX, D)
    n_rows = -(-(b + 1) // 8) * 8
    s_in = jnp.concatenate([c, c_ctx[None, :], jnp.zeros((n_rows - b - 1, D), F32)], axis=0)
    mods = _modulation(s_in, w_ada, b_ada).reshape(DEPTH, n_rows, N_MOD, D)
    lbs = _lower_bounds(hg_lower_bound).reshape(DEPTH, 2, WC)

    cos, sin = (jnp.asarray(a) for a in _rope_tables())
    tri = jnp.asarray(_hgrn_consts(), BF16)
    ones = jnp.asarray(np.kron(np.eye(HG_HEADS), np.ones((HD, HD))), BF16)
    in_perm = _in_col_perm()
    out_perm = _out_row_perm()
    tile2 = lambda g: jnp.tile(g.astype(F32), 2).reshape(1, LANES)

    xs = jnp.concatenate([ctx, x], axis=1)
    for l in range(DEPTH):
        last = l == DEPTH - 1
        t0 = 1 if last else 0
        lat = mods[l, :b]
        cx = jnp.broadcast_to(mods[l, b][None], lat.shape)
        modsel = jnp.pad(jnp.stack([cx, lat], axis=1), ((0, 0), (0, 0), (0, 8 - N_MOD), (0, 0)))
        ya, yb, yc = _in_projection(xs, modsel, _take_runs(w_in[l], in_perm, 1).astype(BF16))
        oa, ob = _mixers_ab(ya, yb, _na_slabs(na_rpb[l]), cos, sin, tile2(q_norm[l]), tile2(k_norm[l]))
        oc = _mixer_c(yc, lbs[l], tile2(hg_norm[l]), tri, ones)
        xs = _outproj_ffn(oa, ob, oc, xs, modsel, _take_runs(w_out[l], out_perm, 0).astype(BF16),
                          ln1_g[l].reshape(1, D), ln1_b[l].reshape(1, D), w_up[l].astype(BF16), conv_w[l],
                          conv_b[l].reshape(1, 2 * D_FF), w_down[l].astype(BF16),
                          ln2_g[l].reshape(1, D), ln2_b[l].reshape(1, D), t0)
    return xs
```

```python
import functools

import numpy as np
import jax
import jax.numpy as jnp
from jax import lax
from jax.experimental import pallas as pl
from jax.experimental.pallas import tpu as pltpu

D = 1024
DEPTH = 4
GRID_W = 64
CTX = 256
SEQ = 2048
ROWS = SEQ // GRID_W
T_ALL = CTX + SEQ
HD = 64
NA_HEADS = 6
NA_WIN_ROWS = 8
NA_WIN_COLS = 16
GQA_HEADS = 6
GQA_KV = 2
HG_HEADS = 4
HG_CHUNK = 64
HG_SUB = 16
ROPE_THETA = 10000.0
LB_FLOOR = 1e-30
WA = NA_HEADS * HD
WB = GQA_HEADS * HD
WKV = GQA_KV * HD
WC = HG_HEADS * HD
IN_COLS = 3072
D_FF = 2816
N_MOD = 6
ALPHA = (2.0 * DEPTH) ** 0.25
LN_EPS = 1e-6
RMS_EPS = 1e-6
ATT_SCALE = HD ** -0.5
LOG2E = 1.4426950408889634
SOFTMAX_SCALE = ATT_SCALE * LOG2E
NEG = -1e30

LANES = 128
SUBLANES = 8
TM = 256
N_TILES = T_ALL // TM
NA_RB = 4
NA_WR = NA_WIN_ROWS + NA_RB - 1
NA_NLOC = NA_WR * GRID_W
NA_TYPES = 3
MOD_COLS = 1536
FF_CHUNK = 256
FFN_TILES_PER_STEP = 3
IN_TILES = 3
HALO = SUBLANES
MIX_HALO = 2 * SUBLANES
VMEM_LIMIT = 56 * 1024 * 1024

YA_COLS = 3 * WA
YB_COLS = WB + 2 * WKV
YC_COLS = 5 * WC
GQA_ORDER = (0, 3, 1, 4, 2, 5)

HI = lax.Precision.HIGHEST
F32 = jnp.float32
BF16 = jnp.bfloat16


def _params(*sem):
    return pltpu.CompilerParams(dimension_semantics=sem, vmem_limit_bytes=VMEM_LIMIT)


def _dot_nt(a, b):
    return lax.dot_general(a, b, (((1,), (1,)), ((), ())), preferred_element_type=F32)


def _dot_tn(a, b):
    return lax.dot_general(a, b, (((0,), (0,)), ((), ())), preferred_element_type=F32)


def _ln(x):
    mu = jnp.mean(x, axis=-1, keepdims=True)
    xc = x - mu
    var = jnp.mean(xc * xc, axis=-1, keepdims=True)
    return xc * lax.rsqrt(var + LN_EPS)


def _dot_01(m, x):
    hi = x.astype(BF16)
    rest = x - hi.astype(F32)
    mid = rest.astype(BF16)
    lo = (rest - mid.astype(F32)).astype(BF16)
    return (jnp.dot(m, hi, preferred_element_type=F32) + jnp.dot(m, mid, preferred_element_type=F32)
            + jnp.dot(m, lo, preferred_element_type=F32))


def _silu(x):
    return x * jax.nn.sigmoid(x)


def _lo_mask(shape):
    return lax.broadcasted_iota(jnp.int32, shape, len(shape) - 1) % LANES < HD


def _head_rms(x, gain):
    lo = _lo_mask(x.shape)
    x2 = x * x
    s_lo = jnp.sum(jnp.where(lo, x2, 0.0), axis=-1, keepdims=True)
    s_hi = jnp.sum(jnp.where(lo, 0.0, x2), axis=-1, keepdims=True)
    ms = jnp.where(lo, s_lo, s_hi) * (1.0 / HD)
    return x * lax.rsqrt(ms + RMS_EPS) * gain


def _mod_kernel(s_ref, w_ref, b_ref, o_ref):
    s = _silu(s_ref[...])
    o_ref[0] = jnp.dot(s, w_ref[0], precision=HI, preferred_element_type=F32) + b_ref[0]


def _modulation(s_in, w_ada, b_ada):
    r = s_in.shape[0]
    cb = MOD_COLS
    return pl.pallas_call(
        _mod_kernel,
        grid=(DEPTH, N_MOD * D // cb),
        in_specs=[pl.BlockSpec((r, D), lambda l, j: (0, 0)),
                  pl.BlockSpec((1, D, cb), lambda l, j: (l, 0, j)),
                  pl.BlockSpec((1, 1, cb), lambda l, j: (l, 0, j))],
        out_specs=pl.BlockSpec((1, r, cb), lambda l, j: (l, 0, j)),
        out_shape=jax.ShapeDtypeStruct((DEPTH, r, N_MOD * D), F32),
        compiler_params=_params("arbitrary", "arbitrary"),
        name="adaln_mod",
    )(s_in, w_ada, b_ada.reshape(DEPTH, 1, N_MOD * D))


def _lb_kernel(p_ref, o_ref):
    p = p_ref[...]
    e = jnp.exp(p - jnp.max(p, axis=0, keepdims=True))
    sm = e / jnp.sum(e, axis=0, keepdims=True)
    acc = sm[0:1]
    rows = [acc - sm[0:1]]
    for l in range(1, DEPTH):
        acc = acc + sm[l:l + 1]
        rows.append(acc - sm[0:1])
    o_ref[...] = jnp.concatenate(rows, axis=0)


def _lower_bounds(hg_lower_bound):
    p = hg_lower_bound.reshape(DEPTH, 2 * WC).astype(F32)
    return pl.pallas_call(
        _lb_kernel,
        out_shape=jax.ShapeDtypeStruct((DEPTH, 2 * WC), F32),
        name="hgrn_lower_bounds",
    )(p)


def _inproj_kernel(x_ref, mod_ref, w_ref, ya_ref, yb_ref, yc_ref):
    step = pl.program_id(1)
    hs = []
    for j in range(IN_TILES):
        mods = mod_ref[0, 1] if j > 0 else jnp.where(step == 0, mod_ref[0, 0], mod_ref[0, 1])
        y = _ln(x_ref[0, j * TM:(j + 1) * TM, :])
        hs.append((y * (1.0 + mods[1:2, :]) + mods[0:1, :]).astype(BF16))
    for j, h in enumerate(hs):
        rows = slice(j * TM, (j + 1) * TM)
        out = jnp.dot(h, w_ref[...], preferred_element_type=F32)
        ya_ref[0, rows, :] = out[:, :YA_COLS].astype(BF16)
        yb_ref[0, rows, :] = out[:, YA_COLS:YA_COLS + YB_COLS].astype(BF16)
        yc_ref[0, rows, :] = out[:, YA_COLS + YB_COLS:]


def _in_projection(xs, modsel, w_in_p):
    b = xs.shape[0]
    row = lambda i, t: (i, t, 0)
    return pl.pallas_call(
        _inproj_kernel,
        grid=(b, N_TILES // IN_TILES),
        in_specs=[pl.BlockSpec((1, IN_TILES * TM, D), row),
                  pl.BlockSpec((1, 2, SUBLANES, D), lambda i, t: (i, 0, 0, 0)),
                  pl.BlockSpec((D, IN_COLS), lambda i, t: (0, 0))],
        out_specs=[pl.BlockSpec((1, IN_TILES * TM, YA_COLS), row),
                   pl.BlockSpec((1, IN_TILES * TM, YB_COLS), row),
                   pl.BlockSpec((1, IN_TILES * TM, YC_COLS), row)],
        out_shape=[jax.ShapeDtypeStruct((b, T_ALL, YA_COLS), BF16),
                   jax.ShapeDtypeStruct((b, T_ALL, YB_COLS), BF16),
                   jax.ShapeDtypeStruct((b, T_ALL, YC_COLS), F32)],
        compiler_params=_params("parallel", "arbitrary"),
        name="in_projection",
    )(xs, modsel, w_in_p)


def _na_slab_index():
    def table(rb):
        r0 = NA_RB * rb
        ws = int(np.clip(r0 - NA_WIN_ROWS // 2, 0, ROWS - NA_WR))
        qr = r0 + np.arange(NA_RB)[:, None]
        kr = ws + np.arange(NA_WR)[None, :]
        rs = np.clip(qr - NA_WIN_ROWS // 2, 0, ROWS - NA_WIN_ROWS)
        valid = (kr >= rs) & (kr < rs + NA_WIN_ROWS)
        return np.where(valid, kr - qr + NA_WIN_ROWS - 1, 2 * NA_WIN_ROWS - 1)

    n_rb = ROWS // NA_RB
    for rb in range(2, n_rb - 1):
        assert np.array_equal(table(rb), table(1))
    return np.stack([table(0), table(1), table(n_rb - 1)])


def _na_slabs(rpb):
    qc = np.arange(GRID_W)[:, None]
    kc = np.arange(GRID_W)[None, :]
    cs = np.clip(qc - NA_WIN_COLS // 2, 0, GRID_W - NA_WIN_COLS)
    valid = (kc >= cs) & (kc < cs + NA_WIN_COLS)
    onehot = (valid[None] & ((kc - qc + NA_WIN_COLS - 1)[None] == np.arange(2 * NA_WIN_COLS - 1)[:, None, None]))
    onehot = jnp.asarray(onehot.reshape(2 * NA_WIN_COLS - 1, GRID_W * GRID_W), F32)
    slabs = jnp.einsum('hrd,dx->hrx', rpb.astype(F32), onehot, precision=HI) * LOG2E
    slabs = jnp.where(jnp.asarray(valid.reshape(-1)), slabs, NEG)
    slabs = jnp.concatenate([slabs, jnp.full((NA_HEADS, 1, GRID_W * GRID_W), NEG, F32)], axis=1)
    slabs = slabs.reshape(NA_HEADS, 2 * NA_WIN_ROWS, GRID_W, GRID_W)
    return jnp.concatenate([slabs, slabs], axis=-1)


def _na_stages(idx, q_ref, k_ref, v_ref, slab_ref, o_ref, bias_ref):
    t = pl.program_id(1)
    lo = _lo_mask((1, LANES))

    def build_bias():
        for head in range(NA_HEADS):
            for typ in range(NA_TYPES):
                for rl in range(NA_RB):
                    rows = slice(rl * GRID_W, (rl + 1) * GRID_W)
                    for m in range(NA_WR // 2):
                        even = slab_ref[head, int(idx[typ, rl, 2 * m])]
                        odd = slab_ref[head, int(idx[typ, rl, 2 * m + 1])]
                        bias_ref[head, typ, rows, m * LANES:(m + 1) * LANES] = jnp.where(lo, even, odd)
                    if NA_WR % 2:
                        tail = slab_ref[head, int(idx[typ, rl, NA_WR - 1])]
                        bias_ref[head, typ, rows, (NA_WR - 1) * GRID_W:NA_NLOC] = tail[:, 0:GRID_W]

    def masked_queries(head):
        blk, half = divmod(head, 2)
        qs = (q_ref[0, :, blk * LANES:(blk + 1) * LANES].astype(F32) * SOFTMAX_SCALE).astype(BF16)
        zero = jnp.zeros_like(qs)
        return jnp.where(lo, qs, zero) if half == 0 else jnp.where(lo, zero, qs)

    def cols(head):
        blk = head // 2
        return slice(blk * LANES, (blk + 1) * LANES)

    def head_tasks(scores, finish):
        outs = {}

        def done(head, s):
            outs[head] = finish(head, s)
            if head % 2:
                o_ref[0, :, cols(head)] = jnp.where(lo, outs[head - 1], outs[head]).astype(o_ref.dtype)
        return [(functools.partial(scores, head), functools.partial(done, head)) for head in range(NA_HEADS)]

    def context_tile():
        def scores(head):
            return _dot_nt(masked_queries(head), k_ref[0, 0:CTX, cols(head)])

        def finish(head, s):
            p = jnp.exp2(s - jnp.max(s, axis=-1, keepdims=True))
            l = jnp.sum(p, axis=-1, keepdims=True)
            return jnp.dot(p.astype(BF16), v_ref[0, 0:CTX, cols(head)], preferred_element_type=F32) / l
        return head_tasks(scores, finish)

    def latent_tile():
        rb = t - 1
        ws = jnp.clip(NA_RB * rb - NA_WIN_ROWS // 2, 0, ROWS - NA_WR)
        window = pl.ds(pl.multiple_of(CTX + ws * GRID_W, GRID_W), NA_NLOC)
        typ = jnp.where(rb == 0, 0, jnp.where(rb == ROWS // NA_RB - 1, 2, 1))

        def scores(head):
            qm = masked_queries(head)
            return (_dot_nt(qm, k_ref[0, window, cols(head)]) + bias_ref[head, typ],
                    _dot_nt(qm, k_ref[0, 0:CTX, cols(head)]))

        def finish(head, s):
            s_loc, s_ctx = s
            m = jnp.maximum(jnp.max(s_loc, axis=-1, keepdims=True), jnp.max(s_ctx, axis=-1, keepdims=True))
            p_loc = jnp.exp2(s_loc - m)
            p_ctx = jnp.exp2(s_ctx - m)
            l = jnp.sum(p_loc, axis=-1, keepdims=True) + jnp.sum(p_ctx, axis=-1, keepdims=True)
            o = (jnp.dot(p_loc.astype(BF16), v_ref[0, window, cols(head)], preferred_element_type=F32)
                 + jnp.dot(p_ctx.astype(BF16), v_ref[0, 0:CTX, cols(head)], preferred_element_type=F32))
            return o / l
        return head_tasks(scores, finish)

    return build_bias, context_tile, latent_tile


def _rope_tables():
    t = np.arange(SEQ)
    pos = np.stack([t // GRID_W, t % GRID_W]).astype(np.float32)
    n_freq = HD // 4
    inv_freq = (ROPE_THETA ** (-np.arange(n_freq, dtype=np.float32) / n_freq)).astype(np.float32)
    ang = pos[:, :, None] * inv_freq
    cos, sin = np.cos(ang), np.sin(ang)
    cos_h = np.concatenate([cos[0], cos[0], cos[1], cos[1]], axis=-1)
    sin_h = np.concatenate([-sin[0], sin[0], -sin[1], sin[1]], axis=-1)
    cos_all = np.concatenate([np.ones((CTX, HD), np.float32), cos_h], axis=0)
    sin_all = np.concatenate([np.zeros((CTX, HD), np.float32), sin_h], axis=0)
    return (np.tile(cos_all, (1, 2)).astype(np.float32), np.tile(sin_all, (1, 2)).astype(np.float32))


def _rope(x, cos, sin):
    quarter = HD // 4
    first = lax.broadcasted_iota(jnp.int32, x.shape, 1) % (2 * quarter) < quarter
    partner = jnp.where(first, pltpu.roll(x, LANES - quarter, axis=1), pltpu.roll(x, quarter, axis=1))
    return x * cos + partner * sin


def _gqa_stages(q_ref, k_ref, v_ref, cos_ref, sin_ref, qg_ref, kg_ref, o_ref, kn_ref):
    t = pl.program_id(1)
    lo = _lo_mask((1, LANES))

    def prepare_keys():
        def prep(i, carry):
            rows = pl.ds(pl.multiple_of(i * TM, TM), TM)
            kx = _head_rms(k_ref[0, rows, :].astype(F32), kg_ref[...])
            kn_ref[rows, :] = _rope(kx, cos_ref[rows, :], sin_ref[rows, :]).astype(BF16)
            return carry
        lax.fori_loop(0, N_TILES, prep, 0)

    def attend(n_keys):
        rows = pl.ds(pl.multiple_of(t * TM, TM), TM)
        cos = cos_ref[rows, :]
        sin = sin_ref[rows, :]
        kn = kn_ref[0:n_keys, :]
        vv = v_ref[0, 0:n_keys, :]

        qn = []
        for j in range(GQA_HEADS // 2):
            qx = _head_rms(q_ref[0, :, j * LANES:(j + 1) * LANES].astype(F32), qg_ref[...])
            qn.append((_rope(qx, cos, sin) * SOFTMAX_SCALE).astype(BF16))
        zero = jnp.zeros_like(qn[0])

        def scores(i):
            j, h = divmod(i, 2)
            return _dot_nt(jnp.where(lo, qn[j], zero) if h == 0 else jnp.where(lo, zero, qn[j]), kn)

        outs = {}

        def done(i, s):
            p = jnp.exp2(s - jnp.max(s, axis=-1, keepdims=True))
            l = jnp.sum(p, axis=-1, keepdims=True)
            outs[i] = jnp.dot(p.astype(BF16), vv, preferred_element_type=F32) / l
            if i % 2:
                j = i // 2
                o_ref[0, :, j * LANES:(j + 1) * LANES] = jnp.where(lo, outs[i - 1], outs[i]).astype(o_ref.dtype)
        return [(functools.partial(scores, i), functools.partial(done, i)) for i in range(GQA_HEADS)]

    return prepare_keys, functools.partial(attend, CTX), functools.partial(attend, T_ALL)


def _run_heads(tasks, ahead):
    pending = [issue() for issue, _ in tasks[:ahead]]
    for i, (_, finish) in enumerate(tasks):
        s = pending.pop(0)
        if i + ahead < len(tasks):
            pending.append(tasks[i + ahead][0]())
        finish(s)


def _attention_kernel(idx, qa_ref, ka_ref, va_ref, slab_ref, qb_ref, kb_ref, vb_ref, cos_ref, sin_ref, qg_ref, kg_ref,
                      oa_ref, ob_ref, bias_ref, kn_ref):
    t = pl.program_id(1)
    na_bias, na_context, na_latent = _na_stages(idx, qa_ref, ka_ref, va_ref, slab_ref, oa_ref, bias_ref)
    gqa_keys, gqa_context, gqa_latent = _gqa_stages(qb_ref, kb_ref, vb_ref, cos_ref, sin_ref, qg_ref, kg_ref,
                                                    ob_ref, kn_ref)

    @pl.when((pl.program_id(0) == 0) & (t == 0))
    def _():
        na_bias()

    @pl.when(t == 0)
    def _():
        gqa_keys()
        _run_heads(na_context() + gqa_context(), 2)

    @pl.when(t > 0)
    def _():
        _run_heads([task for pair in zip(na_latent(), gqa_latent()) for task in pair], 2)


def _mixers_ab(ya, yb, slabs, cos, sin, qg, kg):
    b = ya.shape[0]
    kcol = WB // LANES
    tile = lambda i, t: (i, t, 0)
    full = lambda i, t: (0, 0)
    return pl.pallas_call(
        functools.partial(_attention_kernel, _na_slab_index()),
        grid=(b, N_TILES),
        in_specs=[pl.BlockSpec((1, TM, WA), tile),
                  pl.BlockSpec((1, T_ALL, WA), lambda i, t: (i, 0, 1)),
                  pl.BlockSpec((1, T_ALL, WA), lambda i, t: (i, 0, 2)),
                  pl.BlockSpec((NA_HEADS, 2 * NA_WIN_ROWS, GRID_W, LANES), lambda i, t: (0, 0, 0, 0)),
                  pl.BlockSpec((1, TM, WB), tile),
                  pl.BlockSpec((1, T_ALL, LANES), lambda i, t: (i, 0, kcol)),
                  pl.BlockSpec((1, T_ALL, LANES), lambda i, t: (i, 0, kcol + 1)),
                  pl.BlockSpec((T_ALL, LANES), full),
                  pl.BlockSpec((T_ALL, LANES), full),
                  pl.BlockSpec((1, LANES), full),
                  pl.BlockSpec((1, LANES), full)],
        out_specs=[pl.BlockSpec((1, TM, WA), tile),
                   pl.BlockSpec((1, TM, WB), tile)],
        out_shape=[jax.ShapeDtypeStruct((b, T_ALL, WA), BF16),
                   jax.ShapeDtypeStruct((b, T_ALL, WB), BF16)],
        scratch_shapes=[pltpu.VMEM((NA_HEADS, NA_TYPES, NA_RB * GRID_W, NA_NLOC), F32),
                        pltpu.VMEM((T_ALL, LANES), BF16)],
        compiler_params=_params("arbitrary", "arbitrary"),
        name="mixers_ab_attention",
    )(ya, ya, ya, slabs, yb, yb, yb, cos, sin, qg, kg)


def _hgrn_consts():
    c, sc = HG_CHUNK, HG_SUB
    t = np.arange(c)
    out = []
    for reverse in (False, True):
        if not reverse:
            cum = t[None, :] <= t[:, None]
            through = (t[None, :] // sc) <= (t[:, None] // sc)
        else:
            cum = t[None, :] >= t[:, None]
            through = (t[None, :] // sc) >= (t[:, None] // sc)
        out.append(np.concatenate([cum, through], axis=0).astype(np.float32))
    return np.stack(out)


def _forget_gate(z, lb):
    ez = jnp.exp(-jnp.abs(z))
    big = 1.0 / (1.0 + ez)
    small = ez * big
    pos = z >= 0.0
    lb_floor = jnp.maximum(lb, LB_FLOOR)
    f = lb_floor + (1.0 - lb) * jnp.where(pos, big, small)
    one_minus_f = (1.0 - lb) * jnp.where(pos, small, big) - (lb_floor - lb)
    return jnp.log2(f), one_minus_f


def _hgrn_kernel(zf_ref, zb_ref, v_ref, q_ref, g_ref, lb_ref, gn_ref, tri_ref, ones_ref,
                 o_ref, acc_ref, st_ref, d_ref, b_ref):
    c, sc = HG_CHUNK, HG_SUB
    n_sub = c // sc
    n_pair = WC // LANES
    acc_ref[...] = jnp.zeros_like(acc_ref)
    st_ref[...] = jnp.zeros_like(st_ref)
    sub_row = lax.broadcasted_iota(jnp.int32, (sc, 1), 0)
    blk = (lax.broadcasted_iota(jnp.int32, (LANES, LANES), 0) // HD
           == lax.broadcasted_iota(jnp.int32, (LANES, LANES), 1) // HD)
    own_head = (lax.broadcasted_iota(jnp.int32, (HG_HEADS * c, WC), 0) // c
                == lax.broadcasted_iota(jnp.int32, (HG_HEADS * c, WC), 1) // HD)
    q_sub = lax.broadcasted_iota(jnp.int32, (c, WC), 0) // sc
    k_sub = (lax.broadcasted_iota(jnp.int32, (c, WC), 1) % c) // sc
    k_lane = lax.broadcasted_iota(jnp.int32, (1, WC), 1) % c

    z_refs = (zf_ref, zb_ref)
    n_ctx = CTX // c
    n_all = T_ALL // c

    def chunk_of(d, n):
        return n if d == 0 else jnp.where(n < n_ctx, n_ctx - 1 - n, n_all + n_ctx - 1 - n)

    def rows_of(d, n):
        return pl.ds(pl.multiple_of(chunk_of(d, n) * c, c), c)

    def gates(d, n, slot):
        g, kk = _forget_gate(z_refs[d][0, rows_of(d, n), :], lb_ref[d:d + 1, :])
        cums = _dot_01(tri_ref[d], g)
        b_ref[d, slot, 0] = cums[0:c]
        b_ref[d, slot, 1] = kk
        b_ref[d, slot, 2] = cums[c:2 * c]

    def state_and_far(d, n, slot):
        reverse = d == 1
        rows = rows_of(d, n)
        b = b_ref[d, slot, 0]
        kk = b_ref[d, slot, 1]
        e = b_ref[d, slot, 2]
        q = q_ref[0, rows, :] * ATT_SCALE
        vb = v_ref[0, rows, :].astype(BF16)
        last = c - 1 if not reverse else 0
        b_last = b_ref[d, slot, 0, last:last + 1, :]

        qe = (q * jnp.exp2(b)).astype(BF16)
        kdec = (kk * jnp.exp2(b_last - b)).astype(BF16)
        o_parts = []
        for p in range(n_pair):
            ln = slice(p * LANES, (p + 1) * LANES)
            st = st_ref[d, p]
            o_parts.append(_dot_nt(qe[:, ln], st.astype(BF16)))
            upd = _dot_tn(vb[:, ln], kdec[:, ln])
            st_ref[d, p] = jnp.where(blk, st * jnp.exp2(b_last[:, ln]) + upd, 0.0)
        o = jnp.concatenate(o_parts, axis=1)

        kx = (kk * jnp.exp2(e - b)).astype(BF16)
        keys = jnp.where(own_head, jnp.concatenate([kx] * HG_HEADS, axis=0), jnp.zeros((), BF16))
        vals = jnp.where(own_head, jnp.concatenate([vb] * HG_HEADS, axis=0), jnp.zeros((), BF16))
        q_stack = []
        for j in range(n_sub - 1):
            jj = j if not reverse else n_sub - 1 - j
            edge = (jj + 1) * sc - 1 if not reverse else jj * sc
            e_j = b_ref[d, slot, 0, edge:edge + 1, :]
            q_stack.append((q * jnp.exp2(jnp.minimum(b - e_j, 0.0))).astype(BF16))
        res = _dot_nt(jnp.concatenate(q_stack, axis=0), keys)
        att = jnp.zeros((c, WC), F32)
        for j in range(n_sub - 1):
            jj = j if not reverse else n_sub - 1 - j
            later = q_sub > jj if not reverse else q_sub < jj
            att = att + jnp.where((k_sub == jj) & later, res[j * c:(j + 1) * c, :], 0.0)
        return dict(rows=rows, b=b, q=q, o=o, att=att, vals=vals)

    def near_products(d, slot, w):
        dk = d * 2 + slot % 2
        reverse = d == 1
        for i in range(n_sub):
            bi = w['b'][i * sc:(i + 1) * sc, :]
            qi = w['q'][i * sc:(i + 1) * sc, :]
            for s in range(sc):
                r = i * sc + s
                p = jnp.exp2(bi - b_ref[d, slot, 0, r:r + 1, :]) * (qi * b_ref[d, slot, 1, r:r + 1, :])
                keep = sub_row >= s if not reverse else sub_row <= s
                d_ref[dk, r * sc:(r + 1) * sc, :] = jnp.where(keep, p, 0.0).astype(BF16)
        w['sums'] = jnp.dot(d_ref[dk], ones_ref[...], preferred_element_type=F32)

    def finish(d, w):
        near = []
        for i in range(n_sub):
            a = jnp.zeros((sc, WC), F32)
            for s in range(sc):
                r = i * sc + s
                a = jnp.where(k_lane == r, w['sums'][r * sc:(r + 1) * sc, :], a)
            near.append(a)
        att = w['att'] + jnp.concatenate(near, axis=0)
        acc_ref[w['rows'], :] += w['o'] + jnp.dot(att.astype(BF16), w['vals'], preferred_element_type=F32)

    def two_steps(n0, slots, next_slots):
        streams = [(k, d) for k in range(2) for d in range(2)]
        for k, d in streams:
            gates(d, jnp.minimum(n0 + 2 + k, n_all - 1), next_slots[k])
        work = {(k, d): state_and_far(d, n0 + k, slots[k]) for k, d in streams}
        for k, d in streams:
            near_products(d, slots[k], work[k, d])
        for k, d in streams:
            finish(d, work[k, d])

    for k in range(2):
        for d in range(2):
            gates(d, k, k)

    def body(m, carry):
        two_steps(4 * m, (0, 1), (2, 3))
        two_steps(4 * m + 2, (2, 3), (0, 1))
        return carry
    lax.fori_loop(0, n_all // 4, body, 0)

    def readout(i, carry):
        rows = pl.ds(pl.multiple_of(i * TM, TM), TM)
        gate = _silu(g_ref[0, rows, :])
        for p in range(n_pair):
            ln = slice(p * LANES, (p + 1) * LANES)
            y = _head_rms(acc_ref[rows, ln], gn_ref[...]) * gate[:, ln]
            o_ref[0, rows, ln] = y.astype(o_ref.dtype)
        return carry
    lax.fori_loop(0, N_TILES, readout, 0)


def _mixer_c(yc, lb, gn, tri, ones):
    b = yc.shape[0]
    c = HG_CHUNK
    col = lambda j: (lambda i: (i, 0, j))
    return pl.pallas_call(
        _hgrn_kernel,
        grid=(b,),
        in_specs=[pl.BlockSpec((1, T_ALL, WC), col(0)),
                  pl.BlockSpec((1, T_ALL, WC), col(1)),
                  pl.BlockSpec((1, T_ALL, WC), col(2)),
                  pl.BlockSpec((1, T_ALL, WC), col(3)),
                  pl.BlockSpec((1, T_ALL, WC), col(4)),
                  pl.BlockSpec((2, WC), lambda i: (0, 0)),
                  pl.BlockSpec((1, LANES), lambda i: (0, 0)),
                  pl.BlockSpec((2, 2 * c, c), lambda i: (0, 0, 0)),
                  pl.BlockSpec((WC, WC), lambda i: (0, 0))],
        out_specs=pl.BlockSpec((1, T_ALL, WC), lambda i: (i, 0, 0)),
        out_shape=jax.ShapeDtypeStruct((b, T_ALL, WC), BF16),
        scratch_shapes=[pltpu.VMEM((T_ALL, WC), F32),
                        pltpu.VMEM((2, WC // LANES, LANES, LANES), F32),
                        pltpu.VMEM((4, c * HG_SUB, WC), BF16),
                        pltpu.VMEM((2, 4, 3, c, WC), F32)],
        compiler_params=_params("parallel"),
        name="mixer_c_hgrn2",
    )(yc, yc, yc, yc, yc, lb, gn, tri, ones)


def _outproj_ffn_kernel(n_sub, t0, ap_ref, a_ref, an_ref, bp_ref, b_ref, bn_ref, cp_ref, c_ref, cn_ref,
                        xp_ref, x_ref, xn_ref, mod_ref, wout_ref, g1_ref, b1_ref, wup_ref, cw_ref, cb_ref, wdn_ref,
                        g2_ref, b2_ref, o_ref):
    step = pl.program_id(1)
    n_e = HALO + TM + HALO
    n_chunk = D_FF // FF_CHUNK

    def with_halo(prev_ref, ref, next_ref, j, halo):
        lo, hi = j * TM - halo, (j + 1) * TM + halo
        parts = [prev_ref[0]] if j == 0 else []
        parts.append(ref[0, max(lo, 0):min(hi, n_sub * TM), :])
        if j == n_sub - 1:
            parts.append(next_ref[0])
        return parts[0] if len(parts) == 1 else jnp.concatenate(parts, axis=0)

    def prologue(j):
        tile = step * n_sub + j + t0
        mods = mod_ref[0, 1] if j > 0 else jnp.where(tile == 0, mod_ref[0, 0], mod_ref[0, 1])
        mix = jnp.concatenate([with_halo(p, m, n, j, MIX_HALO) for p, m, n in
                               ((ap_ref, a_ref, an_ref), (bp_ref, b_ref, bn_ref), (cp_ref, c_ref, cn_ref))], axis=1)
        y = jnp.dot(mix, wout_ref[...], preferred_element_type=F32)[MIX_HALO - HALO:MIX_HALO + TM + HALO, :]
        xe = with_halo(xp_ref, x_ref, xn_ref, j, HALO)
        x1e = _ln(ALPHA * xe + mods[2:3, :] * y) * g1_ref[...] + b1_ref[...]
        h = _ln(x1e) * (1.0 + mods[4:5, :]) + mods[3:4, :]
        r = lax.broadcasted_iota(jnp.int32, (n_e, 1), 0)
        keep = ((r >= HALO) | (tile >= 2)) & ((r < HALO + TM) | ((tile >= 1) & (tile < N_TILES - 1)))
        return dict(h=jnp.where(keep, h, 0.0).astype(BF16), x1=x1e[HALO:HALO + TM, :], gate=mods[5:6, :])

    def epilogue(j, w):
        z = _ln(ALPHA * w['x1'] + w['gate'] * w['acc'])
        o_ref[0, j * TM:(j + 1) * TM, :] = z * g2_ref[...] + b2_ref[...]

    def up(h, c, base):
        return jnp.dot(h, wup_ref[:, base + c * FF_CHUNK:base + (c + 1) * FF_CHUNK], preferred_element_type=F32)

    def conv(u, c, base):
        cols = slice(base + c * FF_CHUNK, base + (c + 1) * FF_CHUNK)
        u = (pltpu.roll(u, 1, axis=0) * cw_ref[0:1, cols] + u * cw_ref[1:2, cols]
             + pltpu.roll(u, n_e - 1, axis=0) * cw_ref[2:3, cols])
        return u[HALO:HALO + TM, :] + cb_ref[:, cols]

    def hidden_chunks(w, hooks):
        ahead = 2
        acc = jnp.zeros((TM, D), F32)
        pending = [(up(w['h'], c, 0), up(w['h'], c, D_FF)) for c in range(ahead)]
        for c in range(n_chunk):
            u_gate, u_val = pending.pop(0)
            if c + ahead < n_chunk:
                pending.append((up(w['h'], c + ahead, 0), up(w['h'], c + ahead, D_FF)))
            if c in hooks:
                hooks[c]()
            a = (_silu(conv(u_gate, c, 0)) * conv(u_val, c, D_FF)).astype(BF16)
            acc = acc + jnp.dot(a, wdn_ref[c * FF_CHUNK:(c + 1) * FF_CHUNK, :], preferred_element_type=F32)
        w['acc'] = acc

    work = [None] * n_sub
    work[0] = prologue(0)
    for j in range(n_sub):
        hooks = {}
        if j > 0:
            hooks[0] = functools.partial(epilogue, j - 1, work[j - 1])
        if j + 1 < n_sub:
            hooks[n_chunk // 2] = functools.partial(lambda k: work.__setitem__(k, prologue(k)), j + 1)
        hidden_chunks(work[j], hooks)
    epilogue(n_sub - 1, work[n_sub - 1])


def _outproj_ffn(oa, ob, oc, xs, modsel, w_out_p, ln1_g, ln1_b, w_up, conv_w, conv_b, w_down, ln2_g, ln2_b, t0):
    b = xs.shape[0]
    n_sub = FFN_TILES_PER_STEP if t0 == 0 else 1
    assert (N_TILES - t0) % n_sub == 0
    rows = n_sub * TM
    full = lambda i, s: (0, 0)

    def halos(width, halo):
        per, last = rows // halo, T_ALL // halo - 1
        first = t0 * TM // halo
        return (pl.BlockSpec((1, halo, width), lambda i, s: (i, jnp.maximum(first + s * per - 1, first), 0)),
                pl.BlockSpec((1, rows, width), lambda i, s: (i, s + t0, 0)),
                pl.BlockSpec((1, halo, width), lambda i, s: (i, jnp.minimum(first + (s + 1) * per, last), 0)))

    return pl.pallas_call(
        functools.partial(_outproj_ffn_kernel, n_sub, t0),
        grid=(b, (N_TILES - t0) // n_sub),
        in_specs=[*halos(WA, MIX_HALO), *halos(WB, MIX_HALO), *halos(WC, MIX_HALO), *halos(D, HALO),
                  pl.BlockSpec((1, 2, SUBLANES, D), lambda i, s: (i, 0, 0, 0)),
                  pl.BlockSpec((D, D), full),
                  pl.BlockSpec((1, D), full),
                  pl.BlockSpec((1, D), full),
                  pl.BlockSpec((D, 2 * D_FF), full),
                  pl.BlockSpec((3, 2 * D_FF), full),
                  pl.BlockSpec((1, 2 * D_FF), full),
                  pl.BlockSpec((D_FF, D), full),
                  pl.BlockSpec((1, D), full),
                  pl.BlockSpec((1, D), full)],
        out_specs=pl.BlockSpec((1, rows, D), lambda i, s: (i, s, 0)),
        out_shape=jax.ShapeDtypeStruct((b, T_ALL - t0 * TM, D), F32),
        compiler_params=_params("parallel", "arbitrary"),
        name="outproj_ffn",
    )(oa, oa, oa, ob, ob, ob, oc, oc, oc, xs, xs, xs, modsel, w_out_p, ln1_g, ln1_b,
      w_up, conv_w, conv_b, w_down, ln2_g, ln2_b)


def _in_col_perm():
    off, o = {}, 0
    for name, width in (('a_k', WA), ('a_v', WA), ('b_k', WKV), ('b_v', WKV), ('c_f', WC), ('c_b', WC),
                        ('c_i', WC), ('a_q', WA), ('b_q', WB), ('c_q', WC), ('c_g', WC)):
        off[name] = np.arange(o, o + width)
        o += width
    bq = off['b_q'].reshape(GQA_HEADS, HD)[list(GQA_ORDER)].reshape(-1)
    return np.concatenate([off['a_q'], off['a_k'], off['a_v'], bq, off['b_k'], off['b_v'],
                           off['c_f'], off['c_b'], off['c_i'], off['c_q'], off['c_g']])


def _out_row_perm():
    ob = (WA + np.arange(WB)).reshape(GQA_HEADS, HD)[list(GQA_ORDER)].reshape(-1)
    return np.concatenate([np.arange(WA), ob, WA + WB + np.arange(WC)])


def _take_runs(w, perm, axis):
    cuts = [0] + [i for i in range(1, len(perm)) if perm[i] != perm[i - 1] + 1] + [len(perm)]
    parts = [lax.slice_in_dim(w, int(perm[a]), int(perm[z - 1]) + 1, axis=axis) for a, z in zip(cuts[:-1], cuts[1:])]
    return jnp.concatenate(parts, axis=axis)


def kernel(x, c, ctx, c_ctx, w_ada, b_ada, w_in, w_out, na_rpb, q_norm, k_norm, hg_lower_bound, hg_norm,
           ln1_g, ln1_b, w_up, conv_w, conv_b, w_down, ln2_g, ln2_b):
    b = x.shape[0]
    assert x.shape == (b, SEQ, D) and ctx.shape == (b, CTX, D)
    n_rows = -(-(b + 1) // SUBLANES) * SUBLANES
    s_in = jnp.concatenate([c, c_ctx[None, :], jnp.zeros((n_rows - b - 1, D), F32)], axis=0)
    mods = _modulation(s_in, w_ada, b_ada).reshape(DEPTH, n_rows, N_MOD, D)
    lbs = _lower_bounds(hg_lower_bound).reshape(DEPTH, 2, WC)

    cos, sin = (jnp.asarray(a) for a in _rope_tables())
    tri = jnp.asarray(_hgrn_consts(), BF16)
    ones = jnp.asarray(np.kron(np.eye(HG_HEADS), np.ones((HD, HD))), BF16)
    in_perm = _in_col_perm()
    out_perm = _out_row_perm()
    tile2 = lambda g: jnp.tile(g.astype(F32), 2).reshape(1, LANES)

    xs = jnp.concatenate([ctx, x], axis=1)
    for l in range(DEPTH):
        last = l == DEPTH - 1
        t0 = 1 if last else 0
        lat = mods[l, :b]
        cx = jnp.broadcast_to(mods[l, b][None], lat.shape)
        modsel = jnp.pad(jnp.stack([cx, lat], axis=1), ((0, 0), (0, 0), (0, SUBLANES - N_MOD), (0, 0)))
        ya, yb, yc = _in_projection(xs, modsel, _take_runs(w_in[l], in_perm, 1).astype(BF16))
        oa, ob = _mixers_ab(ya, yb, _na_slabs(na_rpb[l]), cos, sin, tile2(q_norm[l]), tile2(k_norm[l]))
        oc = _mixer_c(yc, lbs[l], tile2(hg_norm[l]), tri, ones)
        xs = _outproj_ffn(oa, ob, oc, xs, modsel, _take_runs(w_out[l], out_perm, 0).astype(BF16),
                          ln1_g[l].reshape(1, D), ln1_b[l].reshape(1, D), w_up[l].astype(BF16), conv_w[l],
                          conv_b[l].reshape(1, 2 * D_FF), w_down[l].astype(BF16),
                          ln2_g[l].reshape(1, D), ln2_b[l].reshape(1, D), t0)
    return xs
```

```python
import functools

import numpy as np
import jax
import jax.numpy as jnp
from jax import lax
from jax.experimental import pallas as pl
from jax.experimental.pallas import tpu as pltpu

D = 1024
DEPTH = 4
GRID_W = 64
CTX = 256
SEQ = 2048
ROWS = SEQ // GRID_W
T_ALL = CTX + SEQ
HD = 64
NA_HEADS = 6
NA_WIN_ROWS = 8
NA_WIN_COLS = 16
GQA_HEADS = 6
GQA_KV = 2
HG_HEADS = 4
HG_CHUNK = 64
HG_SUB = 16
ROPE_THETA = 10000.0
LB_FLOOR = 1e-30
WA = NA_HEADS * HD
WB = GQA_HEADS * HD
WKV = GQA_KV * HD
WC = HG_HEADS * HD
IN_COLS = 3072
D_FF = 2816
N_MOD = 6
ALPHA = (2.0 * DEPTH) ** 0.25
LN_EPS = 1e-6
RMS_EPS = 1e-6
ATT_SCALE = HD ** -0.5
LOG2E = 1.4426950408889634
SOFTMAX_SCALE = ATT_SCALE * LOG2E
NEG = -1e30

LANES = 128
SUBLANES = 8
TM = 256
N_TILES = T_ALL // TM
NA_RB = 4
NA_WR = NA_WIN_ROWS + NA_RB - 1
NA_NLOC = NA_WR * GRID_W
NA_TYPES = 3
MOD_COLS = 1536
FF_CHUNK = 256
FFN_TILES_PER_STEP = 3
IN_TILES = 3
HALO = SUBLANES
MIX_HALO = 2 * SUBLANES
VMEM_LIMIT = 56 * 1024 * 1024

YA_COLS = 3 * WA
YB_COLS = WB + 2 * WKV
YC_COLS = 5 * WC
GQA_ORDER = (0, 3, 1, 4, 2, 5)

HI = lax.Precision.HIGHEST
F32 = jnp.float32
BF16 = jnp.bfloat16


def _params(*sem):
    return pltpu.CompilerParams(dimension_semantics=sem, vmem_limit_bytes=VMEM_LIMIT)


def _dot_nt(a, b):
    return lax.dot_general(a, b, (((1,), (1,)), ((), ())), preferred_element_type=F32)


def _dot_tn(a, b):
    return lax.dot_general(a, b, (((0,), (0,)), ((), ())), preferred_element_type=F32)


def _ln(x):
    mu = jnp.mean(x, axis=-1, keepdims=True)
    xc = x - mu
    var = jnp.mean(xc * xc, axis=-1, keepdims=True)
    return xc * lax.rsqrt(var + LN_EPS)


def _dot_01(m, x):
    hi = x.astype(BF16)
    rest = x - hi.astype(F32)
    mid = rest.astype(BF16)
    lo = (rest - mid.astype(F32)).astype(BF16)
    return (jnp.dot(m, hi, preferred_element_type=F32) + jnp.dot(m, mid, preferred_element_type=F32)
            + jnp.dot(m, lo, preferred_element_type=F32))


def _silu(x):
    return x * jax.nn.sigmoid(x)


def _lo_mask(shape):
    return lax.broadcasted_iota(jnp.int32, shape, len(shape) - 1) % LANES < HD


def _head_rms(x, gain):
    lo = _lo_mask(x.shape)
    x2 = x * x
    s_lo = jnp.sum(jnp.where(lo, x2, 0.0), axis=-1, keepdims=True)
    s_hi = jnp.sum(jnp.where(lo, 0.0, x2), axis=-1, keepdims=True)
    ms = jnp.where(lo, s_lo, s_hi) * (1.0 / HD)
    return x * lax.rsqrt(ms + RMS_EPS) * gain


def _mod_kernel(s_ref, w_ref, b_ref, o_ref):
    s = _silu(s_ref[...])
    o_ref[0] = jnp.dot(s, w_ref[0], precision=HI, preferred_element_type=F32) + b_ref[0]


def _modulation(s_in, w_ada, b_ada):
    r = s_in.shape[0]
    cb = MOD_COLS
    return pl.pallas_call(
        _mod_kernel,
        grid=(DEPTH, N_MOD * D // cb),
        in_specs=[pl.BlockSpec((r, D), lambda l, j: (0, 0)),
                  pl.BlockSpec((1, D, cb), lambda l, j: (l, 0, j)),
                  pl.BlockSpec((1, 1, cb), lambda l, j: (l, 0, j))],
        out_specs=pl.BlockSpec((1, r, cb), lambda l, j: (l, 0, j)),
        out_shape=jax.ShapeDtypeStruct((DEPTH, r, N_MOD * D), F32),
        compiler_params=_params("arbitrary", "arbitrary"),
        name="adaln_mod",
    )(s_in, w_ada, b_ada.reshape(DEPTH, 1, N_MOD * D))


def _lb_kernel(p_ref, o_ref):
    p = p_ref[...]
    e = jnp.exp(p - jnp.max(p, axis=0, keepdims=True))
    sm = e / jnp.sum(e, axis=0, keepdims=True)
    acc = sm[0:1]
    rows = [acc - sm[0:1]]
    for l in range(1, DEPTH):
        acc = acc + sm[l:l + 1]
        rows.append(acc - sm[0:1])
    o_ref[...] = jnp.concatenate(rows, axis=0)


def _lower_bounds(hg_lower_bound):
    p = hg_lower_bound.reshape(DEPTH, 2 * WC).astype(F32)
    return pl.pallas_call(
        _lb_kernel,
        out_shape=jax.ShapeDtypeStruct((DEPTH, 2 * WC), F32),
        name="hgrn_lower_bounds",
    )(p)


def _inproj_kernel(x_ref, mod_ref, w_ref, ya_ref, yb_ref, yc_ref):
    step = pl.program_id(1)
    hs = []
    for j in range(IN_TILES):
        mods = mod_ref[0, 1] if j > 0 else jnp.where(step == 0, mod_ref[0, 0], mod_ref[0, 1])
        y = _ln(x_ref[0, j * TM:(j + 1) * TM, :])
        hs.append((y * (1.0 + mods[1:2, :]) + mods[0:1, :]).astype(BF16))
    for j, h in enumerate(hs):
        rows = slice(j * TM, (j + 1) * TM)
        out = jnp.dot(h, w_ref[...], preferred_element_type=F32)
        ya_ref[0, rows, :] = out[:, :YA_COLS].astype(BF16)
        yb_ref[0, rows, :] = out[:, YA_COLS:YA_COLS + YB_COLS].astype(BF16)
        yc_ref[0, rows, :] = out[:, YA_COLS + YB_COLS:]


def _in_projection(xs, modsel, w_in_p):
    b = xs.shape[0]
    row = lambda i, t: (i, t, 0)
    return pl.pallas_call(
        _inproj_kernel,
        grid=(b, N_TILES // IN_TILES),
        in_specs=[pl.BlockSpec((1, IN_TILES * TM, D), row),
                  pl.BlockSpec((1, 2, SUBLANES, D), lambda i, t: (i, 0, 0, 0)),
                  pl.BlockSpec((D, IN_COLS), lambda i, t: (0, 0))],
        out_specs=[pl.BlockSpec((1, IN_TILES * TM, YA_COLS), row),
                   pl.BlockSpec((1, IN_TILES * TM, YB_COLS), row),
                   pl.BlockSpec((1, IN_TILES * TM, YC_COLS), row)],
        out_shape=[jax.ShapeDtypeStruct((b, T_ALL, YA_COLS), BF16),
                   jax.ShapeDtypeStruct((b, T_ALL, YB_COLS), BF16),
                   jax.ShapeDtypeStruct((b, T_ALL, YC_COLS), F32)],
        compiler_params=_params("parallel", "arbitrary"),
        name="in_projection",
    )(xs, modsel, w_in_p)


def _na_slab_index():
    def table(rb):
        r0 = NA_RB * rb
        ws = int(np.clip(r0 - NA_WIN_ROWS // 2, 0, ROWS - NA_WR))
        qr = r0 + np.arange(NA_RB)[:, None]
        kr = ws + np.arange(NA_WR)[None, :]
        rs = np.clip(qr - NA_WIN_ROWS // 2, 0, ROWS - NA_WIN_ROWS)
        valid = (kr >= rs) & (kr < rs + NA_WIN_ROWS)
        return np.where(valid, kr - qr + NA_WIN_ROWS - 1, 2 * NA_WIN_ROWS - 1)

    n_rb = ROWS // NA_RB
    for rb in range(2, n_rb - 1):
        assert np.array_equal(table(rb), table(1))
    return np.stack([table(0), table(1), table(n_rb - 1)])


def _na_slabs(rpb):
    qc = np.arange(GRID_W)[:, None]
    kc = np.arange(GRID_W)[None, :]
    cs = np.clip(qc - NA_WIN_COLS // 2, 0, GRID_W - NA_WIN_COLS)
    valid = (kc >= cs) & (kc < cs + NA_WIN_COLS)
    onehot = (valid[None] & ((kc - qc + NA_WIN_COLS - 1)[None] == np.arange(2 * NA_WIN_COLS - 1)[:, None, None]))
    onehot = jnp.asarray(onehot.reshape(2 * NA_WIN_COLS - 1, GRID_W * GRID_W), F32)
    slabs = jnp.einsum('hrd,dx->hrx', rpb.astype(F32), onehot, precision=HI) * LOG2E
    slabs = jnp.where(jnp.asarray(valid.reshape(-1)), slabs, NEG)
    slabs = jnp.concatenate([slabs, jnp.full((NA_HEADS, 1, GRID_W * GRID_W), NEG, F32)], axis=1)
    slabs = slabs.reshape(NA_HEADS, 2 * NA_WIN_ROWS, GRID_W, GRID_W)
    return jnp.concatenate([slabs, slabs], axis=-1)


def _na_stages(idx, q_ref, k_ref, v_ref, slab_ref, o_ref, bias_ref):
    t = pl.program_id(1)
    lo = _lo_mask((1, LANES))

    def build_bias():
        for head in range(NA_HEADS):
            for typ in range(NA_TYPES):
                for rl in range(NA_RB):
                    rows = slice(rl * GRID_W, (rl + 1) * GRID_W)
                    for m in range(NA_WR // 2):
                        even = slab_ref[head, int(idx[typ, rl, 2 * m])]
                        odd = slab_ref[head, int(idx[typ, rl, 2 * m + 1])]
                        bias_ref[head, typ, rows, m * LANES:(m + 1) * LANES] = jnp.where(lo, even, odd)
                    if NA_WR % 2:
                        tail = slab_ref[head, int(idx[typ, rl, NA_WR - 1])]
                        bias_ref[head, typ, rows, (NA_WR - 1) * GRID_W:NA_NLOC] = tail[:, 0:GRID_W]

    def masked_queries(head):
        blk, half = divmod(head, 2)
        qs = (q_ref[0, :, blk * LANES:(blk + 1) * LANES].astype(F32) * SOFTMAX_SCALE).astype(BF16)
        zero = jnp.zeros_like(qs)
        return jnp.where(lo, qs, zero) if half == 0 else jnp.where(lo, zero, qs)

    def cols(head):
        blk = head // 2
        return slice(blk * LANES, (blk + 1) * LANES)

    def head_tasks(scores, finish):
        outs = {}

        def done(head, s):
            outs[head] = finish(head, s)
            if head % 2:
                o_ref[0, :, cols(head)] = jnp.where(lo, outs[head - 1], outs[head]).astype(o_ref.dtype)
        return [(functools.partial(scores, head), functools.partial(done, head)) for head in range(NA_HEADS)]

    def context_tile():
        def scores(head):
            return _dot_nt(masked_queries(head), k_ref[0, 0:CTX, cols(head)])

        def finish(head, s):
            p = jnp.exp2(s - jnp.max(s, axis=-1, keepdims=True))
            l = jnp.sum(p, axis=-1, keepdims=True)
            return jnp.dot(p.astype(BF16), v_ref[0, 0:CTX, cols(head)], preferred_element_type=F32) / l
        return head_tasks(scores, finish)

    def latent_tile():
        rb = t - 1
        ws = jnp.clip(NA_RB * rb - NA_WIN_ROWS // 2, 0, ROWS - NA_WR)
        window = pl.ds(pl.multiple_of(CTX + ws * GRID_W, GRID_W), NA_NLOC)
        typ = jnp.where(rb == 0, 0, jnp.where(rb == ROWS // NA_RB - 1, 2, 1))

        def scores(head):
            qm = masked_queries(head)
            return (_dot_nt(qm, k_ref[0, window, cols(head)]) + bias_ref[head, typ],
                    _dot_nt(qm, k_ref[0, 0:CTX, cols(head)]))

        def finish(head, s):
            s_loc, s_ctx = s
            m = jnp.maximum(jnp.max(s_loc, axis=-1, keepdims=True), jnp.max(s_ctx, axis=-1, keepdims=True))
            p_loc = jnp.exp2(s_loc - m)
            p_ctx = jnp.exp2(s_ctx - m)
            l = jnp.sum(p_loc, axis=-1, keepdims=True) + jnp.sum(p_ctx, axis=-1, keepdims=True)
            o = (jnp.dot(p_loc.astype(BF16), v_ref[0, window, cols(head)], preferred_element_type=F32)
                 + jnp.dot(p_ctx.astype(BF16), v_ref[0, 0:CTX, cols(head)], preferred_element_type=F32))
            return o / l
        return head_tasks(scores, finish)

    return build_bias, context_tile, latent_tile


def _rope_tables():
    t = np.arange(SEQ)
    pos = np.stack([t // GRID_W, t % GRID_W]).astype(np.float32)
    n_freq = HD // 4
    inv_freq = (ROPE_THETA ** (-np.arange(n_freq, dtype=np.float32) / n_freq)).astype(np.float32)
    ang = pos[:, :, None] * inv_freq
    cos, sin = np.cos(ang), np.sin(ang)
    cos_h = np.concatenate([cos[0], cos[0], cos[1], cos[1]], axis=-1)
    sin_h = np.concatenate([-sin[0], sin[0], -sin[1], sin[1]], axis=-1)
    cos_all = np.concatenate([np.ones((CTX, HD), np.float32), cos_h], axis=0)
    sin_all = np.concatenate([np.zeros((CTX, HD), np.float32), sin_h], axis=0)
    return (np.tile(cos_all, (1, 2)).astype(np.float32), np.tile(sin_all, (1, 2)).astype(np.float32))


def _rope(x, cos, sin):
    quarter = HD // 4
    first = lax.broadcasted_iota(jnp.int32, x.shape, 1) % (2 * quarter) < quarter
    partner = jnp.where(first, pltpu.roll(x, LANES - quarter, axis=1), pltpu.roll(x, quarter, axis=1))
    return x * cos + partner * sin


def _gqa_stages(q_ref, k_ref, v_ref, cos_ref, sin_ref, qg_ref, kg_ref, o_ref, kn_ref):
    t = pl.program_id(1)
    lo = _lo_mask((1, LANES))

    def prepare_keys():
        def prep(i, carry):
            rows = pl.ds(pl.multiple_of(i * TM, TM), TM)
            kx = _head_rms(k_ref[0, rows, :].astype(F32), kg_ref[...])
            kn_ref[rows, :] = _rope(kx, cos_ref[rows, :], sin_ref[rows, :]).astype(BF16)
            return carry
        lax.fori_loop(0, N_TILES, prep, 0)

    def attend(n_keys):
        rows = pl.ds(pl.multiple_of(t * TM, TM), TM)
        cos = cos_ref[rows, :]
        sin = sin_ref[rows, :]
        kn = kn_ref[0:n_keys, :]
        vv = v_ref[0, 0:n_keys, :]

        qn = []
        for j in range(GQA_HEADS // 2):
            qx = _head_rms(q_ref[0, :, j * LANES:(j + 1) * LANES].astype(F32), qg_ref[...])
            qn.append((_rope(qx, cos, sin) * SOFTMAX_SCALE).astype(BF16))
        zero = jnp.zeros_like(qn[0])

        def scores(i):
            j, h = divmod(i, 2)
            return _dot_nt(jnp.where(lo, qn[j], zero) if h == 0 else jnp.where(lo, zero, qn[j]), kn)

        outs = {}

        def done(i, s):
            p = jnp.exp2(s - jnp.max(s, axis=-1, keepdims=True))
            l = jnp.sum(p, axis=-1, keepdims=True)
            outs[i] = jnp.dot(p.astype(BF16), vv, preferred_element_type=F32) / l
            if i % 2:
                j = i // 2
                o_ref[0, :, j * LANES:(j + 1) * LANES] = jnp.where(lo, outs[i - 1], outs[i]).astype(o_ref.dtype)
        return [(functools.partial(scores, i), functools.partial(done, i)) for i in range(GQA_HEADS)]

    return prepare_keys, functools.partial(attend, CTX), functools.partial(attend, T_ALL)


def _run_heads(tasks, ahead):
    pending = [issue() for issue, _ in tasks[:ahead]]
    for i, (_, finish) in enumerate(tasks):
        s = pending.pop(0)
        if i + ahead < len(tasks):
            pending.append(tasks[i + ahead][0]())
        finish(s)


def _attention_kernel(idx, qa_ref, ka_ref, va_ref, slab_ref, qb_ref, kb_ref, vb_ref, cos_ref, sin_ref, qg_ref, kg_ref,
                      oa_ref, ob_ref, bias_ref, kn_ref):
    t = pl.program_id(1)
    na_bias, na_context, na_latent = _na_stages(idx, qa_ref, ka_ref, va_ref, slab_ref, oa_ref, bias_ref)
    gqa_keys, gqa_context, gqa_latent = _gqa_stages(qb_ref, kb_ref, vb_ref, cos_ref, sin_ref, qg_ref, kg_ref,
                                                    ob_ref, kn_ref)

    @pl.when((pl.program_id(0) == 0) & (t == 0))
    def _():
        na_bias()

    @pl.when(t == 0)
    def _():
        gqa_keys()
        _run_heads(na_context() + gqa_context(), 2)

    @pl.when(t > 0)
    def _():
        _run_heads([task for pair in zip(na_latent(), gqa_latent()) for task in pair], 2)


def _mixers_ab(ya, yb, slabs, cos, sin, qg, kg):
    b = ya.shape[0]
    kcol = WB // LANES
    tile = lambda i, t: (i, t, 0)
    full = lambda i, t: (0, 0)
    return pl.pallas_call(
        functools.partial(_attention_kernel, _na_slab_index()),
        grid=(b, N_TILES),
        in_specs=[pl.BlockSpec((1, TM, WA), tile),
                  pl.BlockSpec((1, T_ALL, WA), lambda i, t: (i, 0, 1)),
                  pl.BlockSpec((1, T_ALL, WA), lambda i, t: (i, 0, 2)),
                  pl.BlockSpec((NA_HEADS, 2 * NA_WIN_ROWS, GRID_W, LANES), lambda i, t: (0, 0, 0, 0)),
                  pl.BlockSpec((1, TM, WB), tile),
                  pl.BlockSpec((1, T_ALL, LANES), lambda i, t: (i, 0, kcol)),
                  pl.BlockSpec((1, T_ALL, LANES), lambda i, t: (i, 0, kcol + 1)),
                  pl.BlockSpec((T_ALL, LANES), full),
                  pl.BlockSpec((T_ALL, LANES), full),
                  pl.BlockSpec((1, LANES), full),
                  pl.BlockSpec((1, LANES), full)],
        out_specs=[pl.BlockSpec((1, TM, WA), tile),
                   pl.BlockSpec((1, TM, WB), tile)],
        out_shape=[jax.ShapeDtypeStruct((b, T_ALL, WA), BF16),
                   jax.ShapeDtypeStruct((b, T_ALL, WB), BF16)],
        scratch_shapes=[pltpu.VMEM((NA_HEADS, NA_TYPES, NA_RB * GRID_W, NA_NLOC), F32),
                        pltpu.VMEM((T_ALL, LANES), BF16)],
        compiler_params=_params("arbitrary", "arbitrary"),
        name="mixers_ab_attention",
    )(ya, ya, ya, slabs, yb, yb, yb, cos, sin, qg, kg)


def _hgrn_consts():
    c, sc = HG_CHUNK, HG_SUB
    t = np.arange(c)
    out = []
    for reverse in (False, True):
        if not reverse:
            cum = t[None, :] <= t[:, None]
            through = (t[None, :] // sc) <= (t[:, None] // sc)
        else:
            cum = t[None, :] >= t[:, None]
            through = (t[None, :] // sc) >= (t[:, None] // sc)
        out.append(np.concatenate([cum, through], axis=0).astype(np.float32))
    return np.stack(out)


def _forget_gate(z, lb):
    ez = jnp.exp(-jnp.abs(z))
    big = 1.0 / (1.0 + ez)
    small = ez * big
    pos = z >= 0.0
    lb_floor = jnp.maximum(lb, LB_FLOOR)
    f = lb_floor + (1.0 - lb) * jnp.where(pos, big, small)
    one_minus_f = (1.0 - lb) * jnp.where(pos, small, big) - (lb_floor - lb)
    return jnp.log(f), one_minus_f


def _hgrn_kernel(zf_ref, zb_ref, v_ref, q_ref, g_ref, lb_ref, gn_ref, tri_ref, ones_ref,
                 o_ref, acc_ref, st_ref, d_ref, b_ref):
    c, sc = HG_CHUNK, HG_SUB
    n_sub = c // sc
    n_pair = WC // LANES
    acc_ref[...] = jnp.zeros_like(acc_ref)
    st_ref[...] = jnp.zeros_like(st_ref)
    sub_row = lax.broadcasted_iota(jnp.int32, (sc, 1), 0)
    blk = (lax.broadcasted_iota(jnp.int32, (LANES, LANES), 0) // HD
           == lax.broadcasted_iota(jnp.int32, (LANES, LANES), 1) // HD)
    own_head = (lax.broadcasted_iota(jnp.int32, (HG_HEADS * c, WC), 0) // c
                == lax.broadcasted_iota(jnp.int32, (HG_HEADS * c, WC), 1) // HD)
    q_sub = lax.broadcasted_iota(jnp.int32, (c, WC), 0) // sc
    k_sub = (lax.broadcasted_iota(jnp.int32, (c, WC), 1) % c) // sc
    k_lane = lax.broadcasted_iota(jnp.int32, (1, WC), 1) % c

    z_refs = (zf_ref, zb_ref)
    n_ctx = CTX // c
    n_all = T_ALL // c

    def chunk_of(d, n):
        return n if d == 0 else jnp.where(n < n_ctx, n_ctx - 1 - n, n_all + n_ctx - 1 - n)

    def rows_of(d, n):
        return pl.ds(pl.multiple_of(chunk_of(d, n) * c, c), c)

    def gates(d, n, slot):
        g, kk = _forget_gate(z_refs[d][0, rows_of(d, n), :], lb_ref[d:d + 1, :])
        cums = _dot_01(tri_ref[d], g * LOG2E)
        b_ref[d, slot, 0] = cums[0:c]
        b_ref[d, slot, 1] = kk
        b_ref[d, slot, 2] = cums[c:2 * c]

    def state_and_far(d, n, slot):
        reverse = d == 1
        rows = rows_of(d, n)
        b = b_ref[d, slot, 0]
        kk = b_ref[d, slot, 1]
        e = b_ref[d, slot, 2]
        q = q_ref[0, rows, :] * ATT_SCALE
        vb = v_ref[0, rows, :].astype(BF16)
        last = c - 1 if not reverse else 0
        b_last = b_ref[d, slot, 0, last:last + 1, :]

        qe = (q * jnp.exp2(b)).astype(BF16)
        kdec = (kk * jnp.exp2(b_last - b)).astype(BF16)
        o_parts = []
        for p in range(n_pair):
            ln = slice(p * LANES, (p + 1) * LANES)
            st = st_ref[d, p]
            o_parts.append(_dot_nt(qe[:, ln], st.astype(BF16)))
            upd = _dot_tn(vb[:, ln], kdec[:, ln])
            st_ref[d, p] = jnp.where(blk, st * jnp.exp2(b_last[:, ln]) + upd, 0.0)
        o = jnp.concatenate(o_parts, axis=1)

        kx = (kk * jnp.exp2(e - b)).astype(BF16)
        keys = jnp.where(own_head, jnp.concatenate([kx] * HG_HEADS, axis=0), jnp.zeros((), BF16))
        vals = jnp.where(own_head, jnp.concatenate([vb] * HG_HEADS, axis=0), jnp.zeros((), BF16))
        q_stack = []
        for j in range(n_sub - 1):
            jj = j if not reverse else n_sub - 1 - j
            edge = (jj + 1) * sc - 1 if not reverse else jj * sc
            e_j = b_ref[d, slot, 0, edge:edge + 1, :]
            q_stack.append((q * jnp.exp2(jnp.minimum(b - e_j, 0.0))).astype(BF16))
        res = _dot_nt(jnp.concatenate(q_stack, axis=0), keys)
        att = jnp.zeros((c, WC), F32)
        for j in range(n_sub - 1):
            jj = j if not reverse else n_sub - 1 - j
            later = q_sub > jj if not reverse else q_sub < jj
            att = att + jnp.where((k_sub == jj) & later, res[j * c:(j + 1) * c, :], 0.0)
        return dict(rows=rows, b=b, q=q, o=o, att=att, vals=vals)

    def near_products(d, slot, w):
        dk = d * 2 + slot % 2
        reverse = d == 1
        for i in range(n_sub):
            bi = w['b'][i * sc:(i + 1) * sc, :]
            qi = w['q'][i * sc:(i + 1) * sc, :]
            for s in range(sc):
                r = i * sc + s
                p = jnp.exp2(bi - b_ref[d, slot, 0, r:r + 1, :]) * (qi * b_ref[d, slot, 1, r:r + 1, :])
                keep = sub_row >= s if not reverse else sub_row <= s
                d_ref[dk, r * sc:(r + 1) * sc, :] = jnp.where(keep, p, 0.0).astype(BF16)
        w['sums'] = jnp.dot(d_ref[dk], ones_ref[...], preferred_element_type=F32)

    def finish(d, w):
        near = []
        for i in range(n_sub):
            a = jnp.zeros((sc, WC), F32)
            for s in range(sc):
                r = i * sc + s
                a = jnp.where(k_lane == r, w['sums'][r * sc:(r + 1) * sc, :], a)
            near.append(a)
        att = w['att'] + jnp.concatenate(near, axis=0)
        acc_ref[w['rows'], :] += w['o'] + jnp.dot(att.astype(BF16), w['vals'], preferred_element_type=F32)

    def two_steps(n0, slots, next_slots):
        streams = [(k, d) for k in range(2) for d in range(2)]
        for k, d in streams:
            gates(d, jnp.minimum(n0 + 2 + k, n_all - 1), next_slots[k])
        work = {(k, d): state_and_far(d, n0 + k, slots[k]) for k, d in streams}
        for k, d in streams:
            near_products(d, slots[k], work[k, d])
        for k, d in streams:
            finish(d, work[k, d])

    for k in range(2):
        for d in range(2):
            gates(d, k, k)

    def body(m, carry):
        two_steps(4 * m, (0, 1), (2, 3))
        two_steps(4 * m + 2, (2, 3), (0, 1))
        return carry
    lax.fori_loop(0, n_all // 4, body, 0)

    def readout(i, carry):
        rows = pl.ds(pl.multiple_of(i * TM, TM), TM)
        gate = _silu(g_ref[0, rows, :])
        for p in range(n_pair):
            ln = slice(p * LANES, (p + 1) * LANES)
            y = _head_rms(acc_ref[rows, ln], gn_ref[...]) * gate[:, ln]
            o_ref[0, rows, ln] = y.astype(o_ref.dtype)
        return carry
    lax.fori_loop(0, N_TILES, readout, 0)


def _mixer_c(yc, lb, gn, tri, ones):
    b = yc.shape[0]
    c = HG_CHUNK
    col = lambda j: (lambda i: (i, 0, j))
    return pl.pallas_call(
        _hgrn_kernel,
        grid=(b,),
        in_specs=[pl.BlockSpec((1, T_ALL, WC), col(0)),
                  pl.BlockSpec((1, T_ALL, WC), col(1)),
                  pl.BlockSpec((1, T_ALL, WC), col(2)),
                  pl.BlockSpec((1, T_ALL, WC), col(3)),
                  pl.BlockSpec((1, T_ALL, WC), col(4)),
                  pl.BlockSpec((2, WC), lambda i: (0, 0)),
                  pl.BlockSpec((1, LANES), lambda i: (0, 0)),
                  pl.BlockSpec((2, 2 * c, c), lambda i: (0, 0, 0)),
                  pl.BlockSpec((WC, WC), lambda i: (0, 0))],
        out_specs=pl.BlockSpec((1, T_ALL, WC), lambda i: (i, 0, 0)),
        out_shape=jax.ShapeDtypeStruct((b, T_ALL, WC), BF16),
        scratch_shapes=[pltpu.VMEM((T_ALL, WC), F32),
                        pltpu.VMEM((2, WC // LANES, LANES, LANES), F32),
                        pltpu.VMEM((4, c * HG_SUB, WC), BF16),
                        pltpu.VMEM((2, 4, 3, c, WC), F32)],
        compiler_params=_params("parallel"),
        name="mixer_c_hgrn2",
    )(yc, yc, yc, yc, yc, lb, gn, tri, ones)


def _outproj_ffn_kernel(n_sub, t0, ap_ref, a_ref, an_ref, bp_ref, b_ref, bn_ref, cp_ref, c_ref, cn_ref,
                        xp_ref, x_ref, xn_ref, mod_ref, wout_ref, g1_ref, b1_ref, wup_ref, cw_ref, cb_ref, wdn_ref,
                        g2_ref, b2_ref, o_ref):
    step = pl.program_id(1)
    n_e = HALO + TM + HALO
    n_chunk = D_FF // FF_CHUNK

    def with_halo(prev_ref, ref, next_ref, j, halo):
        lo, hi = j * TM - halo, (j + 1) * TM + halo
        parts = [prev_ref[0]] if j == 0 else []
        parts.append(ref[0, max(lo, 0):min(hi, n_sub * TM), :])
        if j == n_sub - 1:
            parts.append(next_ref[0])
        return parts[0] if len(parts) == 1 else jnp.concatenate(parts, axis=0)

    def prologue(j):
        tile = step * n_sub + j + t0
        mods = mod_ref[0, 1] if j > 0 else jnp.where(tile == 0, mod_ref[0, 0], mod_ref[0, 1])
        mix = jnp.concatenate([with_halo(p, m, n, j, MIX_HALO) for p, m, n in
                               ((ap_ref, a_ref, an_ref), (bp_ref, b_ref, bn_ref), (cp_ref, c_ref, cn_ref))], axis=1)
        y = jnp.dot(mix, wout_ref[...], preferred_element_type=F32)[MIX_HALO - HALO:MIX_HALO + TM + HALO, :]
        xe = with_halo(xp_ref, x_ref, xn_ref, j, HALO)
        x1e = _ln(ALPHA * xe + mods[2:3, :] * y) * g1_ref[...] + b1_ref[...]
        h = _ln(x1e) * (1.0 + mods[4:5, :]) + mods[3:4, :]
        r = lax.broadcasted_iota(jnp.int32, (n_e, 1), 0)
        keep = ((r >= HALO) | (tile >= 2)) & ((r < HALO + TM) | ((tile >= 1) & (tile < N_TILES - 1)))
        return dict(h=jnp.where(keep, h, 0.0).astype(BF16), x1=x1e[HALO:HALO + TM, :], gate=mods[5:6, :])

    def epilogue(j, w):
        z = _ln(ALPHA * w['x1'] + w['gate'] * w['acc'])
        o_ref[0, j * TM:(j + 1) * TM, :] = z * g2_ref[...] + b2_ref[...]

    def up(h, c, base):
        return jnp.dot(h, wup_ref[:, base + c * FF_CHUNK:base + (c + 1) * FF_CHUNK], preferred_element_type=F32)

    def conv(u, c, base):
        cols = slice(base + c * FF_CHUNK, base + (c + 1) * FF_CHUNK)
        u = (pltpu.roll(u, 1, axis=0) * cw_ref[0:1, cols] + u * cw_ref[1:2, cols]
             + pltpu.roll(u, n_e - 1, axis=0) * cw_ref[2:3, cols])
        return u[HALO:HALO + TM, :] + cb_ref[:, cols]

    def hidden_chunks(w, hooks):
        ahead = 2
        acc = jnp.zeros((TM, D), F32)
        pending = [(up(w['h'], c, 0), up(w['h'], c, D_FF)) for c in range(ahead)]
        for c in range(n_chunk):
            u_gate, u_val = pending.pop(0)
            if c + ahead < n_chunk:
                pending.append((up(w['h'], c + ahead, 0), up(w['h'], c + ahead, D_FF)))
            if c in hooks:
                hooks[c]()
            a = (_silu(conv(u_gate, c, 0)) * conv(u_val, c, D_FF)).astype(BF16)
            acc = acc + jnp.dot(a, wdn_ref[c * FF_CHUNK:(c + 1) * FF_CHUNK, :], preferred_element_type=F32)
        w['acc'] = acc

    work = [None] * n_sub
    work[0] = prologue(0)
    for j in range(n_sub):
        hooks = {}
        if j > 0:
            hooks[0] = functools.partial(epilogue, j - 1, work[j - 1])
        if j + 1 < n_sub:
            hooks[n_chunk // 2] = functools.partial(lambda k: work.__setitem__(k, prologue(k)), j + 1)
        hidden_chunks(work[j], hooks)
    epilogue(n_sub - 1, work[n_sub - 1])


def _outproj_ffn(oa, ob, oc, xs, modsel, w_out_p, ln1_g, ln1_b, w_up, conv_w, conv_b, w_down, ln2_g, ln2_b, t0):
    b = xs.shape[0]
    n_sub = FFN_TILES_PER_STEP if t0 == 0 else 1
    assert (N_TILES - t0) % n_sub == 0
    rows = n_sub * TM
    full = lambda i, s: (0, 0)

    def halos(width, halo):
        per, last = rows // halo, T_ALL // halo - 1
        first = t0 * TM // halo
        return (pl.BlockSpec((1, halo, width), lambda i, s: (i, jnp.maximum(first + s * per - 1, first), 0)),
                pl.BlockSpec((1, rows, width), lambda i, s: (i, s + t0, 0)),
                pl.BlockSpec((1, halo, width), lambda i, s: (i, jnp.minimum(first + (s + 1) * per, last), 0)))

    return pl.pallas_call(
        functools.partial(_outproj_ffn_kernel, n_sub, t0),
        grid=(b, (N_TILES - t0) // n_sub),
        in_specs=[*halos(WA, MIX_HALO), *halos(WB, MIX_HALO), *halos(WC, MIX_HALO), *halos(D, HALO),
                  pl.BlockSpec((1, 2, SUBLANES, D), lambda i, s: (i, 0, 0, 0)),
                  pl.BlockSpec((D, D), full),
                  pl.BlockSpec((1, D), full),
                  pl.BlockSpec((1, D), full),
                  pl.BlockSpec((D, 2 * D_FF), full),
                  pl.BlockSpec((3, 2 * D_FF), full),
                  pl.BlockSpec((1, 2 * D_FF), full),
                  pl.BlockSpec((D_FF, D), full),
                  pl.BlockSpec((1, D), full),
                  pl.BlockSpec((1, D), full)],
        out_specs=pl.BlockSpec((1, rows, D), lambda i, s: (i, s, 0)),
        out_shape=jax.ShapeDtypeStruct((b, T_ALL - t0 * TM, D), F32),
        compiler_params=_params("parallel", "arbitrary"),
        name="outproj_ffn",
    )(oa, oa, oa, ob, ob, ob, oc, oc, oc, xs, xs, xs, modsel, w_out_p, ln1_g, ln1_b,
      w_up, conv_w, conv_b, w_down, ln2_g, ln2_b)


def _in_col_perm():
    off, o = {}, 0
    for name, width in (('a_k', WA), ('a_v', WA), ('b_k', WKV), ('b_v', WKV), ('c_f', WC), ('c_b', WC),
                        ('c_i', WC), ('a_q', WA), ('b_q', WB), ('c_q', WC), ('c_g', WC)):
        off[name] = np.arange(o, o + width)
        o += width
    bq = off['b_q'].reshape(GQA_HEADS, HD)[list(GQA_ORDER)].reshape(-1)
    return np.concatenate([off['a_q'], off['a_k'], off['a_v'], bq, off['b_k'], off['b_v'],
                           off['c_f'], off['c_b'], off['c_i'], off['c_q'], off['c_g']])


def _out_row_perm():
    ob = (WA + np.arange(WB)).reshape(GQA_HEADS, HD)[list(GQA_ORDER)].reshape(-1)
    return np.concatenate([np.arange(WA), ob, WA + WB + np.arange(WC)])


def _take_runs(w, perm, axis):
    cuts = [0] + [i for i in range(1, len(perm)) if perm[i] != perm[i - 1] + 1] + [len(perm)]
    parts = [lax.slice_in_dim(w, int(perm[a]), int(perm[z - 1]) + 1, axis=axis) for a, z in zip(cuts[:-1], cuts[1:])]
    return jnp.concatenate(parts, axis=axis)


def kernel(x, c, ctx, c_ctx, w_ada, b_ada, w_in, w_out, na_rpb, q_norm, k_norm, hg_lower_bound, hg_norm,
           ln1_g, ln1_b, w_up, conv_w, conv_b, w_down, ln2_g, ln2_b):
    b = x.shape[0]
    assert x.shape == (b, SEQ, D) and ctx.shape == (b, CTX, D)
    n_rows = -(-(b + 1) // SUBLANES) * SUBLANES
    s_in = jnp.concatenate([c, c_ctx[None, :], jnp.zeros((n_rows - b - 1, D), F32)], axis=0)
    mods = _modulation(s_in, w_ada, b_ada).reshape(DEPTH, n_rows, N_MOD, D)
    lbs = _lower_bounds(hg_lower_bound).reshape(DEPTH, 2, WC)

    cos, sin = (jnp.asarray(a) for a in _rope_tables())
    tri = jnp.asarray(_hgrn_consts(), BF16)
    ones = jnp.asarray(np.kron(np.eye(HG_HEADS), np.ones((HD, HD))), BF16)
    in_perm = _in_col_perm()
    out_perm = _out_row_perm()
    tile2 = lambda g: jnp.tile(g.astype(F32), 2).reshape(1, LANES)

    xs = jnp.concatenate([ctx, x], axis=1)
    for l in range(DEPTH):
        last = l == DEPTH - 1
        t0 = 1 if last else 0
        lat = mods[l, :b]
        cx = jnp.broadcast_to(mods[l, b][None], lat.shape)
        modsel = jnp.pad(jnp.stack([cx, lat], axis=1), ((0, 0), (0, 0), (0, SUBLANES - N_MOD), (0, 0)))
        ya, yb, yc = _in_projection(xs, modsel, _take_runs(w_in[l], in_perm, 1).astype(BF16))
        oa, ob = _mixers_ab(ya, yb, _na_slabs(na_rpb[l]), cos, sin, tile2(q_norm[l]), tile2(k_norm[l]))
        oc = _mixer_c(yc, lbs[l], tile2(hg_norm[l]), tri, ones)
        xs = _outproj_ffn(oa, ob, oc, xs, modsel, _take_runs(w_out[l], out_perm, 0).astype(BF16),
                          ln1_g[l].reshape(1, D), ln1_b[l].reshape(1, D), w_up[l].astype(BF16), conv_w[l],
                          conv_b[l].reshape(1, 2 * D_FF), w_down[l].astype(BF16),
                          ln2_g[l].reshape(1, D), ln2_b[l].reshape(1, D), t0)
    return xs
```
